```python
import math
import jax
import jax.numpy as jnp
from jax import lax
import numpy as np

D_MODEL = 1024
BATCH = 8
SEQ = 2048
DEPTH = 4
DEC_BATCH = 32
DEC_SEQ = 4
PAST_LEN = 8192
PAGE_SIZE = 128

N_MIXERS = 3
N_META = 16
RMS_EPS = 1e-6
D_FF = 4 * D_MODEL

HEAD_A = 64
H_A = D_MODEL // (2 * HEAD_A)
ROT_DIM = HEAD_A // 4
ROPE_THETA = 500000.0
Q_BLOCK = 128

H_B = 4
KD_B = D_MODEL // 2
VD_B = D_MODEL
DK_B = KD_B // H_B
DV_B = VD_B // H_B
GATE_RANK = 16
GATE_TEMP = 16.0

EXPAND_C = 128
H_C = D_MODEL // EXPAND_C
DK_C = EXPAND_C
DV_C = D_MODEL // H_C
KD_C = H_C * DK_C
VD_C = H_C * DV_C

CHUNK = 64

N_A = len(range(0, DEPTH, N_MIXERS))
N_B = len(range(1, DEPTH, N_MIXERS))
N_C = len(range(2, DEPTH, N_MIXERS))

kernel_name = 'hybrid_diffattn_gla_hgrn2_step'


def rmsnorm(x, w):
    xf = x.astype(jnp.float32)
    y = xf * lax.rsqrt(jnp.mean(xf * xf, axis=-1, keepdims=True) + RMS_EPS)
    return (y * w.astype(jnp.float32)).astype(x.dtype)


def rotary(x, pos):
    half = ROT_DIM // 2
    inv_freq = ROPE_THETA ** (-jnp.arange(half, dtype=jnp.float32) / half)
    ang = pos.astype(jnp.float32)[:, None] * inv_freq[None, :]
    cos = jnp.cos(ang)[None, :, None, :]
    sin = jnp.sin(ang)[None, :, None, :]
    xf = x.astype(jnp.float32)
    x1, x2, rest = xf[..., :half], xf[..., half:ROT_DIM], xf[..., ROT_DIM:]
    return jnp.concatenate([x1 * cos - x2 * sin, x2 * cos + x1 * sin, rest], axis=-1).astype(x.dtype)


def diff_project(u, w_in, pos):
    b, t, _ = u.shape
    q, k, v = jnp.split(u @ w_in, 3, axis=-1)
    q = rotary(q.reshape(b, t, 2 * H_A, HEAD_A), pos)
    k = rotary(k.reshape(b, t, 2 * H_A, HEAD_A), pos)
    return q, k, v.reshape(b, t, H_A, 2 * HEAD_A)


def diff_lambda(lam, layer_idx):
    lam_init = 0.8 - 0.6 * math.exp(-0.3 * layer_idx)
    lf = lam.astype(jnp.float32)
    lam_full = jnp.exp(jnp.sum(lf[0] * lf[1])) - jnp.exp(jnp.sum(lf[2] * lf[3])) + lam_init
    return lam_full, lam_init


def diff_weights(scores, lam):
    p = jax.nn.softmax(scores, axis=-1)
    b, _, nq, nk = p.shape
    p = p.reshape(b, H_A, 2, nq, nk)
    return p[:, :, 0] - lam * p[:, :, 1]


def diff_attn_prompt(q, k, v, lam):
    b, t = q.shape[:2]
    n_blk = -(-t // Q_BLOCK)
    q_pad = jnp.pad(q, ((0, 0), (0, n_blk * Q_BLOCK - t), (0, 0), (0, 0)))
    q_blocks = jnp.moveaxis(q_pad.reshape(b, n_blk, Q_BLOCK, 2 * H_A, HEAD_A), 1, 0)
    starts = jnp.arange(n_blk) * Q_BLOCK
    k_pos = jnp.arange(t)
    scale = HEAD_A ** -0.5

    def one_block(args):
        qb, s0 = args
        sc = jnp.einsum('bqhd,bkhd->bhqk', qb, k).astype(jnp.float32) * scale
        q_pos = s0 + jnp.arange(Q_BLOCK)
        sc = jnp.where(k_pos[None, :] <= q_pos[:, None], sc, -jnp.inf)
        a = diff_weights(sc, lam)
        return jnp.einsum('bhqk,bkhe->bqhe', a.astype(v.dtype), v)

    o = lax.map(one_block, (q_blocks, starts))
    return jnp.moveaxis(o, 0, 1).reshape(b, n_blk * Q_BLOCK, H_A, 2 * HEAD_A)[:, :t]


def diff_attn_sample(q, k_new, v_new, k_past, v_past, lam):
    n_new, n_past = q.shape[1], k_past.shape[1]
    scale = HEAD_A ** -0.5
    sc_past = jnp.einsum('bqhd,bkhd->bhqk', q, k_past).astype(jnp.float32) * scale
    sc_new = jnp.einsum('bqhd,bkhd->bhqk', q, k_new).astype(jnp.float32) * scale
    sc_new = jnp.where(jnp.tril(jnp.ones((n_new, n_new), bool)), sc_new, -jnp.inf)
    a = diff_weights(jnp.concatenate([sc_past, sc_new], axis=-1), lam).astype(v_new.dtype)
    return (jnp.einsum('bhqk,bkhe->bqhe', a[..., :n_past], v_past)
            + jnp.einsum('bhqk,bkhe->bqhe', a[..., n_past:], v_new))


def diff_out(o, lam_init, subln, w_out):
    b, t = o.shape[:2]
    o = rmsnorm(o, subln) * (1.0 - lam_init)
    return o.reshape(b, t, H_A * 2 * HEAD_A) @ w_out


def gated_chunk(state, q, k, v, g):
    L = q.shape[1]
    cum = jnp.cumsum(g, axis=1)
    causal = jnp.tril(jnp.ones((L, L), bool))[None, :, :, None, None]
    decay = jnp.exp(jnp.where(causal, cum[:, :, None] - cum[:, None, :], -jnp.inf))
    scores = jnp.einsum('bthk,bshk,btshk->bhts', q, k, decay)
    o = (jnp.einsum('bthk,bhkv->bthv', q * jnp.exp(cum), state)
         + jnp.einsum('bhts,bshv->bthv', scores, v))
    last = cum[:, -1]
    new_state = (jnp.exp(last)[..., None] * state
                 + jnp.einsum('bshk,bshv->bhkv', k * jnp.exp(last[:, None] - cum), v))
    return new_state, o


def gated_prompt(q, k, v, g):
    b, _, h, dk = q.shape
    dv = v.shape[-1]
    state = jnp.zeros((b, h, dk, dv), jnp.float32)
    state, o_meta = gated_chunk(state, q[:, :N_META], k[:, :N_META], v[:, :N_META], g[:, :N_META])

    def to_chunks(a):
        a = a[:, N_META:]
        return jnp.moveaxis(a.reshape(b, -1, CHUNK, *a.shape[2:]), 1, 0)

    state, o = lax.scan(lambda s, xs: gated_chunk(s, *xs), state,
                        (to_chunks(q), to_chunks(k), to_chunks(v), to_chunks(g)))
    o = jnp.moveaxis(o, 0, 1).reshape(b, -1, h, dv)
    return state, jnp.concatenate([o_meta, o], axis=1)


def gla_prepare(u, w_in, w_gate2, b_gate):
    b, t, _ = u.shape
    q, k, v, r, gl = jnp.split(u @ w_in, [KD_B, 2 * KD_B, 2 * KD_B + VD_B, 2 * KD_B + 2 * VD_B], axis=-1)
    g = jax.nn.log_sigmoid((gl @ w_gate2 + b_gate).astype(jnp.float32)) / GATE_TEMP
    q = q.astype(jnp.float32).reshape(b, t, H_B, DK_B) * DK_B ** -0.5
    k = k.astype(jnp.float32).reshape(b, t, H_B, DK_B)
    v = v.astype(jnp.float32).reshape(b, t, H_B, DV_B)
    return q, k, v, g.reshape(b, t, H_B, DK_B), r


def hgrn_prepare(u, w_in, lb):
    b, t, _ = u.shape
    q, f, i, gate = jnp.split(u @ w_in, [KD_C, 2 * KD_C, 2 * KD_C + VD_C], axis=-1)
    forget = lb + (1.0 - lb) * jax.nn.sigmoid(f.astype(jnp.float32))
    q = jax.nn.silu(q.astype(jnp.float32)).reshape(b, t, H_C, DK_C) * DK_C ** -0.5
    k = (1.0 - forget).reshape(b, t, H_C, DK_C)
    v = i.astype(jnp.float32).reshape(b, t, H_C, DV_C)
    g = jnp.log(forget).reshape(b, t, H_C, DK_C)
    return q, k, v, g, gate


def gated_head_out(o, gate, gnorm, w_out):
    b, t, h, dv = o.shape
    o = rmsnorm(o, gnorm) * jax.nn.silu(gate.astype(jnp.float32)).reshape(b, t, h, dv)
    return o.reshape(b, t, h * dv).astype(w_out.dtype) @ w_out


def recurrent_mixer(up, us, state_in, prepare, gnorm, w_out):
    qp, kp, vp, gp, rp = prepare(up)
    sp, op = gated_prompt(qp, kp, vp, gp)
    qs, ks, vs, gs, rs = prepare(us)
    ss, os_ = gated_chunk(state_in.astype(jnp.float32), qs, ks, vs, gs)
    return gated_head_out(op, rp, gnorm, w_out), gated_head_out(os_, rs, gnorm, w_out), sp, ss


def sq_relu_mlp(x, w_up, w_down):
    return jnp.square(jax.nn.relu(x @ w_up)) @ w_down


def setup_inputs(seed: int = 0) -> dict:
    key = jax.random.key(seed)
    keys = iter(jax.random.split(key, 40))

    def normal(shape, scale):
        return jax.random.normal(next(keys), shape, jnp.float32) * scale

    n_pages = PAST_LEN // PAGE_SIZE
    n_pool = (DEC_BATCH * n_pages * 5) // 4
    perm = jax.random.permutation(next(keys), n_pool)
    page_table = perm[:DEC_BATCH * n_pages].reshape(DEC_BATCH, n_pages).astype(jnp.int32)
    d = D_MODEL
    return {
        'x_prompt': normal((BATCH, SEQ, d), 1.0),
        'x_sample': normal((DEC_BATCH, DEC_SEQ, d), 1.0),
        'cache_k': normal((N_A, n_pool, PAGE_SIZE, 2 * H_A, HEAD_A), 1.0),
        'cache_v': normal((N_A, n_pool, PAGE_SIZE, H_A, 2 * HEAD_A), 1.0),
        'state_gla': normal((N_B, DEC_BATCH, H_B, DK_B, DV_B), 0.3),
        'state_hgrn': normal((N_C, DEC_BATCH, H_C, DK_C, DV_C), 0.3),
        'page_table': page_table,
        'meta_tokens': normal((N_META, d), 1.0),
        'norm_mix': 1.0 + normal((DEPTH, d), 0.02),
        'norm_ffn': 1.0 + normal((DEPTH, d), 0.02),
        'norm_final': 1.0 + normal((d,), 0.02),
        'w_in_a': normal((N_A, d, 3 * d), d ** -0.5),
        'lam_a': normal((N_A, 4, HEAD_A), 0.1),
        'subln_a': 1.0 + normal((N_A, 2 * HEAD_A), 0.02),
        'w_out_a': normal((N_A, d, d), d ** -0.5),
        'w_in_b': normal((N_B, d, 2 * KD_B + 2 * VD_B + GATE_RANK), d ** -0.5),
        'w_gate2_b': normal((N_B, GATE_RANK, KD_B), GATE_RANK ** -0.5),
        'b_gate_b': normal((N_B, KD_B), 0.1),
        'gnorm_b': 1.0 + normal((N_B, DV_B), 0.02),
        'w_out_b': normal((N_B, VD_B, d), VD_B ** -0.5),
        'w_in_c': normal((N_C, d, 2 * KD_C + 2 * VD_C), d ** -0.5),
        'lb_c': normal((DEPTH, KD_C), 0.1),
        'gnorm_c': 1.0 + normal((N_C, DV_C), 0.02),
        'w_out_c': normal((N_C, VD_C, d), VD_C ** -0.5),
        'w_up': normal((DEPTH, d, D_FF), d ** -0.5),
        'w_down': normal((DEPTH, D_FF, d), D_FF ** -0.5),
    }


def reference(x_prompt, x_sample, cache_k, cache_v, state_gla, state_hgrn, page_table,
              meta_tokens, norm_mix, norm_ffn, norm_final,
              w_in_a, lam_a, subln_a, w_out_a,
              w_in_b, w_gate2_b, b_gate_b, gnorm_b, w_out_b,
              w_in_c, lb_c, gnorm_c, w_out_c,
              w_up, w_down):
    bp, seq = x_prompt.shape[:2]
    db, n_new = x_sample.shape[:2]
    past_len = page_table.shape[1] * cache_k.shape[2]
    dtp, dts = x_prompt.dtype, x_sample.dtype
    meta = jnp.broadcast_to(meta_tokens.astype(dtp)[None], (bp, N_META, D_MODEL))
    hp = jnp.concatenate([meta, x_prompt], axis=1)
    hs = x_sample
    pos_p = jnp.arange(N_META + seq)
    pos_s = past_len + jnp.arange(n_new)
    lb_soft = jax.nn.softmax(lb_c.astype(jnp.float32), axis=0)
    lb_all = jnp.cumsum(lb_soft, axis=0) - lb_soft[0]

    kp_rows, vp_rows, ks_rows, vs_rows = [], [], [], []
    gla_p, gla_s, hgrn_p, hgrn_s = [], [], [], []
    for i in range(DEPTH):
        kind, j = i % N_MIXERS, i // N_MIXERS
        up = rmsnorm(hp, norm_mix[i])
        us = rmsnorm(hs, norm_mix[i])
        if kind == 0:
            lam, lam_init = diff_lambda(lam_a[j], i)
            qp, kp, vp = diff_project(up, w_in_a[j], pos_p)
            qs, ks, vs = diff_project(us, w_in_a[j], pos_s)
            k_past = cache_k[j, page_table].reshape(db, past_len, 2 * H_A, HEAD_A).astype(ks.dtype)
            v_past = cache_v[j, page_table].reshape(db, past_len, H_A, 2 * HEAD_A).astype(vs.dtype)
            mp = diff_out(diff_attn_prompt(qp, kp, vp, lam), lam_init, subln_a[j], w_out_a[j])
            ms = diff_out(diff_attn_sample(qs, ks, vs, k_past, v_past, lam), lam_init, subln_a[j], w_out_a[j])
            kp_rows.append(kp)
            vp_rows.append(vp)
            ks_rows.append(ks)
            vs_rows.append(vs)
        elif kind == 1:
            mp, ms, sp, ss = recurrent_mixer(
                up, us, state_gla[j],
                lambda u: gla_prepare(u, w_in_b[j], w_gate2_b[j], b_gate_b[j]),
                gnorm_b[j], w_out_b[j])
            gla_p.append(sp.astype(dtp))
            gla_s.append(ss.astype(dts))
        else:
            mp, ms, sp, ss = recurrent_mixer(
                up, us, state_hgrn[j],
                lambda u: hgrn_prepare(u, w_in_c[j], lb_all[i]),
                gnorm_c[j], w_out_c[j])
            hgrn_p.append(sp.astype(dtp))
            hgrn_s.append(ss.astype(dts))
        hp = hp + mp.astype(dtp)
        hs = hs + ms.astype(dts)
        hp = hp + sq_relu_mlp(rmsnorm(hp, norm_ffn[i]), w_up[i], w_down[i]).astype(dtp)
        hs = hs + sq_relu_mlp(rmsnorm(hs, norm_ffn[i]), w_up[i], w_down[i]).astype(dts)

    y_prompt = rmsnorm(hp, norm_final)[:, N_META:]
    y_sample = rmsnorm(hs, norm_final)
    return (y_prompt, y_sample,
            jnp.stack(kp_rows), jnp.stack(vp_rows), jnp.stack(ks_rows), jnp.stack(vs_rows),
            jnp.stack(gla_p), jnp.stack(gla_s), jnp.stack(hgrn_p), jnp.stack(hgrn_s))
```

```python
import functools
import math

import numpy as np
import jax
import jax.numpy as jnp
from jax import lax
from jax.experimental import pallas as pl
from jax.experimental.pallas import tpu as pltpu

F32 = jnp.float32
BF16 = jnp.bfloat16

RMS_EPS = 1e-6
N_META_TOKENS = 16
N_MIXERS = 3
HEAD_A = 64
ROT_DIM = HEAD_A // 4
ROPE_THETA = 500000.0
H_B = 4
GATE_RANK = 16
GATE_TEMP = 16.0
DK = 128
SAMPLE_ROWS = 8
SMALL_CHUNK = 16
MAIN_CHUNK = 64
LANES = 128
VMEM_LIMIT = 56 * 1024 * 1024

NT_DIMS = (((1,), (1,)), ((), ()))
TN_DIMS = (((0,), (0,)), ((), ()))


def _cparams(*sem):
    return pltpu.CompilerParams(dimension_semantics=sem, vmem_limit_bytes=VMEM_LIMIT)


def _pick_tile(n, target, mult=16):
    best = None
    for t in range(mult, min(n, target) + 1, mult):
        if n % t == 0:
            best = t
    return n if best is None else best


def _rms(x, w):
    ms = jnp.mean(x * x, axis=-1, keepdims=True)
    return x * lax.rsqrt(ms + RMS_EPS) * w


def _sigmoid(x):
    return 1.0 / (1.0 + jnp.exp(-x))


def _proj_a_kernel(x_ref, nw_ref, w_ref, c_ref, sm_ref, sp_ref, q_ref, k_ref, v_ref, *, cw):
    xn = _rms(x_ref[...], nw_ref[...]).astype(BF16)
    d = q_ref.shape[1]
    rep = cw // LANES
    cos = jnp.tile(c_ref[...], (1, rep))
    sin_m = jnp.tile(sm_ref[...], (1, rep))
    sin_p = jnp.tile(sp_ref[...], (1, rep))
    for dst_i, dst in enumerate((q_ref, k_ref, v_ref)):
        for c in range(d // cw):
            col = dst_i * d + c * cw
            y = jnp.dot(xn, w_ref[:, col:col + cw], preferred_element_type=F32)
            if dst_i < 2:
                y = (y * cos + pltpu.roll(y, ROT_DIM // 2, 1) * sin_p
                     + pltpu.roll(y, cw - ROT_DIM // 2, 1) * sin_m)
            dst[:, c * cw:(c + 1) * cw] = y


def _proj_b_kernel(x_ref, nw_ref, w_ref, wgl_ref, wg2_ref, bg_ref,
                   q_ref, k_ref, v_ref, r_ref, g_ref, *, cw):
    xn = _rms(x_ref[...], nw_ref[...]).astype(BF16)
    kd = q_ref.shape[1]
    col = 0
    for dst, scale in ((q_ref, DK ** -0.5), (k_ref, None), (v_ref, None), (r_ref, None)):
        for c in range(dst.shape[1] // cw):
            y = jnp.dot(xn, w_ref[:, col:col + cw], preferred_element_type=F32)
            if scale is not None:
                y = y * scale
            dst[:, c * cw:(c + 1) * cw] = y
            col += cw
    gl = jnp.dot(xn, wgl_ref[...], preferred_element_type=F32).astype(BF16)
    z = jnp.dot(gl, wg2_ref[...], preferred_element_type=F32) + bg_ref[...]
    g_ref[...] = (jnp.minimum(z, 0.0) - jnp.log(1.0 + jnp.exp(-jnp.abs(z)))) * (1.0 / GATE_TEMP)
    del kd


def _proj_c_kernel(x_ref, nw_ref, w_ref, lb_ref, q_ref, k_ref, g_ref, v_ref, r_ref, *, cw, layer):
    xn = _rms(x_ref[...], nw_ref[...]).astype(BF16)
    d = q_ref.shape[1]
    lbs = lb_ref[...]
    mx = jnp.max(lbs, axis=0, keepdims=True)
    e = jnp.exp(lbs - mx)
    lb = jnp.sum(e[1:layer + 1], axis=0, keepdims=True) / jnp.sum(e, axis=0, keepdims=True)
    for c in range(d // cw):
        sl = slice(c * cw, (c + 1) * cw)
        y = jnp.dot(xn, w_ref[:, c * cw:(c + 1) * cw], preferred_element_type=F32)
        q_ref[:, sl] = y * _sigmoid(y) * (DK ** -0.5)
        f = jnp.dot(xn, w_ref[:, d + c * cw:d + (c + 1) * cw], preferred_element_type=F32)
        lbc = lb[:, sl]
        forget = lbc + (1.0 - lbc) * _sigmoid(f)
        k_ref[:, sl] = 1.0 - forget
        g_ref[:, sl] = jnp.log(forget)
        v_ref[:, sl] = jnp.dot(xn, w_ref[:, 2 * d + c * cw:2 * d + (c + 1) * cw],
                               preferred_element_type=F32)
        r_ref[:, sl] = jnp.dot(xn, w_ref[:, 3 * d + c * cw:3 * d + (c + 1) * cw],
                               preferred_element_type=F32)


def _row_spec(tm, width):
    return pl.BlockSpec((tm, width), lambda i: (i, 0))


def _full_spec(shape):
    return pl.BlockSpec(shape, lambda i: (0,) * len(shape))


def _proj_a(x, nw, w, tables, tm, table_blocks):
    n, d = x.shape
    cw = min(512, d)
    tab_spec = pl.BlockSpec((tm, LANES), lambda i: (i % table_blocks, 0))
    out = jax.ShapeDtypeStruct((n, d), F32)
    return pl.pallas_call(
        functools.partial(_proj_a_kernel, cw=cw),
        out_shape=(out, out, out),
        grid=(n // tm,),
        in_specs=[_row_spec(tm, d), _full_spec((1, d)), _full_spec(w.shape),
                  tab_spec, tab_spec, tab_spec],
        out_specs=(_row_spec(tm, d),) * 3,
        compiler_params=_cparams("parallel"),
        name="proj_a",
    )(x, nw, w, *tables)


def _proj_b(x, nw, w, wgl, wg2, bg, tm):
    n, d = x.shape
    kd, vd = wg2.shape[1], d
    cw = min(512, kd)
    return pl.pallas_call(
        functools.partial(_proj_b_kernel, cw=cw),
        out_shape=(jax.ShapeDtypeStruct((n, kd), F32), jax.ShapeDtypeStruct((n, kd), F32),
                   jax.ShapeDtypeStruct((n, vd), F32), jax.ShapeDtypeStruct((n, vd), F32),
                   jax.ShapeDtypeStruct((n, kd), F32)),
        grid=(n // tm,),
        in_specs=[_row_spec(tm, d), _full_spec((1, d)), _full_spec(w.shape),
                  _full_spec(wgl.shape), _full_spec(wg2.shape), _full_spec((1, kd))],
        out_specs=(_row_spec(tm, kd), _row_spec(tm, kd), _row_spec(tm, vd), _row_spec(tm, vd),
                   _row_spec(tm, kd)),
        compiler_params=_cparams("parallel"),
        name="proj_b",
    )(x, nw, w, wgl, wg2, bg)


def _proj_c(x, nw, w, lb_c, layer, tm):
    n, d = x.shape
    cw = min(512, d)
    out = jax.ShapeDtypeStruct((n, d), F32)
    return pl.pallas_call(
        functools.partial(_proj_c_kernel, cw=cw, layer=layer),
        out_shape=(out,) * 5,
        grid=(n // tm,),
        in_specs=[_row_spec(tm, d), _full_spec((1, d)), _full_spec(w.shape),
                  _full_spec(lb_c.shape)],
        out_specs=(_row_spec(tm, d),) * 5,
        compiler_params=_cparams("parallel"),
        name="proj_c",
    )(x, nw, w, lb_c)


def _lambda_full(lam_ref, lam_init):
    l = lam_ref[...]
    a = jnp.sum(l[0:1] * l[1:2], axis=-1, keepdims=True)
    b = jnp.sum(l[2:3] * l[3:4], axis=-1, keepdims=True)
    return jnp.exp(a) - jnp.exp(b) + lam_init


def _head_finish(o, subln, lam_init):
    ms = jnp.mean(o * o, axis=-1, keepdims=True)
    return o * lax.rsqrt(ms + RMS_EPS) * subln * (1.0 - lam_init)


def _attn_prompt_kernel(lam_ref, subln_ref, q_ref, k_ref, v_ref, o_ref, kb_ref, vb_ref,
                        *, tq, n_meta, lam_init):
    t_all = q_ref.shape[0]
    nq = (t_all - n_meta) // tq
    kb_ref[...] = k_ref[...].astype(BF16)
    vb_ref[...] = v_ref[...].astype(BF16)
    lam = _lambda_full(lam_ref, lam_init)
    subln = subln_ref[...]
    first_map = lax.broadcasted_iota(jnp.int32, (1, 2 * HEAD_A), 1) < HEAD_A

    def stack_q(qt):
        qt = qt * (HEAD_A ** -0.5)
        return jnp.concatenate([jnp.where(first_map, qt, 0.0), jnp.where(first_map, 0.0, qt)],
                               axis=0).astype(BF16)

    def finish(acc, l, n):
        o = acc[:n] / l[:n] - lam * (acc[n:] / l[n:])
        return _head_finish(o, subln, lam_init).astype(o_ref.dtype)

    def causal(n, width):
        r = lax.broadcasted_iota(jnp.int32, (2 * n, width), 0)
        c = lax.broadcasted_iota(jnp.int32, (2 * n, width), 1)
        return c <= jnp.where(r >= n, r - n, r)

    qs = stack_q(q_ref[0:n_meta])
    s = lax.dot_general(qs, kb_ref[0:n_meta], NT_DIMS, preferred_element_type=F32)
    s = jnp.where(causal(n_meta, n_meta), s, -jnp.inf)
    m = jnp.max(s, axis=-1, keepdims=True)
    p = jnp.exp(s - m)
    acc = jnp.dot(p.astype(BF16), vb_ref[0:n_meta], preferred_element_type=F32)
    o_ref[0:n_meta] = finish(acc, jnp.sum(p, axis=-1, keepdims=True), n_meta)

    diag_mask = causal(tq, tq)

    def online(carry, s, vblk):
        m, l, acc = carry
        m_new = jnp.maximum(m, jnp.max(s, axis=-1, keepdims=True))
        alpha = jnp.exp(m - m_new)
        p = jnp.exp(s - m_new)
        l = alpha * l + jnp.sum(p, axis=-1, keepdims=True)
        acc = alpha * acc + jnp.dot(p.astype(BF16), vblk, preferred_element_type=F32)
        return m_new, l, acc

    def q_body(qi, _):
        q0 = pl.multiple_of(n_meta + qi * tq, 16)
        qs = stack_q(q_ref[pl.ds(q0, tq)])
        s = lax.dot_general(qs, kb_ref[0:n_meta], NT_DIMS, preferred_element_type=F32)
        m = jnp.max(s, axis=-1, keepdims=True)
        p = jnp.exp(s - m)
        l = jnp.sum(p, axis=-1, keepdims=True)
        acc = jnp.dot(p.astype(BF16), vb_ref[0:n_meta], preferred_element_type=F32)

        def kv_body(ki, carry):
            k0 = pl.multiple_of(n_meta + ki * tq, 16)
            s = lax.dot_general(qs, kb_ref[pl.ds(k0, tq)], NT_DIMS, preferred_element_type=F32)
            return online(carry, s, vb_ref[pl.ds(k0, tq)])

        carry = lax.fori_loop(0, qi, kv_body, (m, l, acc))
        s = lax.dot_general(qs, kb_ref[pl.ds(q0, tq)], NT_DIMS, preferred_element_type=F32)
        s = jnp.where(diag_mask, s, -jnp.inf)
        m, l, acc = online(carry, s, vb_ref[pl.ds(q0, tq)])
        o_ref[pl.ds(q0, tq)] = finish(acc, l, tq)
        return 0

    lax.fori_loop(0, nq, q_body, 0)


def _attn_prompt(q, k, v, lam, subln, lam_init, n_meta):
    b, t, d = q.shape
    hw = 2 * HEAD_A
    tq = _pick_tile(t - n_meta, 256)
    blk = pl.BlockSpec((None, t, hw), lambda i, h: (i, 0, h))
    return pl.pallas_call(
        functools.partial(_attn_prompt_kernel, tq=tq, n_meta=n_meta, lam_init=lam_init),
        out_shape=jax.ShapeDtypeStruct((b, t, d), BF16),
        grid=(b, d // hw),
        in_specs=[pl.BlockSpec(lam.shape, lambda i, h: (0, 0)),
                  pl.BlockSpec((1, hw), lambda i, h: (0, 0)), blk, blk, blk],
        out_specs=blk,
        scratch_shapes=[pltpu.VMEM((t, hw), BF16), pltpu.VMEM((t, hw), BF16)],
        compiler_params=_cparams("parallel", "parallel"),
        name="attn_prompt",
    )(lam, subln, q, k, v)


def _attn_sample_kernel(pt_ref, lam_ref, subln_ref, q_ref, kn_ref, vn_ref, *rest,
                        pp, n_new, lam_init):
    del pt_ref
    k_refs, v_refs = rest[:pp], rest[pp:2 * pp]
    o_ref, qbd_ref, m_ref, l_ref, acc_ref = rest[2 * pp:]
    step = pl.program_id(1)
    d = q_ref.shape[1]
    n_heads = d // HEAD_A
    rows = n_heads * SAMPLE_ROWS
    page = k_refs[0].shape[0]

    @pl.when(step == 0)
    def _():
        q = q_ref[...] * (HEAD_A ** -0.5)
        qrep = jnp.concatenate([q] * n_heads, axis=0)
        row_head = lax.broadcasted_iota(jnp.int32, (rows, d), 0) // SAMPLE_ROWS
        lane_head = lax.broadcasted_iota(jnp.int32, (rows, d), 1) // HEAD_A
        qbd = jnp.where(row_head == lane_head, qrep, 0.0).astype(BF16)
        qbd_ref[...] = qbd
        pad = jnp.zeros((page - SAMPLE_ROWS, d), F32)
        kn = jnp.concatenate([kn_ref[...], pad], axis=0).astype(BF16)
        vn = jnp.concatenate([vn_ref[...], pad], axis=0).astype(BF16)
        s = lax.dot_general(qbd, kn, NT_DIMS, preferred_element_type=F32)
        tok = lax.broadcasted_iota(jnp.int32, (rows, page), 0) % SAMPLE_ROWS
        key = lax.broadcasted_iota(jnp.int32, (rows, page), 1)
        s = jnp.where(key <= jnp.minimum(tok, n_new - 1), s, -jnp.inf)
        m = jnp.max(s, axis=-1, keepdims=True)
        p = jnp.exp(s - m)
        m_ref[...] = m
        l_ref[...] = jnp.sum(p, axis=-1, keepdims=True)
        acc_ref[...] = jnp.dot(p.astype(BF16), vn, preferred_element_type=F32)

    qbd = qbd_ref[...]
    s = jnp.concatenate(
        [lax.dot_general(qbd, kr[...].astype(BF16), NT_DIMS, preferred_element_type=F32)
         for kr in k_refs], axis=1)
    m_old = m_ref[...]
    m_new = jnp.maximum(m_old, jnp.max(s, axis=-1, keepdims=True))
    alpha = jnp.exp(m_old - m_new)
    p = jnp.exp(s - m_new).astype(BF16)
    l_ref[...] = alpha * l_ref[...] + jnp.sum(p.astype(F32), axis=-1, keepdims=True)
    pv = jnp.dot(p[:, 0:page], v_refs[0][...].astype(BF16), preferred_element_type=F32)
    for i in range(1, pp):
        pv = pv + jnp.dot(p[:, i * page:(i + 1) * page], v_refs[i][...].astype(BF16),
                          preferred_element_type=F32)
    acc_ref[...] = alpha * acc_ref[...] + pv
    m_ref[...] = m_new

    @pl.when(step == pl.num_programs(1) - 1)
    def _():
        lam = _lambda_full(lam_ref, lam_init)
        subln = subln_ref[...]
        hw = 2 * HEAD_A
        for h in range(d // hw):
            r0 = 2 * h * SAMPLE_ROWS
            r1 = r0 + SAMPLE_ROWS
            a1 = acc_ref[r0:r1, h * hw:(h + 1) * hw] / l_ref[r0:r1]
            a2 = acc_ref[r1:r1 + SAMPLE_ROWS, h * hw:(h + 1) * hw] / l_ref[r1:r1 + SAMPLE_ROWS]
            o = a1 - lam * a2
            o_ref[:, h * hw:(h + 1) * hw] = _head_finish(o, subln, lam_init).astype(o_ref.dtype)


def _attn_sample(q, k_new, v_new, cache_k, cache_v, page_table, layer_j, lam, subln, lam_init,
                 n_new, pp):
    n, d = q.shape
    nb, n_pages = page_table.shape
    page = cache_k.shape[2]
    rows = (d // HEAD_A) * SAMPLE_ROWS
    row_blk = pl.BlockSpec((SAMPLE_ROWS, d), lambda b, s, pt: (b, 0))

    def page_spec(i):
        return pl.BlockSpec((None, None, page, d),
                            lambda b, s, pt: (layer_j, pt[b, s * pp + i], 0, 0))

    grid_spec = pltpu.PrefetchScalarGridSpec(
        num_scalar_prefetch=1,
        grid=(nb, n_pages // pp),
        in_specs=[pl.BlockSpec(lam.shape, lambda b, s, pt: (0, 0)),
                  pl.BlockSpec((1, 2 * HEAD_A), lambda b, s, pt: (0, 0)),
                  row_blk, row_blk, row_blk]
                 + [page_spec(i) for i in range(pp)] + [page_spec(i) for i in range(pp)],
        out_specs=row_blk,
        scratch_shapes=[pltpu.VMEM((rows, d), BF16), pltpu.VMEM((rows, 1), F32),
                        pltpu.VMEM((rows, 1), F32), pltpu.VMEM((rows, d), F32)],
    )
    return pl.pallas_call(
        functools.partial(_attn_sample_kernel, pp=pp, n_new=n_new, lam_init=lam_init),
        out_shape=jax.ShapeDtypeStruct((n, d), BF16),
        grid_spec=grid_spec,
        compiler_params=_cparams("parallel", "arbitrary"),
        name="attn_sample",
    )(page_table, lam, subln, q, k_new, v_new, *([cache_k] * pp), *([cache_v] * pp))


def _chunk_constants(length):
    nlev = int(math.log2(length))
    assert 2 ** nlev == length
    w = np.zeros(((2 + nlev) * length, length), np.float32)
    masks = np.zeros((nlev + 1, length, length), np.float32)
    masks[0] = np.eye(length)
    for t in range(length):
        w[t, :t + 1] = 1.0
        w[length + t, t + 1:] = 1.0
    for lvl in range(1, nlev + 1):
        bs, half = 2 ** lvl, 2 ** (lvl - 1)
        for t in range(length):
            mid = t - t % bs + half
            row = (1 + lvl) * length + t
            if t >= mid:
                w[row, mid:t + 1] = 1.0
                masks[lvl, t, mid - half:mid] = 1.0
            else:
                w[row, t + 1:mid] = 1.0
    return jnp.asarray(w, BF16), jnp.asarray(masks, F32)


def _chunk_step(q, k, g, v, st, w, masks):
    length = q.shape[0]
    nlev = masks.shape[0] - 1
    g1 = g.astype(BF16)
    rem = g - g1.astype(F32)
    g2 = rem.astype(BF16)
    g3 = (rem - g2.astype(F32)).astype(BF16)
    ex = jnp.exp(jnp.dot(w, g1, preferred_element_type=F32)
                 + jnp.dot(w, g2, preferred_element_type=F32)
                 + jnp.dot(w, g3, preferred_element_type=F32))
    e_cum = ex[0:length]
    e_rem = ex[length:2 * length]
    e_last = e_cum[length - 1:length]
    vb = v.astype(BF16)
    o = lax.dot_general((q * e_cum).astype(BF16), st.astype(BF16), NT_DIMS,
                        preferred_element_type=F32)
    qb, kb = q.astype(BF16), k.astype(BF16)
    a = masks[0] * lax.dot_general(qb, kb, NT_DIMS, preferred_element_type=F32)
    row = lax.broadcasted_iota(jnp.int32, q.shape, 0)
    for lvl in range(1, nlev + 1):
        second_half = (row & (2 ** lvl - 1)) >= 2 ** (lvl - 1)
        x = (jnp.where(second_half, q, k) * ex[(1 + lvl) * length:(2 + lvl) * length]).astype(BF16)
        a = a + masks[lvl] * lax.dot_general(x, x, NT_DIMS, preferred_element_type=F32)
    o = o + jnp.dot(a.astype(BF16), vb, preferred_element_type=F32)
    kd = (k * e_rem).astype(BF16)
    st = st * e_last + lax.dot_general(vb, kd, TN_DIMS, preferred_element_type=F32)
    return o, st


def _gate_norm(o, r, gn):
    ms = jnp.mean(o * o, axis=-1, keepdims=True)
    return o * lax.rsqrt(ms + RMS_EPS) * gn * (r * _sigmoid(r))


def _recur_prompt_kernel(gn_ref, ws_ref, ms_ref, wm_ref, mm_ref, q_ref, k_ref, g_ref, v_ref, r_ref,
                         o_ref, s_ref, st_ref, *, n_meta, chunk):
    t_all = q_ref.shape[0]
    gn = gn_ref[...]
    st_ref[...] = jnp.zeros_like(st_ref)

    def run(r0, length, w, masks):
        rows = pl.ds(r0, length)
        o, st = _chunk_step(q_ref[rows], k_ref[rows], g_ref[rows], v_ref[rows], st_ref[...],
                            w, masks)
        st_ref[...] = st
        o_ref[rows] = _gate_norm(o, r_ref[rows], gn).astype(o_ref.dtype)

    run(0, n_meta, ws_ref[...], ms_ref[...])

    def body(c, _):
        run(pl.multiple_of(n_meta + c * chunk, 16), chunk, wm_ref[...], mm_ref[...])
        return 0

    lax.fori_loop(0, (t_all - n_meta) // chunk, body, 0)
    s_ref[...] = st_ref[...].T


def _recur_prompt(q, k, g, v, r, gn, n_meta, chunk):
    b, t, kd = q.shape
    vd = v.shape[2]
    nh = kd // DK
    dv = vd // nh
    ws, ms = _chunk_constants(n_meta)
    wm, mm = _chunk_constants(chunk)
    kblk = pl.BlockSpec((None, t, DK), lambda i, h: (i, 0, h))
    vblk = pl.BlockSpec((None, t, dv), lambda i, h: (i, 0, h))

    def const(x):
        return pl.BlockSpec(x.shape, lambda i, h: (0,) * x.ndim)

    return pl.pallas_call(
        functools.partial(_recur_prompt_kernel, n_meta=n_meta, chunk=chunk),
        out_shape=(jax.ShapeDtypeStruct((b, t, vd), BF16),
                   jax.ShapeDtypeStruct((b, nh, DK, dv), F32)),
        grid=(b, nh),
        in_specs=[pl.BlockSpec((1, dv), lambda i, h: (0, 0)), const(ws), const(ms), const(wm),
                  const(mm), kblk, kblk, kblk, vblk, vblk],
        out_specs=(vblk, pl.BlockSpec((None, None, DK, dv), lambda i, h: (i, h, 0, 0))),
        scratch_shapes=[pltpu.VMEM((dv, DK), F32)],
        compiler_params=_cparams("parallel", "parallel"),
        name="recur_prompt",
    )(gn, ws, ms, wm, mm, q, k, g, v, r)


def _recur_sample_kernel(gn_ref, w_ref, m_ref, q_ref, k_ref, g_ref, v_ref, r_ref, s0_ref,
                         o_ref, s_ref, *, n_new):
    nh = s0_ref.shape[0]
    dv = s0_ref.shape[2]
    gn = gn_ref[...]
    w, masks = w_ref[...], m_ref[...]
    pad_rows = SMALL_CHUNK - SAMPLE_ROWS
    valid = lax.broadcasted_iota(jnp.int32, (SAMPLE_ROWS, DK), 0) < n_new

    def padded(x):
        return jnp.concatenate([x, jnp.zeros((pad_rows, x.shape[1]), F32)], axis=0)

    for h in range(nh):
        ksl = slice(h * DK, (h + 1) * DK)
        vsl = slice(h * dv, (h + 1) * dv)
        kk = jnp.where(valid, k_ref[:, ksl], 0.0)
        gg = jnp.where(valid, g_ref[:, ksl], 0.0)
        o, st = _chunk_step(padded(q_ref[:, ksl]), padded(kk), padded(gg), padded(v_ref[:, vsl]),
                            s0_ref[h].T, w, masks)
        s_ref[h] = st.T
        o_ref[:, vsl] = _gate_norm(o[0:SAMPLE_ROWS], r_ref[:, vsl], gn).astype(o_ref.dtype)


def _recur_sample(q, k, g, v, r, s0, gn, n_new):
    n, kd = q.shape
    vd = v.shape[1]
    nb, nh, _, dv = s0.shape
    w, masks = _chunk_constants(SMALL_CHUNK)
    kblk = pl.BlockSpec((SAMPLE_ROWS, kd), lambda i: (i, 0))
    vblk = pl.BlockSpec((SAMPLE_ROWS, vd), lambda i: (i, 0))
    sblk = pl.BlockSpec((None, nh, DK, dv), lambda i: (i, 0, 0, 0))
    return pl.pallas_call(
        functools.partial(_recur_sample_kernel, n_new=n_new),
        out_shape=(jax.ShapeDtypeStruct((n, vd), BF16), jax.ShapeDtypeStruct(s0.shape, F32)),
        grid=(nb,),
        in_specs=[_full_spec((1, dv)), _full_spec(w.shape), _full_spec(masks.shape),
                  kblk, kblk, kblk, vblk, vblk, sblk],
        out_specs=(vblk, sblk),
        compiler_params=_cparams("parallel"),
        name="recur_sample",
    )(gn, w, masks, q, k, g, v, r, s0)


def _out_proj_kernel(a_ref, w_ref, res_ref, o_ref):
    o_ref[...] = res_ref[...] + jnp.dot(a_ref[...], w_ref[...], preferred_element_type=F32)


def _out_proj(a, w, res, tm):
    n, d = res.shape
    return pl.pallas_call(
        _out_proj_kernel,
        out_shape=jax.ShapeDtypeStruct((n, d), F32),
        grid=(n // tm,),
        in_specs=[_row_spec(tm, a.shape[1]), _full_spec(w.shape), _row_spec(tm, d)],
        out_specs=_row_spec(tm, d),
        compiler_params=_cparams("parallel"),
        name="out_proj",
    )(a, w, res)


def _mlp_kernel(x_ref, nw_ref, wu_ref, wd_ref, o_ref, xn_ref):
    j = pl.program_id(1)

    @pl.when(j == 0)
    def _():
        x = x_ref[...]
        xn_ref[...] = _rms(x, nw_ref[...]).astype(BF16)
        o_ref[...] = x

    h = jnp.dot(xn_ref[...], wu_ref[...], preferred_element_type=F32)
    h = jnp.square(jnp.maximum(h, 0.0)).astype(BF16)
    o_ref[...] += jnp.dot(h, wd_ref[...], preferred_element_type=F32)


def _mlp(x, nw, wu, wd, tm, tf):
    n, d = x.shape
    ff = wu.shape[1]
    return pl.pallas_call(
        _mlp_kernel,
        out_shape=jax.ShapeDtypeStruct((n, d), F32),
        grid=(n // tm, ff // tf),
        in_specs=[pl.BlockSpec((tm, d), lambda i, j: (i, 0)),
                  pl.BlockSpec((1, d), lambda i, j: (0, 0)),
                  pl.BlockSpec((d, tf), lambda i, j: (0, j)),
                  pl.BlockSpec((tf, d), lambda i, j: (j, 0))],
        out_specs=pl.BlockSpec((tm, d), lambda i, j: (i, 0)),
        scratch_shapes=[pltpu.VMEM((tm, d), BF16)],
        compiler_params=_cparams("parallel", "arbitrary"),
        name="mlp",
    )(x, nw, wu, wd)


def _final_norm_prompt_kernel(x_ref, nw_ref, o_ref, *, n_meta):
    tm = o_ref.shape[0]
    r0 = pl.multiple_of(n_meta + pl.program_id(1) * tm, 8)
    o_ref[...] = _rms(x_ref[pl.ds(r0, tm)], nw_ref[...])


def _final_norm_prompt(x, nw, n_meta):
    b, t, d = x.shape
    seq = t - n_meta
    tm = _pick_tile(seq, 512)
    return pl.pallas_call(
        functools.partial(_final_norm_prompt_kernel, n_meta=n_meta),
        out_shape=jax.ShapeDtypeStruct((b, seq, d), F32),
        grid=(b, seq // tm),
        in_specs=[pl.BlockSpec((None, t, d), lambda i, j: (i, 0, 0)),
                  pl.BlockSpec((1, d), lambda i, j: (0, 0))],
        out_specs=pl.BlockSpec((None, tm, d), lambda i, j: (i, j, 0)),
        compiler_params=_cparams("parallel", "arbitrary"),
        name="final_norm_prompt",
    )(x, nw)


def _final_norm_kernel(x_ref, nw_ref, o_ref):
    o_ref[...] = _rms(x_ref[...], nw_ref[...])


def _final_norm(x, nw):
    return pl.pallas_call(
        _final_norm_kernel,
        out_shape=jax.ShapeDtypeStruct(x.shape, F32),
        name="final_norm",
    )(x, nw)


def _rope_tables(pos):
    half = ROT_DIM // 2
    inv_freq = ROPE_THETA ** (-jnp.arange(half, dtype=F32) / half)
    ang = pos.astype(F32)[:, None] * inv_freq[None, :]
    cos, sin = jnp.cos(ang), jnp.sin(ang)
    n = pos.shape[0]
    zeros = jnp.zeros((n, half), F32)
    rest0 = jnp.zeros((n, HEAD_A - ROT_DIM), F32)
    cos_h = jnp.concatenate([cos, cos, jnp.ones((n, HEAD_A - ROT_DIM), F32)], axis=1)
    sin_m = jnp.concatenate([-sin, zeros, rest0], axis=1)
    sin_p = jnp.concatenate([zeros, sin, rest0], axis=1)
    return tuple(jnp.concatenate([t, t], axis=1) for t in (cos_h, sin_m, sin_p))


def kernel(x_prompt, x_sample, cache_k, cache_v, state_gla, state_hgrn, page_table, meta_tokens,
           norm_mix, norm_ffn, norm_final, w_in_a, lam_a, subln_a, w_out_a, w_in_b, w_gate2_b,
           b_gate_b, gnorm_b, w_out_b, w_in_c, lb_c, gnorm_c, w_out_c, w_up, w_down):
    bp, seq, d = x_prompt.shape
    db, n_new = x_sample.shape[:2]
    depth = norm_mix.shape[0]
    n_pages = page_table.shape[1]
    page = cache_k.shape[2]
    past_len = n_pages * page
    t_all = N_META_TOKENS + seq
    n_main = bp * t_all
    n_small = db * SAMPLE_ROWS
    h_a = d // (2 * HEAD_A)
    kd_b = w_gate2_b.shape[2]
    dv_b = d // H_B
    h_c = d // DK
    dv_c = d // h_c

    hp = jnp.concatenate(
        [jnp.broadcast_to(meta_tokens.astype(F32)[None], (bp, N_META_TOKENS, d)), x_prompt],
        axis=1).reshape(n_main, d)
    hs = jnp.pad(x_sample, ((0, 0), (0, SAMPLE_ROWS - n_new), (0, 0))).reshape(n_small, d)

    tm_main = _pick_tile(t_all, 704)
    tm_mlp = _pick_tile(n_main, 1400)
    tf = 512
    tab_p = _rope_tables(jnp.arange(t_all))
    tab_s = _rope_tables(jnp.tile(past_len + jnp.arange(SAMPLE_ROWS), db))
    ck = cache_k.reshape(cache_k.shape[0], cache_k.shape[1], page, d)
    cv = cache_v.reshape(cache_v.shape[0], cache_v.shape[1], page, d)
    pp = 4 if n_pages % 4 == 0 else 1

    kp_rows, vp_rows, ks_rows, vs_rows = [], [], [], []
    gla_p, gla_s, hgrn_p, hgrn_s = [], [], [], []
    for i in range(depth):
        kind, j = i % N_MIXERS, i // N_MIXERS
        nw = norm_mix[i].reshape(1, d)
        if kind == 0:
            lam_init = 0.8 - 0.6 * math.exp(-0.3 * i)
            w = w_in_a[j].astype(BF16)
            lam = lam_a[j].astype(F32)
            subln = subln_a[j].reshape(1, 2 * HEAD_A)
            qp, kp, vp = _proj_a(hp, nw, w, tab_p, tm_main, t_all // tm_main)
            qs, ks, vs = _proj_a(hs, nw, w, tab_s, n_small, 1)
            shp = (bp, t_all, d)
            ap = _attn_prompt(qp.reshape(shp), kp.reshape(shp), vp.reshape(shp), lam, subln,
                              lam_init, N_META_TOKENS).reshape(n_main, d)
            as_ = _attn_sample(qs, ks, vs, ck, cv, page_table, j, lam, subln, lam_init, n_new, pp)
            w_out = w_out_a[j]
            kp_rows.append(kp.reshape(bp, t_all, 2 * h_a, HEAD_A))
            vp_rows.append(vp.reshape(bp, t_all, h_a, 2 * HEAD_A))
            ks_rows.append(ks.reshape(db, SAMPLE_ROWS, 2 * h_a, HEAD_A)[:, :n_new])
            vs_rows.append(vs.reshape(db, SAMPLE_ROWS, h_a, 2 * HEAD_A)[:, :n_new])
        elif kind == 1:
            w = w_in_b[j]
            n_main_cols = 2 * kd_b + 2 * d
            w_main = w[:, :n_main_cols].astype(BF16)
            w_gl = jnp.pad(w[:, n_main_cols:], ((0, 0), (0, LANES - GATE_RANK))).astype(BF16)
            w_g2 = jnp.pad(w_gate2_b[j], ((0, LANES - GATE_RANK), (0, 0))).astype(BF16)
            bg = b_gate_b[j].reshape(1, kd_b)
            gn = gnorm_b[j].reshape(1, dv_b)
            qp, kp, vp, rp, gp = _proj_b(hp, nw, w_main, w_gl, w_g2, bg, tm_main)
            qs, ks, vs, rs, gs = _proj_b(hs, nw, w_main, w_gl, w_g2, bg, n_small)
            r3 = lambda a: a.reshape(bp, t_all, a.shape[1])
            ap, sp = _recur_prompt(r3(qp), r3(kp), r3(gp), r3(vp), r3(rp), gn, N_META_TOKENS,
                                   MAIN_CHUNK)
            ap = ap.reshape(n_main, d)
            as_, ss = _recur_sample(qs, ks, gs, vs, rs, state_gla[j].astype(F32), gn, n_new)
            gla_p.append(sp)
            gla_s.append(ss)
            w_out = w_out_b[j]
        else:
            w = w_in_c[j].astype(BF16)
            gn = gnorm_c[j].reshape(1, dv_c)
            lbf = lb_c.astype(F32)
            qp, kp, gp, vp, rp = _proj_c(hp, nw, w, lbf, i, tm_main)
            qs, ks, gs, vs, rs = _proj_c(hs, nw, w, lbf, i, n_small)
            r3 = lambda a: a.reshape(bp, t_all, a.shape[1])
            ap, sp = _recur_prompt(r3(qp), r3(kp), r3(gp), r3(vp), r3(rp), gn, N_META_TOKENS,
                                   MAIN_CHUNK)
            ap = ap.reshape(n_main, d)
            as_, ss = _recur_sample(qs, ks, gs, vs, rs, state_hgrn[j].astype(F32), gn, n_new)
            hgrn_p.append(sp)
            hgrn_s.append(ss)
            w_out = w_out_c[j]
        w_out = w_out.astype(BF16)
        hp = _out_proj(ap, w_out, hp, tm_main)
        hs = _out_proj(as_, w_out, hs, n_small)
        nf = norm_ffn[i].reshape(1, d)
        wu, wd = w_up[i].astype(BF16), w_down[i].astype(BF16)
        hp = _mlp(hp, nf, wu, wd, tm_mlp, tf)
        hs = _mlp(hs, nf, wu, wd, n_small, tf)

    nfin = norm_final.reshape(1, d)
    y_prompt = _final_norm_prompt(hp.reshape(bp, t_all, d), nfin, N_META_TOKENS)
    y_sample = _final_norm(hs, nfin).reshape(db, SAMPLE_ROWS, d)[:, :n_new]
    return (y_prompt, y_sample,
            jnp.stack(kp_rows), jnp.stack(vp_rows), jnp.stack(ks_rows), jnp.stack(vs_rows),
            jnp.stack(gla_p), jnp.stack(gla_s), jnp.stack(hgrn_p), jnp.stack(hgrn_s))
```

```python
import functools
import math

import numpy as np
import jax
import jax.numpy as jnp
from jax import lax
from jax.experimental import pallas as pl
from jax.experimental.pallas import tpu as pltpu

F32 = jnp.float32
BF16 = jnp.bfloat16

RMS_EPS = 1e-6
N_META_TOKENS = 16
N_MIXERS = 3
HEAD_A = 64
ROT_DIM = HEAD_A // 4
ROPE_THETA = 500000.0
H_B = 4
GATE_RANK = 16
GATE_TEMP = 16.0
DK = 128
SAMPLE_ROWS = 8
SMALL_CHUNK = 16
MAIN_CHUNK = 64
LANES = 128
VMEM_LIMIT = 56 * 1024 * 1024
LOG2E = 1.4426950408889634

NT_DIMS = (((1,), (1,)), ((), ()))
TN_DIMS = (((0,), (0,)), ((), ()))


def _cparams(*sem):
    return pltpu.CompilerParams(dimension_semantics=sem, vmem_limit_bytes=VMEM_LIMIT)


def _pick_tile(n, target, mult=16):
    best = None
    for t in range(mult, min(n, target) + 1, mult):
        if n % t == 0:
            best = t
    return n if best is None else best


def _rms(x, w):
    ms = jnp.mean(x * x, axis=-1, keepdims=True)
    return x * lax.rsqrt(ms + RMS_EPS) * w


def _sigmoid(x):
    return 1.0 / (1.0 + jnp.exp(-x))


def _proj_a_kernel(x_ref, nw_ref, w_ref, c_ref, sm_ref, sp_ref, q_ref, k_ref, v_ref, *, cw):
    xn = _rms(x_ref[...], nw_ref[...]).astype(BF16)
    d = q_ref.shape[1]
    rep = cw // LANES
    cos = jnp.tile(c_ref[...], (1, rep))
    sin_m = jnp.tile(sm_ref[...], (1, rep))
    sin_p = jnp.tile(sp_ref[...], (1, rep))
    for dst_i, dst in enumerate((q_ref, k_ref, v_ref)):
        for c in range(d // cw):
            col = dst_i * d + c * cw
            y = jnp.dot(xn, w_ref[:, col:col + cw], preferred_element_type=F32)
            if dst_i < 2:
                y = (y * cos + pltpu.roll(y, ROT_DIM // 2, 1) * sin_p
                     + pltpu.roll(y, cw - ROT_DIM // 2, 1) * sin_m)
            dst[:, c * cw:(c + 1) * cw] = y


def _proj_b_kernel(x_ref, nw_ref, w_ref, wgl_ref, wg2_ref, bg_ref,
                   q_ref, k_ref, v_ref, r_ref, g_ref, *, cw):
    xn = _rms(x_ref[...], nw_ref[...]).astype(BF16)
    col = 0
    for dst, scale in ((q_ref, DK ** -0.5), (k_ref, None), (v_ref, None), (r_ref, None)):
        for c in range(dst.shape[1] // cw):
            y = jnp.dot(xn, w_ref[:, col:col + cw], preferred_element_type=F32)
            if scale is not None:
                y = y * scale
            dst[:, c * cw:(c + 1) * cw] = y.astype(dst.dtype)
            col += cw
    gl = jnp.dot(xn, wgl_ref[...], preferred_element_type=F32).astype(BF16)
    z = jnp.dot(gl, wg2_ref[...], preferred_element_type=F32) + bg_ref[...]
    g_ref[...] = (jnp.minimum(z, 0.0) - jnp.log(1.0 + jnp.exp(-jnp.abs(z)))) * (1.0 / GATE_TEMP)


def _proj_c_kernel(x_ref, nw_ref, w_ref, lb_ref, q_ref, k_ref, g_ref, v_ref, r_ref, *, cw, layer):
    xn = _rms(x_ref[...], nw_ref[...]).astype(BF16)
    d = q_ref.shape[1]
    lbs = lb_ref[...]
    mx = jnp.max(lbs, axis=0, keepdims=True)
    e = jnp.exp(lbs - mx)
    lb = jnp.sum(e[1:layer + 1], axis=0, keepdims=True) / jnp.sum(e, axis=0, keepdims=True)
    for c in range(d // cw):
        sl = slice(c * cw, (c + 1) * cw)
        y = jnp.dot(xn, w_ref[:, c * cw:(c + 1) * cw], preferred_element_type=F32)
        q_ref[:, sl] = y * _sigmoid(y) * (DK ** -0.5)
        f = jnp.dot(xn, w_ref[:, d + c * cw:d + (c + 1) * cw], preferred_element_type=F32)
        lbc = lb[:, sl]
        forget = lbc + (1.0 - lbc) * _sigmoid(f)
        k_ref[:, sl] = 1.0 - forget
        g_ref[:, sl] = jnp.log(forget)
        v_ref[:, sl] = jnp.dot(xn, w_ref[:, 2 * d + c * cw:2 * d + (c + 1) * cw],
                               preferred_element_type=F32).astype(v_ref.dtype)
        r_ref[:, sl] = jnp.dot(xn, w_ref[:, 3 * d + c * cw:3 * d + (c + 1) * cw],
                               preferred_element_type=F32)


def _row_spec(tm, width):
    return pl.BlockSpec((tm, width), lambda i: (i, 0))


def _full_spec(shape):
    return pl.BlockSpec(shape, lambda i: (0,) * len(shape))


def _proj_a(x, nw, w, tables, tm, table_blocks):
    n, d = x.shape
    cw = min(512, d)
    tab_spec = pl.BlockSpec((tm, LANES), lambda i: (i % table_blocks, 0))
    out = jax.ShapeDtypeStruct((n, d), F32)
    return pl.pallas_call(
        functools.partial(_proj_a_kernel, cw=cw),
        out_shape=(out, out, out),
        grid=(n // tm,),
        in_specs=[_row_spec(tm, d), _full_spec((1, d)), _full_spec(w.shape),
                  tab_spec, tab_spec, tab_spec],
        out_specs=(_row_spec(tm, d),) * 3,
        compiler_params=_cparams("parallel"),
        name="proj_a",
    )(x, nw, w, *tables)


def _proj_b(x, nw, w, wgl, wg2, bg, tm):
    n, d = x.shape
    kd, vd = wg2.shape[1], d
    cw = min(512, kd)
    return pl.pallas_call(
        functools.partial(_proj_b_kernel, cw=cw),
        out_shape=(jax.ShapeDtypeStruct((n, kd), F32), jax.ShapeDtypeStruct((n, kd), F32),
                   jax.ShapeDtypeStruct((n, vd), BF16), jax.ShapeDtypeStruct((n, vd), F32),
                   jax.ShapeDtypeStruct((n, kd), F32)),
        grid=(n // tm,),
        in_specs=[_row_spec(tm, d), _full_spec((1, d)), _full_spec(w.shape),
                  _full_spec(wgl.shape), _full_spec(wg2.shape), _full_spec((1, kd))],
        out_specs=(_row_spec(tm, kd), _row_spec(tm, kd), _row_spec(tm, vd), _row_spec(tm, vd),
                   _row_spec(tm, kd)),
        compiler_params=_cparams("parallel"),
        name="proj_b",
    )(x, nw, w, wgl, wg2, bg)


def _proj_c(x, nw, w, lb_c, layer, tm):
    n, d = x.shape
    cw = min(512, d)
    out = jax.ShapeDtypeStruct((n, d), F32)
    return pl.pallas_call(
        functools.partial(_proj_c_kernel, cw=cw, layer=layer),
        out_shape=(out, out, out, jax.ShapeDtypeStruct((n, d), BF16), out),
        grid=(n // tm,),
        in_specs=[_row_spec(tm, d), _full_spec((1, d)), _full_spec(w.shape),
                  _full_spec(lb_c.shape)],
        out_specs=(_row_spec(tm, d),) * 5,
        compiler_params=_cparams("parallel"),
        name="proj_c",
    )(x, nw, w, lb_c)


def _lambda_full(lam_ref, lam_init):
    l = lam_ref[...]
    a = jnp.sum(l[0:1] * l[1:2], axis=-1, keepdims=True)
    b = jnp.sum(l[2:3] * l[3:4], axis=-1, keepdims=True)
    return jnp.exp(a) - jnp.exp(b) + lam_init


def _head_finish(o, subln, lam_init):
    ms = jnp.mean(o * o, axis=-1, keepdims=True)
    return o * lax.rsqrt(ms + RMS_EPS) * subln * (1.0 - lam_init)


def _attn_prompt_kernel(lam_ref, subln_ref, q_ref, k_ref, v_ref, o_ref,
                        kb_ref, vt_ref, km_ref, vtm_ref, m_ref, l_ref, acc_ref, bias_ref,
                        *, tq, n_meta, lam_init, hb):
    t_all = q_ref.shape[0]
    nq = (t_all - n_meta) // tq
    hw = 2 * HEAD_A
    lam = _lambda_full(lam_ref, lam_init)
    subln = subln_ref[...]
    first_map = lax.broadcasted_iota(jnp.int32, (1, hw), 1) < HEAD_A

    def stack_q(qt, scale):
        qt = qt * scale
        return jnp.concatenate([jnp.where(first_map, qt, 0.0), jnp.where(first_map, 0.0, qt)],
                               axis=0).astype(BF16)

    r = lax.broadcasted_iota(jnp.int32, (2 * n_meta, n_meta), 0)
    c = lax.broadcasted_iota(jnp.int32, (2 * n_meta, n_meta), 1)
    meta_causal = c <= jnp.where(r >= n_meta, r - n_meta, r)
    pad = jnp.zeros((hw - n_meta, hw), F32)
    for h in range(hb):
        lanes = slice(h * hw, (h + 1) * hw)
        kmeta = k_ref[0:n_meta, lanes]
        vmeta = v_ref[0:n_meta, lanes]
        qs = stack_q(q_ref[0:n_meta, lanes], HEAD_A ** -0.5)
        s = lax.dot_general(qs, kmeta.astype(BF16), NT_DIMS, preferred_element_type=F32)
        s = jnp.where(meta_causal, s, -jnp.inf)
        p = jnp.exp(s - jnp.max(s, axis=-1, keepdims=True))
        acc = jnp.dot(p.astype(BF16), vmeta.astype(BF16), preferred_element_type=F32)
        acc = acc / jnp.sum(p, axis=-1, keepdims=True)
        o = acc[:n_meta] - lam * acc[n_meta:]
        o_ref[0:n_meta, lanes] = _head_finish(o, subln, lam_init).astype(o_ref.dtype)
        km_ref[h] = jnp.concatenate([kmeta, pad], axis=0).astype(BF16)
        vtm_ref[h] = jnp.concatenate([vmeta, pad], axis=0).T.astype(BF16)
        for cidx in range(nq):
            rows = slice(n_meta + cidx * tq, n_meta + (cidx + 1) * tq)
            kb_ref[h, cidx] = k_ref[rows, lanes].astype(BF16)
            vt_ref[h, cidx] = v_ref[rows, lanes].T.astype(BF16)

    key_i = lax.broadcasted_iota(jnp.int32, (hw + tq, 2 * tq), 0)
    qry_i = lax.broadcasted_iota(jnp.int32, (hw + tq, 2 * tq), 1)
    key_limit = jnp.where(key_i < hw, n_meta - 1, hw + jnp.where(qry_i >= tq, qry_i - tq, qry_i))
    bias_ref[...] = jnp.where(key_i <= key_limit, 0.0, -jnp.inf)

    def online(h, s, pv_fn):
        m_old = m_ref[h]
        m_new = jnp.maximum(m_old, jnp.max(s, axis=0, keepdims=True))
        alpha = jnp.exp2(m_old - m_new)
        p = jnp.exp2(s - m_new)
        l_ref[h] = alpha * l_ref[h] + jnp.sum(p, axis=0, keepdims=True)
        acc_ref[h] = alpha * acc_ref[h] + pv_fn(p.astype(BF16))
        m_ref[h] = m_new

    def q_body(qi, _):
        q0 = pl.multiple_of(n_meta + qi * tq, 16)
        qss = [stack_q(q_ref[pl.ds(q0, tq), h * hw:(h + 1) * hw], (HEAD_A ** -0.5) * LOG2E)
               for h in range(hb)]
        for h in range(hb):
            s = jnp.concatenate(
                [lax.dot_general(km_ref[h], qss[h], NT_DIMS, preferred_element_type=F32),
                 lax.dot_general(kb_ref[h, qi], qss[h], NT_DIMS, preferred_element_type=F32)],
                axis=0)
            s = s + bias_ref[...]
            m = jnp.max(s, axis=0, keepdims=True)
            p = jnp.exp2(s - m)
            m_ref[h] = m
            l_ref[h] = jnp.sum(p, axis=0, keepdims=True)
            p = p.astype(BF16)
            acc_ref[h] = (jnp.dot(vtm_ref[h], p[:hw], preferred_element_type=F32)
                          + jnp.dot(vt_ref[h, qi], p[hw:], preferred_element_type=F32))

        def pair_body(j, _):
            for h in range(hb):
                kpair = kb_ref[h, pl.ds(2 * j, 2)].reshape(2 * tq, hw)
                s = lax.dot_general(kpair, qss[h], NT_DIMS, preferred_element_type=F32)
                online(h, s, lambda p, h=h: (
                    jnp.dot(vt_ref[h, 2 * j], p[:tq], preferred_element_type=F32)
                    + jnp.dot(vt_ref[h, 2 * j + 1], p[tq:], preferred_element_type=F32)))
            return 0

        lax.fori_loop(0, qi // 2, pair_body, 0)

        @pl.when(qi % 2 == 1)
        def _():
            for h in range(hb):
                s = lax.dot_general(kb_ref[h, qi - 1], qss[h], NT_DIMS,
                                    preferred_element_type=F32)
                online(h, s, lambda p, h=h: jnp.dot(vt_ref[h, qi - 1], p,
                                                    preferred_element_type=F32))

        for h in range(hb):
            ot = acc_ref[h] * (1.0 / l_ref[h])
            od = ot[:, :tq] - lam * ot[:, tq:]
            ms = jnp.mean(od * od, axis=0, keepdims=True)
            od = od * lax.rsqrt(ms + RMS_EPS)
            o_ref[pl.ds(q0, tq), h * hw:(h + 1) * hw] = (
                od.T * (subln * (1.0 - lam_init))).astype(o_ref.dtype)
        return 0

    lax.fori_loop(0, nq, q_body, 0)


def _attn_prompt(q, k, v, lam, subln, lam_init, n_meta, hb):
    b, t, d = q.shape
    hw = 2 * HEAD_A
    tq = _pick_tile(t - n_meta, 256, LANES)
    nq = (t - n_meta) // tq
    blk = pl.BlockSpec((None, t, hb * hw), lambda i, h: (i, 0, h))
    return pl.pallas_call(
        functools.partial(_attn_prompt_kernel, tq=tq, n_meta=n_meta, lam_init=lam_init, hb=hb),
        out_shape=jax.ShapeDtypeStruct((b, t, d), BF16),
        grid=(b, d // (hb * hw)),
        in_specs=[pl.BlockSpec(lam.shape, lambda i, h: (0, 0)),
                  pl.BlockSpec((1, hw), lambda i, h: (0, 0)), blk, blk, blk],
        out_specs=blk,
        scratch_shapes=[pltpu.VMEM((hb, nq, tq, hw), BF16), pltpu.VMEM((hb, nq, hw, tq), BF16),
                        pltpu.VMEM((hb, hw, hw), BF16), pltpu.VMEM((hb, hw, hw), BF16),
                        pltpu.VMEM((hb, 1, 2 * tq), F32), pltpu.VMEM((hb, 1, 2 * tq), F32),
                        pltpu.VMEM((hb, hw, 2 * tq), F32), pltpu.VMEM((hw + tq, 2 * tq), F32)],
        compiler_params=_cparams("parallel", "parallel"),
        name="attn_prompt",
    )(lam, subln, q, k, v)


def _attn_sample_kernel(pt_ref, lam_ref, subln_ref, q_ref, kn_ref, vn_ref, *rest,
                        pp, n_new, lam_init):
    del pt_ref
    k_refs, v_refs = rest[:pp], rest[pp:2 * pp]
    o_ref, m_ref, l_ref, acc_ref, s_ref, ve_ref = rest[2 * pp:]
    step = pl.program_id(1)
    rows = q_ref.shape[0]
    n_heads = rows // n_new
    krows, vrows = k_refs[0].shape[0], v_refs[0].shape[0]
    qb = (q_ref[...] * ((HEAD_A ** -0.5) * LOG2E)).astype(BF16)
    row_i = lax.broadcasted_iota(jnp.int32, (rows, LANES), 0)
    lane_i = lax.broadcasted_iota(jnp.int32, (rows, LANES), 1)
    own_head = (lane_i % n_heads) == (row_i // n_new)
    bias = jnp.tile(jnp.where(own_head, 0.0, -jnp.inf), (1, krows // LANES))

    @pl.when(step == 0)
    def _():
        s = lax.dot_general(qb, kn_ref[...].astype(BF16), NT_DIMS, preferred_element_type=F32)
        ok = own_head & ((lane_i // n_heads) <= (row_i % n_new))
        s = jnp.where(ok, s, -jnp.inf)
        m = jnp.max(s, axis=-1, keepdims=True)
        p = jnp.exp2(s - m)
        m_ref[...] = m
        l_ref[...] = jnp.sum(p, axis=-1, keepdims=True)
        acc_ref[...] = jnp.dot(p.astype(BF16), vn_ref[...].astype(BF16),
                               preferred_element_type=F32)

    m_old = m_ref[...]
    m_new = m_old
    for i in range(pp):
        s = lax.dot_general(qb, k_refs[i][...].astype(BF16), NT_DIMS,
                            preferred_element_type=F32) + bias
        s_ref[i] = s
        m_new = jnp.maximum(m_new, jnp.max(s, axis=-1, keepdims=True))
        v = v_refs[i][...]
        ve_ref[i, pl.ds(0, vrows, stride=2), :] = v
        ve_ref[i, pl.ds(1, vrows, stride=2), :] = v
    alpha = jnp.exp2(m_old - m_new)
    l_new = alpha * l_ref[...]
    acc = alpha * acc_ref[...]
    for i in range(pp):
        p = jnp.exp2(s_ref[i] - m_new)
        l_new = l_new + jnp.sum(p, axis=-1, keepdims=True)
        acc = acc + jnp.dot(p.astype(BF16), ve_ref[i].astype(BF16), preferred_element_type=F32)
    l_ref[...] = l_new
    acc_ref[...] = acc
    m_ref[...] = m_new

    @pl.when(step == pl.num_programs(1) - 1)
    def _():
        lam = _lambda_full(lam_ref, lam_init)
        subln = subln_ref[...]
        hw = 2 * HEAD_A
        full = acc_ref[...] / l_ref[...]
        diff = full - lam * pltpu.roll(full, rows - n_new, 0)
        for h in range(n_heads // 2):
            tile = diff[2 * n_new * h:2 * n_new * (h + 1)]
            o_ref[:, h * hw:(h + 1) * hw] = _head_finish(tile, subln, lam_init).astype(o_ref.dtype)


def _attn_sample(q, k_new, v_new, cache_k, cache_v, page_table, layer_j, lam, subln, lam_init,
                 n_new, pp):
    nb, rows, _ = q.shape
    n_pages = page_table.shape[1]
    krows, vrows = cache_k.shape[2], cache_v.shape[2]
    hw = 2 * HEAD_A
    d = (rows // n_new) * HEAD_A
    assert 2 * n_new == SAMPLE_ROWS and LANES % (rows // n_new) == 0 and krows == 2 * vrows

    def per_sample(shape):
        return pl.BlockSpec((None,) + shape, lambda b, s, pt: (b, 0, 0))

    def kpage(i):
        return pl.BlockSpec((None, None, krows, HEAD_A),
                            lambda b, s, pt: (layer_j, pt[b, s * pp + i], 0, 0))

    def vpage(i):
        return pl.BlockSpec((None, None, vrows, hw),
                            lambda b, s, pt: (layer_j, pt[b, s * pp + i], 0, 0))

    grid_spec = pltpu.PrefetchScalarGridSpec(
        num_scalar_prefetch=1,
        grid=(nb, n_pages // pp),
        in_specs=[pl.BlockSpec(lam.shape, lambda b, s, pt: (0, 0)),
                  pl.BlockSpec((1, hw), lambda b, s, pt: (0, 0)),
                  per_sample(q.shape[1:]), per_sample(k_new.shape[1:]),
                  per_sample(v_new.shape[1:])]
                 + [kpage(i) for i in range(pp)] + [vpage(i) for i in range(pp)],
        out_specs=pl.BlockSpec((SAMPLE_ROWS, d), lambda b, s, pt: (b, 0)),
        scratch_shapes=[pltpu.VMEM((rows, 1), F32), pltpu.VMEM((rows, 1), F32),
                        pltpu.VMEM((rows, hw), F32), pltpu.VMEM((pp, rows, krows), F32),
                        pltpu.VMEM((pp, krows, hw), F32)],
    )
    return pl.pallas_call(
        functools.partial(_attn_sample_kernel, pp=pp, n_new=n_new, lam_init=lam_init),
        out_shape=jax.ShapeDtypeStruct((nb * SAMPLE_ROWS, d), BF16),
        grid_spec=grid_spec,
        compiler_params=_cparams("parallel", "arbitrary"),
        name="attn_sample",
    )(page_table, lam, subln, q, k_new, v_new, *([cache_k] * pp), *([cache_v] * pp))


def _chunk_constants(length):
    nlev = int(math.log2(length))
    assert 2 ** nlev == length
    w = np.zeros(((2 + nlev) * length, length), np.float32)
    masks = np.zeros((nlev + 1, length, length), np.float32)
    masks[0] = np.eye(length)
    for t in range(length):
        w[t, :t + 1] = 1.0
        w[length + t, t + 1:] = 1.0
    for lvl in range(1, nlev + 1):
        bs, half = 2 ** lvl, 2 ** (lvl - 1)
        for t in range(length):
            mid = t - t % bs + half
            row = (1 + lvl) * length + t
            if t >= mid:
                w[row, mid:t + 1] = 1.0
                masks[lvl, t, mid - half:mid] = 1.0
            else:
                w[row, t + 1:mid] = 1.0
    return jnp.asarray(np.tile(w, (1, 3)), BF16), jnp.asarray(masks, F32)


def _chunk_exponents(g, w3):
    g = g * LOG2E
    g1 = g.astype(BF16)
    rem = g - g1.astype(F32)
    g2 = rem.astype(BF16)
    g3 = (rem - g2.astype(F32)).astype(BF16)
    return jnp.exp2(jnp.dot(w3, jnp.concatenate([g1, g2, g3], axis=0),
                            preferred_element_type=F32))


def _chunk_head(q, k, vb, st, ex, masks):
    length = q.shape[0]
    nlev = masks.shape[0] - 1
    e_cum = ex[0:length]
    e_rem = ex[length:2 * length]
    e_last = e_cum[length - 1:length]
    o = lax.dot_general((q * e_cum).astype(BF16), st.astype(BF16), NT_DIMS,
                        preferred_element_type=F32)
    qb, kb = q.astype(BF16), k.astype(BF16)
    a = masks[0] * lax.dot_general(qb, kb, NT_DIMS, preferred_element_type=F32)
    row = lax.broadcasted_iota(jnp.int32, q.shape, 0)
    for lvl in range(1, nlev + 1):
        second_half = (row & (2 ** lvl - 1)) >= 2 ** (lvl - 1)
        x = (jnp.where(second_half, q, k) * ex[(1 + lvl) * length:(2 + lvl) * length]).astype(BF16)
        a = a + masks[lvl] * lax.dot_general(x, x, NT_DIMS, preferred_element_type=F32)
    o = o + jnp.dot(a.astype(BF16), vb, preferred_element_type=F32)
    kd = (k * e_rem).astype(BF16)
    st = st * e_last + lax.dot_general(vb, kd, TN_DIMS, preferred_element_type=F32)
    return o, st


def _gate_norm(o, r, gn):
    ms = jnp.mean(o * o, axis=-1, keepdims=True)
    return o * lax.rsqrt(ms + RMS_EPS) * gn * (r * _sigmoid(r))


def _recur_prompt_kernel(gn_ref, ws_ref, ms_ref, wm_ref, mm_ref, q_ref, k_ref, g_ref, v_ref, r_ref,
                         o_ref, s_ref, st_ref, *, n_meta, chunk, hb):
    t_all = q_ref.shape[0]
    dv = v_ref.shape[1] // hb
    gn = gn_ref[...]
    st_ref[...] = jnp.zeros_like(st_ref)

    def run(r0, length, w3, masks):
        rows = pl.ds(r0, length)
        ex = _chunk_exponents(g_ref[rows], w3)
        for h in range(hb):
            ksl = slice(h * DK, (h + 1) * DK)
            vsl = slice(h * dv, (h + 1) * dv)
            o, st = _chunk_head(q_ref[rows, ksl], k_ref[rows, ksl], v_ref[rows, vsl], st_ref[h],
                                ex[:, ksl], masks)
            st_ref[h] = st
            o_ref[rows, vsl] = _gate_norm(o, r_ref[rows, vsl], gn).astype(o_ref.dtype)

    run(0, n_meta, ws_ref[...], ms_ref[...])

    def body(c, _):
        run(pl.multiple_of(n_meta + c * chunk, 16), chunk, wm_ref[...], mm_ref[...])
        return 0

    lax.fori_loop(0, (t_all - n_meta) // chunk, body, 0, unroll=2)
    for h in range(hb):
        s_ref[h] = st_ref[h].T


def _recur_prompt(q, k, g, v, r, gn, n_meta, chunk, hb):
    b, t, kd = q.shape
    vd = v.shape[2]
    nh = kd // DK
    dv = vd // nh
    ws, ms = _chunk_constants(n_meta)
    wm, mm = _chunk_constants(chunk)
    kblk = pl.BlockSpec((None, t, hb * DK), lambda i, h: (i, 0, h))
    vblk = pl.BlockSpec((None, t, hb * dv), lambda i, h: (i, 0, h))

    def const(x):
        return pl.BlockSpec(x.shape, lambda i, h: (0,) * x.ndim)

    return pl.pallas_call(
        functools.partial(_recur_prompt_kernel, n_meta=n_meta, chunk=chunk, hb=hb),
        out_shape=(jax.ShapeDtypeStruct((b, t, vd), BF16),
                   jax.ShapeDtypeStruct((b, nh, DK, dv), F32)),
        grid=(b, nh // hb),
        in_specs=[pl.BlockSpec((1, dv), lambda i, h: (0, 0)), const(ws), const(ms), const(wm),
                  const(mm), kblk, kblk, kblk, vblk, vblk],
        out_specs=(vblk, pl.BlockSpec((None, hb, DK, dv), lambda i, h: (i, h, 0, 0))),
        scratch_shapes=[pltpu.VMEM((hb, dv, DK), F32)],
        compiler_params=_cparams("parallel", "parallel"),
        name="recur_prompt",
    )(gn, ws, ms, wm, mm, q, k, g, v, r)


def _recur_sample_kernel(gn_ref, w_ref, m_ref, q_ref, k_ref, g_ref, v_ref, r_ref, s0_ref,
                         o_ref, s_ref, *, n_new):
    nh = s0_ref.shape[0]
    dv = s0_ref.shape[2]
    gn = gn_ref[...]
    masks = m_ref[...]
    pad_rows = SMALL_CHUNK - SAMPLE_ROWS
    kd = q_ref.shape[1]
    valid = lax.broadcasted_iota(jnp.int32, (SAMPLE_ROWS, kd), 0) < n_new

    def padded(x):
        return jnp.concatenate([x, jnp.zeros((pad_rows, x.shape[1]), x.dtype)], axis=0)

    q = padded(q_ref[...])
    k = padded(jnp.where(valid, k_ref[...], 0.0))
    ex = _chunk_exponents(padded(jnp.where(valid, g_ref[...], 0.0)), w_ref[...])
    vb = padded(v_ref[...].astype(F32)).astype(BF16)
    for h in range(nh):
        ksl = slice(h * DK, (h + 1) * DK)
        vsl = slice(h * dv, (h + 1) * dv)
        o, st = _chunk_head(q[:, ksl], k[:, ksl], vb[:, vsl], s0_ref[h].T, ex[:, ksl], masks)
        s_ref[h] = st.T
        o_ref[:, vsl] = _gate_norm(o[0:SAMPLE_ROWS], r_ref[:, vsl], gn).astype(o_ref.dtype)


def _recur_sample(q, k, g, v, r, s0, gn, n_new):
    n, kd = q.shape
    vd = v.shape[1]
    nb, nh, _, dv = s0.shape
    w, masks = _chunk_constants(SMALL_CHUNK)
    kblk = pl.BlockSpec((SAMPLE_ROWS, kd), lambda i: (i, 0))
    vblk = pl.BlockSpec((SAMPLE_ROWS, vd), lambda i: (i, 0))
    sblk = pl.BlockSpec((None, nh, DK, dv), lambda i: (i, 0, 0, 0))
    return pl.pallas_call(
        functools.partial(_recur_sample_kernel, n_new=n_new),
        out_shape=(jax.ShapeDtypeStruct((n, vd), BF16), jax.ShapeDtypeStruct(s0.shape, F32)),
        grid=(nb,),
        in_specs=[_full_spec((1, dv)), _full_spec(w.shape), _full_spec(masks.shape),
                  kblk, kblk, kblk, vblk, vblk, sblk],
        out_specs=(vblk, sblk),
        compiler_params=_cparams("parallel"),
        name="recur_sample",
    )(gn, w, masks, q, k, g, v, r, s0)


def _out_proj_kernel(a_ref, w_ref, res_ref, o_ref):
    o_ref[...] = res_ref[...] + jnp.dot(a_ref[...], w_ref[...], preferred_element_type=F32)


def _out_proj(a, w, res, tm):
    n, d = res.shape
    return pl.pallas_call(
        _out_proj_kernel,
        out_shape=jax.ShapeDtypeStruct((n, d), F32),
        grid=(n // tm,),
        in_specs=[_row_spec(tm, a.shape[1]), _full_spec(w.shape), _row_spec(tm, d)],
        out_specs=_row_spec(tm, d),
        compiler_params=_cparams("parallel"),
        name="out_proj",
    )(a, w, res)


def _mlp_kernel(x_ref, nw_ref, wu_ref, wd_ref, o_ref, xn_ref):
    j = pl.program_id(1)

    @pl.when(j == 0)
    def _():
        x = x_ref[...]
        xn_ref[...] = _rms(x, nw_ref[...]).astype(BF16)
        o_ref[...] = x

    h = jnp.dot(xn_ref[...], wu_ref[...], preferred_element_type=F32)
    h = jnp.square(jnp.maximum(h, 0.0)).astype(BF16)
    o_ref[...] += jnp.dot(h, wd_ref[...], preferred_element_type=F32)


def _mlp(x, nw, wu, wd, tm, tf):
    n, d = x.shape
    ff = wu.shape[1]
    return pl.pallas_call(
        _mlp_kernel,
        out_shape=jax.ShapeDtypeStruct((n, d), F32),
        grid=(n // tm, ff // tf),
        in_specs=[pl.BlockSpec((tm, d), lambda i, j: (i, 0)),
                  pl.BlockSpec((1, d), lambda i, j: (0, 0)),
                  pl.BlockSpec((d, tf), lambda i, j: (0, j)),
                  pl.BlockSpec((tf, d), lambda i, j: (j, 0))],
        out_specs=pl.BlockSpec((tm, d), lambda i, j: (i, 0)),
        scratch_shapes=[pltpu.VMEM((tm, d), BF16)],
        compiler_params=_cparams("parallel", "arbitrary"),
        name="mlp",
    )(x, nw, wu, wd)


def _final_norm_prompt_kernel(x_ref, nw_ref, o_ref, *, n_meta):
    tm = o_ref.shape[0]
    r0 = pl.multiple_of(n_meta + pl.program_id(1) * tm, 8)
    o_ref[...] = _rms(x_ref[pl.ds(r0, tm)], nw_ref[...])


def _final_norm_prompt(x, nw, n_meta):
    b, t, d = x.shape
    seq = t - n_meta
    tm = _pick_tile(seq, 512)
    return pl.pallas_call(
        functools.partial(_final_norm_prompt_kernel, n_meta=n_meta),
        out_shape=jax.ShapeDtypeStruct((b, seq, d), F32),
        grid=(b, seq // tm),
        in_specs=[pl.BlockSpec((None, t, d), lambda i, j: (i, 0, 0)),
                  pl.BlockSpec((1, d), lambda i, j: (0, 0))],
        out_specs=pl.BlockSpec((None, tm, d), lambda i, j: (i, j, 0)),
        compiler_params=_cparams("parallel", "arbitrary"),
        name="final_norm_prompt",
    )(x, nw)


def _final_norm_kernel(x_ref, nw_ref, o_ref):
    o_ref[...] = _rms(x_ref[...], nw_ref[...])


def _final_norm(x, nw):
    return pl.pallas_call(
        _final_norm_kernel,
        out_shape=jax.ShapeDtypeStruct(x.shape, F32),
        name="final_norm",
    )(x, nw)


def _rope_tables(pos):
    half = ROT_DIM // 2
    inv_freq = ROPE_THETA ** (-jnp.arange(half, dtype=F32) / half)
    ang = pos.astype(F32)[:, None] * inv_freq[None, :]
    cos, sin = jnp.cos(ang), jnp.sin(ang)
    n = pos.shape[0]
    zeros = jnp.zeros((n, half), F32)
    rest0 = jnp.zeros((n, HEAD_A - ROT_DIM), F32)
    cos_h = jnp.concatenate([cos, cos, jnp.ones((n, HEAD_A - ROT_DIM), F32)], axis=1)
    sin_m = jnp.concatenate([-sin, zeros, rest0], axis=1)
    sin_p = jnp.concatenate([zeros, sin, rest0], axis=1)
    return tuple(jnp.concatenate([t, t], axis=1) for t in (cos_h, sin_m, sin_p))


def kernel(x_prompt, x_sample, cache_k, cache_v, state_gla, state_hgrn, page_table, meta_tokens,
           norm_mix, norm_ffn, norm_final, w_in_a, lam_a, subln_a, w_out_a, w_in_b, w_gate2_b,
           b_gate_b, gnorm_b, w_out_b, w_in_c, lb_c, gnorm_c, w_out_c, w_up, w_down):
    bp, seq, d = x_prompt.shape
    db, n_new = x_sample.shape[:2]
    depth = norm_mix.shape[0]
    n_pages = page_table.shape[1]
    page = cache_k.shape[2]
    past_len = n_pages * page
    t_all = N_META_TOKENS + seq
    n_main = bp * t_all
    n_small = db * SAMPLE_ROWS
    h_a = d // (2 * HEAD_A)
    kd_b = w_gate2_b.shape[2]
    dv_b = d // H_B
    h_c = d // DK
    dv_c = d // h_c

    hp = jnp.concatenate(
        [jnp.broadcast_to(meta_tokens.astype(F32)[None], (bp, N_META_TOKENS, d)), x_prompt],
        axis=1).reshape(n_main, d)
    hs = jnp.pad(x_sample, ((0, 0), (0, SAMPLE_ROWS - n_new), (0, 0))).reshape(n_small, d)

    tm_main = _pick_tile(t_all, 704)
    tm_mlp = _pick_tile(n_main, 1400)
    tf = 512
    tab_p = _rope_tables(jnp.arange(t_all))
    tab_s = _rope_tables(jnp.tile(past_len + jnp.arange(SAMPLE_ROWS), db))
    ck = cache_k.reshape(cache_k.shape[0], cache_k.shape[1], page * 2 * h_a, HEAD_A)
    cv = cache_v.reshape(cache_v.shape[0], cache_v.shape[1], page * h_a, 2 * HEAD_A)
    pp = 8 if n_pages % 8 == 0 else 1

    kp_rows, vp_rows, ks_rows, vs_rows = [], [], [], []
    gla_p, gla_s, hgrn_p, hgrn_s = [], [], [], []
    for i in range(depth):
        kind, j = i % N_MIXERS, i // N_MIXERS
        nw = norm_mix[i].reshape(1, d)
        if kind == 0:
            lam_init = 0.8 - 0.6 * math.exp(-0.3 * i)
            w = w_in_a[j].astype(BF16)
            lam = lam_a[j].astype(F32)
            subln = subln_a[j].reshape(1, 2 * HEAD_A)
            qp, kp, vp = _proj_a(hp, nw, w, tab_p, tm_main, t_all // tm_main)
            qs, ks, vs = _proj_a(hs, nw, w, tab_s, n_small, 1)
            shp = (bp, t_all, d)
            ap = _attn_prompt(qp.reshape(shp), kp.reshape(shp), vp.reshape(shp), lam, subln,
                              lam_init, N_META_TOKENS, 2).reshape(n_main, d)
            ks4 = ks.reshape(db, SAMPLE_ROWS, 2 * h_a, HEAD_A)
            vs4 = vs.reshape(db, SAMPLE_ROWS, h_a, 2 * HEAD_A)
            q_rows = qs.reshape(db, SAMPLE_ROWS, 2 * h_a, HEAD_A)[:, :n_new]
            q_rows = q_rows.transpose(0, 2, 1, 3).reshape(db, 2 * h_a * n_new, HEAD_A)
            k_rows = ks4.reshape(db, SAMPLE_ROWS * 2 * h_a, HEAD_A)
            v_rows = jnp.repeat(vs4, 2, axis=2).reshape(db, SAMPLE_ROWS * 2 * h_a, 2 * HEAD_A)
            as_ = _attn_sample(q_rows, k_rows, v_rows, ck, cv, page_table, j, lam, subln,
                               lam_init, n_new, pp)
            w_out = w_out_a[j]
            kp_rows.append(kp.reshape(bp, t_all, 2 * h_a, HEAD_A))
            vp_rows.append(vp.reshape(bp, t_all, h_a, 2 * HEAD_A))
            ks_rows.append(ks4[:, :n_new])
            vs_rows.append(vs4[:, :n_new])
        elif kind == 1:
            w = w_in_b[j]
            n_main_cols = 2 * kd_b + 2 * d
            w_main = w[:, :n_main_cols].astype(BF16)
            w_gl = jnp.pad(w[:, n_main_cols:], ((0, 0), (0, LANES - GATE_RANK))).astype(BF16)
            w_g2 = jnp.pad(w_gate2_b[j], ((0, LANES - GATE_RANK), (0, 0))).astype(BF16)
            bg = b_gate_b[j].reshape(1, kd_b)
            gn = gnorm_b[j].reshape(1, dv_b)
            qp, kp, vp, rp, gp = _proj_b(hp, nw, w_main, w_gl, w_g2, bg, tm_main)
            qs, ks, vs, rs, gs = _proj_b(hs, nw, w_main, w_gl, w_g2, bg, n_small)
            r3 = lambda a: a.reshape(bp, t_all, a.shape[1])
            ap, sp = _recur_prompt(r3(qp), r3(kp), r3(gp), r3(vp), r3(rp), gn, N_META_TOKENS,
                                   MAIN_CHUNK, 2)
            ap = ap.reshape(n_main, d)
            as_, ss = _recur_sample(qs, ks, gs, vs, rs, state_gla[j].astype(F32), gn, n_new)
            gla_p.append(sp)
            gla_s.append(ss)
            w_out = w_out_b[j]
        else:
            w = w_in_c[j].astype(BF16)
            gn = gnorm_c[j].reshape(1, dv_c)
            lbf = lb_c.astype(F32)
            qp, kp, gp, vp, rp = _proj_c(hp, nw, w, lbf, i, tm_main)
            qs, ks, gs, vs, rs = _proj_c(hs, nw, w, lbf, i, n_small)
            r3 = lambda a: a.reshape(bp, t_all, a.shape[1])
            ap, sp = _recur_prompt(r3(qp), r3(kp), r3(gp), r3(vp), r3(rp), gn, N_META_TOKENS,
                                   MAIN_CHUNK, 4)
            ap = ap.reshape(n_main, d)
            as_, ss = _recur_sample(qs, ks, gs, vs, rs, state_hgrn[j].astype(F32), gn, n_new)
            hgrn_p.append(sp)
            hgrn_s.append(ss)
            w_out = w_out_c[j]
        w_out = w_out.astype(BF16)
        hp = _out_proj(ap, w_out, hp, tm_main)
        hs = _out_proj(as_, w_out, hs, n_small)
        nf = norm_ffn[i].reshape(1, d)
        wu, wd = w_up[i].astype(BF16), w_down[i].astype(BF16)
        hp = _mlp(hp, nf, wu, wd, tm_mlp, tf)
        hs = _mlp(hs, nf, wu, wd, n_small, tf)

    nfin = norm_final.reshape(1, d)
    y_prompt = _final_norm_prompt(hp.reshape(bp, t_all, d), nfin, N_META_TOKENS)
    y_sample = _final_norm(hs, nfin).reshape(db, SAMPLE_ROWS, d)[:, :n_new]
    return (y_prompt, y_sample,
            jnp.stack(kp_rows), jnp.stack(vp_rows), jnp.stack(ks_rows), jnp.stack(vs_rows),
            jnp.stack(gla_p), jnp.stack(gla_s), jnp.stack(hgrn_p), jnp.stack(hgrn_s))
```

```python
import functools
import math

import numpy as np
import jax
import jax.numpy as jnp
from jax import lax
from jax.experimental import pallas as pl
from jax.experimental.pallas import tpu as pltpu

F32 = jnp.float32
BF16 = jnp.bfloat16

RMS_EPS = 1e-6
N_META_TOKENS = 16
N_MIXERS = 3
HEAD_A = 64
ROT_DIM = HEAD_A // 4
ROPE_THETA = 500000.0
H_B = 4
GATE_RANK = 16
GATE_TEMP = 16.0
DK = 128
SAMPLE_ROWS = 8
SMALL_CHUNK = 16
MAIN_CHUNK = 64
LANES = 128
ONES_ROWS = 16
VMEM_LIMIT = 56 * 1024 * 1024
LOG2E = 1.4426950408889634

NT_DIMS = (((1,), (1,)), ((), ()))
TN_DIMS = (((0,), (0,)), ((), ()))


def _cparams(*sem):
    return pltpu.CompilerParams(dimension_semantics=sem, vmem_limit_bytes=VMEM_LIMIT)


def _pick_tile(n, target, mult=16):
    best = None
    for t in range(mult, min(n, target) + 1, mult):
        if n % t == 0:
            best = t
    return n if best is None else best


def _rms(x, w):
    ms = jnp.mean(x * x, axis=-1, keepdims=True)
    return x * lax.rsqrt(ms + RMS_EPS) * w


def _sigmoid(x):
    return 1.0 / (1.0 + jnp.exp(-x))


def _proj_a_kernel(x_ref, nw_ref, w_ref, c_ref, sm_ref, sp_ref, q_ref, k_ref, v_ref, *, cw):
    xn = _rms(x_ref[...], nw_ref[...]).astype(BF16)
    d = q_ref.shape[1]
    rep = cw // LANES
    cos = jnp.tile(c_ref[...], (1, rep))
    sin_m = jnp.tile(sm_ref[...], (1, rep))
    sin_p = jnp.tile(sp_ref[...], (1, rep))
    for dst_i, dst in enumerate((q_ref, k_ref, v_ref)):
        for c in range(d // cw):
            col = dst_i * d + c * cw
            y = jnp.dot(xn, w_ref[:, col:col + cw], preferred_element_type=F32)
            if dst_i < 2:
                y = (y * cos + pltpu.roll(y, ROT_DIM // 2, 1) * sin_p
                     + pltpu.roll(y, cw - ROT_DIM // 2, 1) * sin_m)
            dst[:, c * cw:(c + 1) * cw] = y


def _proj_b_kernel(x_ref, nw_ref, w_ref, wgl_ref, wg2_ref, bg_ref,
                   q_ref, k_ref, v_ref, r_ref, g_ref, *, cw):
    xn = _rms(x_ref[...], nw_ref[...]).astype(BF16)
    col = 0
    for dst, scale in ((q_ref, DK ** -0.5), (k_ref, None), (v_ref, None), (r_ref, None)):
        for c in range(dst.shape[1] // cw):
            y = jnp.dot(xn, w_ref[:, col:col + cw], preferred_element_type=F32)
            if scale is not None:
                y = y * scale
            dst[:, c * cw:(c + 1) * cw] = y.astype(dst.dtype)
            col += cw
    gl = jnp.dot(xn, wgl_ref[...], preferred_element_type=F32).astype(BF16)
    z = jnp.dot(gl, wg2_ref[...], preferred_element_type=F32) + bg_ref[...]
    g_ref[...] = (jnp.minimum(z, 0.0) - jnp.log(1.0 + jnp.exp(-jnp.abs(z)))) * (1.0 / GATE_TEMP)


def _proj_c_kernel(x_ref, nw_ref, w_ref, lb_ref, q_ref, k_ref, g_ref, v_ref, r_ref, *, cw, layer):
    xn = _rms(x_ref[...], nw_ref[...]).astype(BF16)
    d = q_ref.shape[1]
    lbs = lb_ref[...]
    mx = jnp.max(lbs, axis=0, keepdims=True)
    e = jnp.exp(lbs - mx)
    lb = jnp.sum(e[1:layer + 1], axis=0, keepdims=True) / jnp.sum(e, axis=0, keepdims=True)
    for c in range(d // cw):
        sl = slice(c * cw, (c + 1) * cw)
        y = jnp.dot(xn, w_ref[:, c * cw:(c + 1) * cw], preferred_element_type=F32)
        q_ref[:, sl] = y * _sigmoid(y) * (DK ** -0.5)
        f = jnp.dot(xn, w_ref[:, d + c * cw:d + (c + 1) * cw], preferred_element_type=F32)
        lbc = lb[:, sl]
        forget = lbc + (1.0 - lbc) * _sigmoid(f)
        k_ref[:, sl] = 1.0 - forget
        g_ref[:, sl] = jnp.log(forget)
        v_ref[:, sl] = jnp.dot(xn, w_ref[:, 2 * d + c * cw:2 * d + (c + 1) * cw],
                               preferred_element_type=F32).astype(v_ref.dtype)
        r_ref[:, sl] = jnp.dot(xn, w_ref[:, 3 * d + c * cw:3 * d + (c + 1) * cw],
                               preferred_element_type=F32)


def _row_spec(tm, width):
    return pl.BlockSpec((tm, width), lambda i: (i, 0))


def _full_spec(shape):
    return pl.BlockSpec(shape, lambda i: (0,) * len(shape))


def _proj_a(x, nw, w, tables, tm, table_blocks):
    n, d = x.shape
    cw = min(512, d)
    tab_spec = pl.BlockSpec((tm, LANES), lambda i: (i % table_blocks, 0))
    out = jax.ShapeDtypeStruct((n, d), F32)
    return pl.pallas_call(
        functools.partial(_proj_a_kernel, cw=cw),
        out_shape=(out, out, out),
        grid=(n // tm,),
        in_specs=[_row_spec(tm, d), _full_spec((1, d)), _full_spec(w.shape),
                  tab_spec, tab_spec, tab_spec],
        out_specs=(_row_spec(tm, d),) * 3,
        compiler_params=_cparams("parallel"),
        name="proj_a",
    )(x, nw, w, *tables)


def _proj_b(x, nw, w, wgl, wg2, bg, tm):
    n, d = x.shape
    kd, vd = wg2.shape[1], d
    cw = min(512, kd)
    return pl.pallas_call(
        functools.partial(_proj_b_kernel, cw=cw),
        out_shape=(jax.ShapeDtypeStruct((n, kd), F32), jax.ShapeDtypeStruct((n, kd), F32),
                   jax.ShapeDtypeStruct((n, vd), BF16), jax.ShapeDtypeStruct((n, vd), F32),
                   jax.ShapeDtypeStruct((n, kd), F32)),
        grid=(n // tm,),
        in_specs=[_row_spec(tm, d), _full_spec((1, d)), _full_spec(w.shape),
                  _full_spec(wgl.shape), _full_spec(wg2.shape), _full_spec((1, kd))],
        out_specs=(_row_spec(tm, kd), _row_spec(tm, kd), _row_spec(tm, vd), _row_spec(tm, vd),
                   _row_spec(tm, kd)),
        compiler_params=_cparams("parallel"),
        name="proj_b",
    )(x, nw, w, wgl, wg2, bg)


def _proj_c(x, nw, w, lb_c, layer, tm):
    n, d = x.shape
    cw = min(512, d)
    out = jax.ShapeDtypeStruct((n, d), F32)
    return pl.pallas_call(
        functools.partial(_proj_c_kernel, cw=cw, layer=layer),
        out_shape=(out, out, out, jax.ShapeDtypeStruct((n, d), BF16), out),
        grid=(n // tm,),
        in_specs=[_row_spec(tm, d), _full_spec((1, d)), _full_spec(w.shape),
                  _full_spec(lb_c.shape)],
        out_specs=(_row_spec(tm, d),) * 5,
        compiler_params=_cparams("parallel"),
        name="proj_c",
    )(x, nw, w, lb_c)


def _lambda_full(lam_ref, lam_init):
    l = lam_ref[...]
    a = jnp.sum(l[0:1] * l[1:2], axis=-1, keepdims=True)
    b = jnp.sum(l[2:3] * l[3:4], axis=-1, keepdims=True)
    return jnp.exp(a) - jnp.exp(b) + lam_init


def _head_finish(o, subln, lam_init):
    ms = jnp.mean(o * o, axis=-1, keepdims=True)
    return o * lax.rsqrt(ms + RMS_EPS) * subln * (1.0 - lam_init)


def _attn_prompt_kernel(lam_ref, subln_ref, q_ref, k_ref, v_ref, o_ref,
                        kb_ref, vt_ref, vt2_ref, km_ref, vtm_ref, m_ref, acc_ref, bias_ref,
                        *, tq, n_meta, lam_init, hb):
    t_all = q_ref.shape[0]
    nq = (t_all - n_meta) // tq
    hw = 2 * HEAD_A
    lam = _lambda_full(lam_ref, lam_init)
    subln = subln_ref[...]
    first_map = lax.broadcasted_iota(jnp.int32, (1, hw), 1) < HEAD_A

    def stack_q(qt, scale):
        qt = qt * scale
        return jnp.concatenate([jnp.where(first_map, qt, 0.0), jnp.where(first_map, 0.0, qt)],
                               axis=0).astype(BF16)

    r = lax.broadcasted_iota(jnp.int32, (2 * n_meta, n_meta), 0)
    c = lax.broadcasted_iota(jnp.int32, (2 * n_meta, n_meta), 1)
    meta_causal = c <= jnp.where(r >= n_meta, r - n_meta, r)
    pad = jnp.zeros((hw - n_meta, hw), F32)
    for h in range(hb):
        lanes = slice(h * hw, (h + 1) * hw)
        kmeta = k_ref[0:n_meta, lanes]
        vmeta = v_ref[0:n_meta, lanes]
        qs = stack_q(q_ref[0:n_meta, lanes], HEAD_A ** -0.5)
        s = lax.dot_general(qs, kmeta.astype(BF16), NT_DIMS, preferred_element_type=F32)
        s = jnp.where(meta_causal, s, -jnp.inf)
        p = jnp.exp(s - jnp.max(s, axis=-1, keepdims=True))
        acc = jnp.dot(p.astype(BF16), vmeta.astype(BF16), preferred_element_type=F32)
        acc = acc / jnp.sum(p, axis=-1, keepdims=True)
        o = acc[:n_meta] - lam * acc[n_meta:]
        o_ref[0:n_meta, lanes] = _head_finish(o, subln, lam_init).astype(o_ref.dtype)
        km_ref[h] = jnp.concatenate([kmeta, pad], axis=0).astype(BF16)
        vtm_ref[h] = jnp.concatenate(
            [jnp.concatenate([vmeta, pad], axis=0).T.astype(BF16), jnp.ones((ONES_ROWS, hw), BF16)],
            axis=0)
        for cidx in range(nq):
            rows = slice(n_meta + cidx * tq, n_meta + (cidx + 1) * tq)
            kb_ref[h, cidx] = k_ref[rows, lanes].astype(BF16)
            vte = jnp.concatenate([v_ref[rows, lanes].T.astype(BF16),
                                   jnp.ones((ONES_ROWS, tq), BF16)], axis=0)
            vt_ref[h, cidx] = vte
            if cidx < 2 * (nq // 2):
                vt2_ref[h, cidx // 2, :, (cidx % 2) * tq:(cidx % 2 + 1) * tq] = vte

    key_i = lax.broadcasted_iota(jnp.int32, (hw + tq, 2 * tq), 0)
    qry_i = lax.broadcasted_iota(jnp.int32, (hw + tq, 2 * tq), 1)
    key_limit = jnp.where(key_i < hw, n_meta - 1, hw + jnp.where(qry_i >= tq, qry_i - tq, qry_i))
    bias_ref[...] = jnp.where(key_i <= key_limit, 0.0, -jnp.inf)

    def online(h, s, vte):
        m_old = m_ref[h]
        m_new = jnp.maximum(m_old, jnp.max(s, axis=0, keepdims=True))
        alpha = jnp.exp2(m_old - m_new)
        p = jnp.exp2(s - m_new).astype(BF16)
        acc_ref[h] = alpha * acc_ref[h] + jnp.dot(vte, p, preferred_element_type=F32)
        m_ref[h] = m_new

    def q_body(qi, _):
        q0 = pl.multiple_of(n_meta + qi * tq, 16)
        qss = [stack_q(q_ref[pl.ds(q0, tq), h * hw:(h + 1) * hw], (HEAD_A ** -0.5) * LOG2E)
               for h in range(hb)]
        for h in range(hb):
            s = jnp.concatenate(
                [lax.dot_general(km_ref[h], qss[h], NT_DIMS, preferred_element_type=F32),
                 lax.dot_general(kb_ref[h, qi], qss[h], NT_DIMS, preferred_element_type=F32)],
                axis=0)
            s = s + bias_ref[...]
            m = jnp.max(s, axis=0, keepdims=True)
            p = jnp.exp2(s - m).astype(BF16)
            m_ref[h] = m
            acc_ref[h] = (jnp.dot(vtm_ref[h], p[:hw], preferred_element_type=F32)
                          + jnp.dot(vt_ref[h, qi], p[hw:], preferred_element_type=F32))

        def pair_body(j, _):
            for h in range(hb):
                kpair = kb_ref[h, pl.ds(2 * j, 2)].reshape(2 * tq, hw)
                s = lax.dot_general(kpair, qss[h], NT_DIMS, preferred_element_type=F32)
                online(h, s, vt2_ref[h, j])
            return 0

        lax.fori_loop(0, qi // 2, pair_body, 0)

        @pl.when(qi % 2 == 1)
        def _():
            for h in range(hb):
                s = lax.dot_general(kb_ref[h, qi - 1], qss[h], NT_DIMS,
                                    preferred_element_type=F32)
                online(h, s, vt_ref[h, qi - 1])

        for h in range(hb):
            acc = acc_ref[h]
            ot = acc[:hw] * (1.0 / acc[hw:hw + 1])
            od = ot[:, :tq] - lam * ot[:, tq:]
            ms = jnp.mean(od * od, axis=0, keepdims=True)
            od = od * lax.rsqrt(ms + RMS_EPS)
            o_ref[pl.ds(q0, tq), h * hw:(h + 1) * hw] = (
                od.T * (subln * (1.0 - lam_init))).astype(o_ref.dtype)
        return 0

    lax.fori_loop(0, nq, q_body, 0)


def _attn_prompt(q, k, v, lam, subln, lam_init, n_meta, hb):
    b, t, d = q.shape
    hw = 2 * HEAD_A
    tq = _pick_tile(t - n_meta, 256, LANES)
    nq = (t - n_meta) // tq
    blk = pl.BlockSpec((None, t, hb * hw), lambda i, h: (i, 0, h))
    return pl.pallas_call(
        functools.partial(_attn_prompt_kernel, tq=tq, n_meta=n_meta, lam_init=lam_init, hb=hb),
        out_shape=jax.ShapeDtypeStruct((b, t, d), BF16),
        grid=(b, d // (hb * hw)),
        in_specs=[pl.BlockSpec(lam.shape, lambda i, h: (0, 0)),
                  pl.BlockSpec((1, hw), lambda i, h: (0, 0)), blk, blk, blk],
        out_specs=blk,
        scratch_shapes=[pltpu.VMEM((hb, nq, tq, hw), BF16),
                        pltpu.VMEM((hb, nq, hw + ONES_ROWS, tq), BF16),
                        pltpu.VMEM((hb, max(nq // 2, 1), hw + ONES_ROWS, 2 * tq), BF16),
                        pltpu.VMEM((hb, hw, hw), BF16), pltpu.VMEM((hb, hw + ONES_ROWS, hw), BF16),
                        pltpu.VMEM((hb, 1, 2 * tq), F32),
                        pltpu.VMEM((hb, hw + ONES_ROWS, 2 * tq), F32),
                        pltpu.VMEM((hw + tq, 2 * tq), F32)],
        compiler_params=_cparams("parallel", "parallel"),
        name="attn_prompt",
    )(lam, subln, q, k, v)


def _attn_sample_kernel(pt_ref, lam_ref, subln_ref, q_ref, kn_ref, vn_ref, *rest,
                        pp, n_new, lam_init):
    del pt_ref
    k_refs, v_refs = rest[:pp], rest[pp:2 * pp]
    o_ref, qbd_ref, m_ref, l_ref, acc_ref = rest[2 * pp:]
    step = pl.program_id(1)
    rows = q_ref.shape[0]
    n_heads = rows // n_new
    n_vheads = n_heads // 2
    page = k_refs[0].shape[1]
    grp = 4 * n_new

    def scores(k_list):
        qbd = qbd_ref[...]
        return jnp.concatenate(
            [jnp.dot(qbd, kr[...].astype(BF16), preferred_element_type=F32) for kr in k_list],
            axis=1)

    def values(p, v_list):
        out = []
        for g in range(rows // grp):
            pg = p[g * grp:(g + 1) * grp]
            halves = []
            for vh in (2 * g, 2 * g + 1):
                r = None
                for i, vr in enumerate(v_list):
                    v = vr[pl.ds(vh, page, stride=n_vheads), :].astype(BF16)
                    t = jnp.dot(pg[:, i * page:(i + 1) * page], v, preferred_element_type=F32)
                    r = t if r is None else r + t
                halves.append(r)
            out.append(halves[0][:grp // 2])
            out.append(halves[1][grp // 2:])
        return jnp.concatenate(out, axis=0)

    @pl.when(step == 0)
    def _():
        d = n_heads * HEAD_A
        q = jnp.tile(q_ref[...] * ((HEAD_A ** -0.5) * LOG2E), (1, n_heads))
        row_head = lax.broadcasted_iota(jnp.int32, (rows, d), 0) // n_new
        lane_head = lax.broadcasted_iota(jnp.int32, (rows, d), 1) // HEAD_A
        qbd_ref[...] = jnp.where(row_head == lane_head, q, 0.0).astype(BF16)
        s = scores([kn_ref])
        tok = lax.broadcasted_iota(jnp.int32, (rows, page), 0) % n_new
        key = lax.broadcasted_iota(jnp.int32, (rows, page), 1)
        s = jnp.where(key <= tok, s, -jnp.inf)
        m = jnp.max(s, axis=-1, keepdims=True)
        p = jnp.exp2(s - m)
        m_ref[...] = m
        l_ref[...] = jnp.sum(p, axis=-1, keepdims=True)
        acc_ref[...] = values(p.astype(BF16), [vn_ref])

    s = scores(k_refs)
    m_old = m_ref[...]
    m_new = jnp.maximum(m_old, jnp.max(s, axis=-1, keepdims=True))
    alpha = jnp.exp2(m_old - m_new)
    p = jnp.exp2(s - m_new)
    l_ref[...] = alpha * l_ref[...] + jnp.sum(p, axis=-1, keepdims=True)
    acc_ref[...] = alpha * acc_ref[...] + values(p.astype(BF16), v_refs)
    m_ref[...] = m_new

    @pl.when(step == pl.num_programs(1) - 1)
    def _():
        lam = _lambda_full(lam_ref, lam_init)
        subln = subln_ref[...]
        hw = 2 * HEAD_A
        full = acc_ref[...] / l_ref[...]
        diff = full - lam * pltpu.roll(full, rows - n_new, 0)
        for h in range(n_heads // 2):
            tile = diff[2 * n_new * h:2 * n_new * (h + 1)]
            o_ref[:, h * hw:(h + 1) * hw] = _head_finish(tile, subln, lam_init).astype(o_ref.dtype)


def _attn_sample(q, k_new, v_new, cache_k, cache_v, page_table, layer_j, lam, subln, lam_init,
                 n_new, pp):
    nb, rows, _ = q.shape
    n_pages = page_table.shape[1]
    krows, page = cache_k.shape[2:]
    vrows = cache_v.shape[2]
    hw = 2 * HEAD_A
    d = (rows // n_new) * HEAD_A
    assert 2 * n_new == SAMPLE_ROWS and krows == d and vrows * 2 * HEAD_A == page * d

    def per_sample(shape):
        return pl.BlockSpec((None,) + shape, lambda b, s, pt: (b, 0, 0))

    def kpage(i):
        return pl.BlockSpec((None, None, krows, page),
                            lambda b, s, pt: (layer_j, pt[b, s * pp + i], 0, 0))

    def vpage(i):
        return pl.BlockSpec((None, None, vrows, hw),
                            lambda b, s, pt: (layer_j, pt[b, s * pp + i], 0, 0))

    grid_spec = pltpu.PrefetchScalarGridSpec(
        num_scalar_prefetch=1,
        grid=(nb, n_pages // pp),
        in_specs=[pl.BlockSpec(lam.shape, lambda b, s, pt: (0, 0)),
                  pl.BlockSpec((1, hw), lambda b, s, pt: (0, 0)),
                  per_sample(q.shape[1:]), per_sample(k_new.shape[1:]),
                  per_sample(v_new.shape[1:])]
                 + [kpage(i) for i in range(pp)] + [vpage(i) for i in range(pp)],
        out_specs=pl.BlockSpec((SAMPLE_ROWS, d), lambda b, s, pt: (b, 0)),
        scratch_shapes=[pltpu.VMEM((rows, d), BF16), pltpu.VMEM((rows, 1), F32),
                        pltpu.VMEM((rows, 1), F32), pltpu.VMEM((rows, hw), F32)],
    )
    return pl.pallas_call(
        functools.partial(_attn_sample_kernel, pp=pp, n_new=n_new, lam_init=lam_init),
        out_shape=jax.ShapeDtypeStruct((nb * SAMPLE_ROWS, d), BF16),
        grid_spec=grid_spec,
        compiler_params=_cparams("parallel", "arbitrary"),
        name="attn_sample",
    )(page_table, lam, subln, q, k_new, v_new, *([cache_k] * pp), *([cache_v] * pp))


def _chunk_constants(length):
    nlev = int(math.log2(length))
    assert 2 ** nlev == length
    w = np.zeros(((2 + nlev) * length, length), np.float32)
    masks = np.zeros((nlev + 1, length, length), np.float32)
    masks[0] = np.eye(length)
    for t in range(length):
        w[t, :t + 1] = 1.0
        w[length + t, t + 1:] = 1.0
    for lvl in range(1, nlev + 1):
        bs, half = 2 ** lvl, 2 ** (lvl - 1)
        for t in range(length):
            mid = t - t % bs + half
            row = (1 + lvl) * length + t
            if t >= mid:
                w[row, mid:t + 1] = 1.0
                masks[lvl, t, mid - half:mid] = 1.0
            else:
                w[row, t + 1:mid] = 1.0
    return jnp.asarray(np.tile(w, (1, 3)), BF16), jnp.asarray(masks, F32)


def _chunk_exponents(g, w3):
    g = g * LOG2E
    g1 = g.astype(BF16)
    rem = g - g1.astype(F32)
    g2 = rem.astype(BF16)
    g3 = (rem - g2.astype(F32)).astype(BF16)
    return jnp.exp2(jnp.dot(w3, jnp.concatenate([g1, g2, g3], axis=0),
                            preferred_element_type=F32))


def _chunk_head(q, k, vb, st, ex, masks):
    length = q.shape[0]
    nlev = masks.shape[0] - 1
    e_cum = ex[0:length]
    e_rem = ex[length:2 * length]
    e_last = e_cum[length - 1:length]
    o = lax.dot_general((q * e_cum).astype(BF16), st.astype(BF16), NT_DIMS,
                        preferred_element_type=F32)
    qb, kb = q.astype(BF16), k.astype(BF16)
    a = masks[0] * lax.dot_general(qb, kb, NT_DIMS, preferred_element_type=F32)
    row = lax.broadcasted_iota(jnp.int32, q.shape, 0)
    for lvl in range(1, nlev + 1):
        second_half = (row & (2 ** lvl - 1)) >= 2 ** (lvl - 1)
        x = (jnp.where(second_half, q, k) * ex[(1 + lvl) * length:(2 + lvl) * length]).astype(BF16)
        a = a + masks[lvl] * lax.dot_general(x, x, NT_DIMS, preferred_element_type=F32)
    o = o + jnp.dot(a.astype(BF16), vb, preferred_element_type=F32)
    kd = (k * e_rem).astype(BF16)
    st = st * e_last + lax.dot_general(vb, kd, TN_DIMS, preferred_element_type=F32)
    return o, st


def _gate_norm(o, r, gn):
    ms = jnp.mean(o * o, axis=-1, keepdims=True)
    return o * lax.rsqrt(ms + RMS_EPS) * gn * (r * _sigmoid(r))


def _recur_prompt_kernel(gn_ref, ws_ref, ms_ref, wm_ref, mm_ref, q_ref, k_ref, g_ref, v_ref, r_ref,
                         o_ref, s_ref, st_ref, *, n_meta, chunk, hb):
    t_all = q_ref.shape[0]
    dv = v_ref.shape[1] // hb
    gn = gn_ref[...]
    st_ref[...] = jnp.zeros_like(st_ref)

    def run(r0, length, w3, masks):
        rows = pl.ds(r0, length)
        ex = _chunk_exponents(g_ref[rows], w3)
        for h in range(hb):
            ksl = slice(h * DK, (h + 1) * DK)
            vsl = slice(h * dv, (h + 1) * dv)
            o, st = _chunk_head(q_ref[rows, ksl], k_ref[rows, ksl], v_ref[rows, vsl], st_ref[h],
                                ex[:, ksl], masks)
            st_ref[h] = st
            o_ref[rows, vsl] = _gate_norm(o, r_ref[rows, vsl], gn).astype(o_ref.dtype)

    run(0, n_meta, ws_ref[...], ms_ref[...])

    def body(c, _):
        run(pl.multiple_of(n_meta + c * chunk, 16), chunk, wm_ref[...], mm_ref[...])
        return 0

    lax.fori_loop(0, (t_all - n_meta) // chunk, body, 0, unroll=2)
    for h in range(hb):
        s_ref[h] = st_ref[h].T


def _recur_prompt(q, k, g, v, r, gn, n_meta, chunk, hb):
    b, t, kd = q.shape
    vd = v.shape[2]
    nh = kd // DK
    dv = vd // nh
    ws, ms = _chunk_constants(n_meta)
    wm, mm = _chunk_constants(chunk)
    kblk = pl.BlockSpec((None, t, hb * DK), lambda i, h: (i, 0, h))
    vblk = pl.BlockSpec((None, t, hb * dv), lambda i, h: (i, 0, h))

    def const(x):
        return pl.BlockSpec(x.shape, lambda i, h: (0,) * x.ndim)

    return pl.pallas_call(
        functools.partial(_recur_prompt_kernel, n_meta=n_meta, chunk=chunk, hb=hb),
        out_shape=(jax.ShapeDtypeStruct((b, t, vd), BF16),
                   jax.ShapeDtypeStruct((b, nh, DK, dv), F32)),
        grid=(b, nh // hb),
        in_specs=[pl.BlockSpec((1, dv), lambda i, h: (0, 0)), const(ws), const(ms), const(wm),
                  const(mm), kblk, kblk, kblk, vblk, vblk],
        out_specs=(vblk, pl.BlockSpec((None, hb, DK, dv), lambda i, h: (i, h, 0, 0))),
        scratch_shapes=[pltpu.VMEM((hb, dv, DK), F32)],
        compiler_params=_cparams("parallel", "parallel"),
        name="recur_prompt",
    )(gn, ws, ms, wm, mm, q, k, g, v, r)


def _recur_sample_kernel(gn_ref, w_ref, m_ref, q_ref, k_ref, g_ref, v_ref, r_ref, s0_ref,
                         o_ref, s_ref, *, n_new):
    nh = s0_ref.shape[0]
    dv = s0_ref.shape[2]
    gn = gn_ref[...]
    masks = m_ref[...]
    pad_rows = SMALL_CHUNK - SAMPLE_ROWS
    kd = q_ref.shape[1]
    valid = lax.broadcasted_iota(jnp.int32, (SAMPLE_ROWS, kd), 0) < n_new

    def padded(x):
        return jnp.concatenate([x, jnp.zeros((pad_rows, x.shape[1]), x.dtype)], axis=0)

    q = padded(q_ref[...])
    k = padded(jnp.where(valid, k_ref[...], 0.0))
    ex = _chunk_exponents(padded(jnp.where(valid, g_ref[...], 0.0)), w_ref[...])
    vb = padded(v_ref[...].astype(F32)).astype(BF16)
    for h in range(nh):
        ksl = slice(h * DK, (h + 1) * DK)
        vsl = slice(h * dv, (h + 1) * dv)
        o, st = _chunk_head(q[:, ksl], k[:, ksl], vb[:, vsl], s0_ref[h].T, ex[:, ksl], masks)
        s_ref[h] = st.T
        o_ref[:, vsl] = _gate_norm(o[0:SAMPLE_ROWS], r_ref[:, vsl], gn).astype(o_ref.dtype)


def _recur_sample(q, k, g, v, r, s0, gn, n_new):
    n, kd = q.shape
    vd = v.shape[1]
    nb, nh, _, dv = s0.shape
    w, masks = _chunk_constants(SMALL_CHUNK)
    kblk = pl.BlockSpec((SAMPLE_ROWS, kd), lambda i: (i, 0))
    vblk = pl.BlockSpec((SAMPLE_ROWS, vd), lambda i: (i, 0))
    sblk = pl.BlockSpec((None, nh, DK, dv), lambda i: (i, 0, 0, 0))
    return pl.pallas_call(
        functools.partial(_recur_sample_kernel, n_new=n_new),
        out_shape=(jax.ShapeDtypeStruct((n, vd), BF16), jax.ShapeDtypeStruct(s0.shape, F32)),
        grid=(nb,),
        in_specs=[_full_spec((1, dv)), _full_spec(w.shape), _full_spec(masks.shape),
                  kblk, kblk, kblk, vblk, vblk, sblk],
        out_specs=(vblk, sblk),
        compiler_params=_cparams("parallel"),
        name="recur_sample",
    )(gn, w, masks, q, k, g, v, r, s0)


def _out_proj_kernel(a_ref, w_ref, res_ref, o_ref):
    o_ref[...] = res_ref[...] + jnp.dot(a_ref[...], w_ref[...], preferred_element_type=F32)


def _out_proj(a, w, res, tm):
    n, d = res.shape
    return pl.pallas_call(
        _out_proj_kernel,
        out_shape=jax.ShapeDtypeStruct((n, d), F32),
        grid=(n // tm,),
        in_specs=[_row_spec(tm, a.shape[1]), _full_spec(w.shape), _row_spec(tm, d)],
        out_specs=_row_spec(tm, d),
        compiler_params=_cparams("parallel"),
        name="out_proj",
    )(a, w, res)


def _mlp_kernel(x_ref, nw_ref, wu_ref, wd_ref, o_ref, xn_ref):
    j = pl.program_id(1)

    @pl.when(j == 0)
    def _():
        x = x_ref[...]
        xn_ref[...] = _rms(x, nw_ref[...]).astype(BF16)
        o_ref[...] = x

    h = jnp.dot(xn_ref[...], wu_ref[...], preferred_element_type=F32)
    h = jnp.square(jnp.maximum(h, 0.0)).astype(BF16)
    o_ref[...] += jnp.dot(h, wd_ref[...], preferred_element_type=F32)


def _mlp(x, nw, wu, wd, tm, tf):
    n, d = x.shape
    ff = wu.shape[1]
    return pl.pallas_call(
        _mlp_kernel,
        out_shape=jax.ShapeDtypeStruct((n, d), F32),
        grid=(n // tm, ff // tf),
        in_specs=[pl.BlockSpec((tm, d), lambda i, j: (i, 0)),
                  pl.BlockSpec((1, d), lambda i, j: (0, 0)),
                  pl.BlockSpec((d, tf), lambda i, j: (0, j)),
                  pl.BlockSpec((tf, d), lambda i, j: (j, 0))],
        out_specs=pl.BlockSpec((tm, d), lambda i, j: (i, 0)),
        scratch_shapes=[pltpu.VMEM((tm, d), BF16)],
        compiler_params=_cparams("parallel", "arbitrary"),
        name="mlp",
    )(x, nw, wu, wd)


def _final_norm_prompt_kernel(x_ref, nw_ref, o_ref, *, n_meta):
    tm = o_ref.shape[0]
    r0 = pl.multiple_of(n_meta + pl.program_id(1) * tm, 8)
    o_ref[...] = _rms(x_ref[pl.ds(r0, tm)], nw_ref[...])


def _final_norm_prompt(x, nw, n_meta):
    b, t, d = x.shape
    seq = t - n_meta
    tm = _pick_tile(seq, 512)
    return pl.pallas_call(
        functools.partial(_final_norm_prompt_kernel, n_meta=n_meta),
        out_shape=jax.ShapeDtypeStruct((b, seq, d), F32),
        grid=(b, seq // tm),
        in_specs=[pl.BlockSpec((None, t, d), lambda i, j: (i, 0, 0)),
                  pl.BlockSpec((1, d), lambda i, j: (0, 0))],
        out_specs=pl.BlockSpec((None, tm, d), lambda i, j: (i, j, 0)),
        compiler_params=_cparams("parallel", "arbitrary"),
        name="final_norm_prompt",
    )(x, nw)


def _final_norm_kernel(x_ref, nw_ref, o_ref):
    o_ref[...] = _rms(x_ref[...], nw_ref[...])


def _final_norm(x, nw):
    return pl.pallas_call(
        _final_norm_kernel,
        out_shape=jax.ShapeDtypeStruct(x.shape, F32),
        name="final_norm",
    )(x, nw)


def _rope_tables(pos):
    half = ROT_DIM // 2
    inv_freq = ROPE_THETA ** (-jnp.arange(half, dtype=F32) / half)
    ang = pos.astype(F32)[:, None] * inv_freq[None, :]
    cos, sin = jnp.cos(ang), jnp.sin(ang)
    n = pos.shape[0]
    zeros = jnp.zeros((n, half), F32)
    rest0 = jnp.zeros((n, HEAD_A - ROT_DIM), F32)
    cos_h = jnp.concatenate([cos, cos, jnp.ones((n, HEAD_A - ROT_DIM), F32)], axis=1)
    sin_m = jnp.concatenate([-sin, zeros, rest0], axis=1)
    sin_p = jnp.concatenate([zeros, sin, rest0], axis=1)
    return tuple(jnp.concatenate([t, t], axis=1) for t in (cos_h, sin_m, sin_p))


def kernel(x_prompt, x_sample, cache_k, cache_v, state_gla, state_hgrn, page_table, meta_tokens,
           norm_mix, norm_ffn, norm_final, w_in_a, lam_a, subln_a, w_out_a, w_in_b, w_gate2_b,
           b_gate_b, gnorm_b, w_out_b, w_in_c, lb_c, gnorm_c, w_out_c, w_up, w_down):
    bp, seq, d = x_prompt.shape
    db, n_new = x_sample.shape[:2]
    depth = norm_mix.shape[0]
    n_pages = page_table.shape[1]
    page = cache_k.shape[2]
    past_len = n_pages * page
    t_all = N_META_TOKENS + seq
    n_main = bp * t_all
    n_small = db * SAMPLE_ROWS
    h_a = d // (2 * HEAD_A)
    kd_b = w_gate2_b.shape[2]
    dv_b = d // H_B
    h_c = d // DK
    dv_c = d // h_c

    hp = jnp.concatenate(
        [jnp.broadcast_to(meta_tokens.astype(F32)[None], (bp, N_META_TOKENS, d)), x_prompt],
        axis=1).reshape(n_main, d)
    hs = jnp.pad(x_sample, ((0, 0), (0, SAMPLE_ROWS - n_new), (0, 0))).reshape(n_small, d)

    tm_main = _pick_tile(t_all, 704)
    tm_mlp = _pick_tile(n_main, 1400)
    tf = 512
    tab_p = _rope_tables(jnp.arange(t_all))
    tab_s = _rope_tables(jnp.tile(past_len + jnp.arange(SAMPLE_ROWS), db))
    ck = cache_k.transpose(0, 1, 3, 4, 2).reshape(cache_k.shape[0], cache_k.shape[1], d, page)
    cv = cache_v.reshape(cache_v.shape[0], cache_v.shape[1], page * h_a, 2 * HEAD_A)
    pp = 16 if n_pages % 16 == 0 else 1

    kp_rows, vp_rows, ks_rows, vs_rows = [], [], [], []
    gla_p, gla_s, hgrn_p, hgrn_s = [], [], [], []
    for i in range(depth):
        kind, j = i % N_MIXERS, i // N_MIXERS
        nw = norm_mix[i].reshape(1, d)
        if kind == 0:
            lam_init = 0.8 - 0.6 * math.exp(-0.3 * i)
            w = w_in_a[j].astype(BF16)
            lam = lam_a[j].astype(F32)
            subln = subln_a[j].reshape(1, 2 * HEAD_A)
            qp, kp, vp = _proj_a(hp, nw, w, tab_p, tm_main, t_all // tm_main)
            qs, ks, vs = _proj_a(hs, nw, w, tab_s, n_small, 1)
            shp = (bp, t_all, d)
            ap = _attn_prompt(qp.reshape(shp), kp.reshape(shp), vp.reshape(shp), lam, subln,
                              lam_init, N_META_TOKENS, 4).reshape(n_main, d)
            ks4 = ks.reshape(db, SAMPLE_ROWS, 2 * h_a, HEAD_A)
            vs4 = vs.reshape(db, SAMPLE_ROWS, h_a, 2 * HEAD_A)
            q_rows = qs.reshape(db, SAMPLE_ROWS, 2 * h_a, HEAD_A)[:, :n_new]
            q_rows = q_rows.transpose(0, 2, 1, 3).reshape(db, 2 * h_a * n_new, HEAD_A)
            k_rows = jnp.pad(ks.reshape(db, SAMPLE_ROWS, d).transpose(0, 2, 1),
                             ((0, 0), (0, 0), (0, page - SAMPLE_ROWS)))
            v_rows = jnp.pad(vs4.reshape(db, SAMPLE_ROWS * h_a, 2 * HEAD_A),
                             ((0, 0), (0, (page - SAMPLE_ROWS) * h_a), (0, 0)))
            as_ = _attn_sample(q_rows, k_rows, v_rows, ck, cv, page_table, j, lam, subln,
                               lam_init, n_new, pp)
            w_out = w_out_a[j]
            kp_rows.append(kp.reshape(bp, t_all, 2 * h_a, HEAD_A))
            vp_rows.append(vp.reshape(bp, t_all, h_a, 2 * HEAD_A))
            ks_rows.append(ks4[:, :n_new])
            vs_rows.append(vs4[:, :n_new])
        elif kind == 1:
            w = w_in_b[j]
            n_main_cols = 2 * kd_b + 2 * d
            w_main = w[:, :n_main_cols].astype(BF16)
            w_gl = jnp.pad(w[:, n_main_cols:], ((0, 0), (0, LANES - GATE_RANK))).astype(BF16)
            w_g2 = jnp.pad(w_gate2_b[j], ((0, LANES - GATE_RANK), (0, 0))).astype(BF16)
            bg = b_gate_b[j].reshape(1, kd_b)
            gn = gnorm_b[j].reshape(1, dv_b)
            qp, kp, vp, rp, gp = _proj_b(hp, nw, w_main, w_gl, w_g2, bg, tm_main)
            qs, ks, vs, rs, gs = _proj_b(hs, nw, w_main, w_gl, w_g2, bg, n_small)
            r3 = lambda a: a.reshape(bp, t_all, a.shape[1])
            ap, sp = _recur_prompt(r3(qp), r3(kp), r3(gp), r3(vp), r3(rp), gn, N_META_TOKENS,
                                   MAIN_CHUNK, 2)
            ap = ap.reshape(n_main, d)
            as_, ss = _recur_sample(qs, ks, gs, vs, rs, state_gla[j].astype(F32), gn, n_new)
            gla_p.append(sp)
            gla_s.append(ss)
            w_out = w_out_b[j]
        else:
            w = w_in_c[j].astype(BF16)
            gn = gnorm_c[j].reshape(1, dv_c)
            lbf = lb_c.astype(F32)
            qp, kp, gp, vp, rp = _proj_c(hp, nw, w, lbf, i, tm_main)
            qs, ks, gs, vs, rs = _proj_c(hs, nw, w, lbf, i, n_small)
            r3 = lambda a: a.reshape(bp, t_all, a.shape[1])
            ap, sp = _recur_prompt(r3(qp), r3(kp), r3(gp), r3(vp), r3(rp), gn, N_META_TOKENS,
                                   MAIN_CHUNK, 4)
            ap = ap.reshape(n_main, d)
            as_, ss = _recur_sample(qs, ks, gs, vs, rs, state_hgrn[j].astype(F32), gn, n_new)
            hgrn_p.append(sp)
            hgrn_s.append(ss)
            w_out = w_out_c[j]
        w_out = w_out.astype(BF16)
        hp = _out_proj(ap, w_out, hp, tm_main)
        hs = _out_proj(as_, w_out, hs, n_small)
        nf = norm_ffn[i].reshape(1, d)
        wu, wd = w_up[i].astype(BF16), w_down[i].astype(BF16)
        hp = _mlp(hp, nf, wu, wd, tm_mlp, tf)
        hs = _mlp(hs, nf, wu, wd, n_small, tf)

    nfin = norm_final.reshape(1, d)
    y_prompt = _final_norm_prompt(hp.reshape(bp, t_all, d), nfin, N_META_TOKENS)
    y_sample = _final_norm(hs, nfin).reshape(db, SAMPLE_ROWS, d)[:, :n_new]
    return (y_prompt, y_sample,
            jnp.stack(kp_rows), jnp.stack(vp_rows), jnp.stack(ks_rows), jnp.stack(vs_rows),
            jnp.stack(gla_p), jnp.stack(gla_s), jnp.stack(hgrn_p), jnp.stack(hgrn_s))
```

```python
import functools
import math

import numpy as np
import jax
import jax.numpy as jnp
from jax import lax
from jax.experimental import pallas as pl
from jax.experimental.pallas import tpu as pltpu

F32 = jnp.float32
BF16 = jnp.bfloat16

RMS_EPS = 1e-6
N_META_TOKENS = 16
N_MIXERS = 3
HEAD_A = 64
ROT_DIM = HEAD_A // 4
ROPE_THETA = 500000.0
H_B = 4
GATE_RANK = 16
GATE_TEMP = 16.0
DK = 128
SAMPLE_ROWS = 8
SMALL_CHUNK = 16
MAIN_CHUNK = 64
LANES = 128
ONES_ROWS = 16
VMEM_LIMIT = 56 * 1024 * 1024
LOG2E = 1.4426950408889634

NT_DIMS = (((1,), (1,)), ((), ()))
TN_DIMS = (((0,), (0,)), ((), ()))


def _cparams(*sem):
    return pltpu.CompilerParams(dimension_semantics=sem, vmem_limit_bytes=VMEM_LIMIT)


def _pick_tile(n, target, mult=16):
    best = None
    for t in range(mult, min(n, target) + 1, mult):
        if n % t == 0:
            best = t
    return n if best is None else best


def _rms(x, w):
    ms = jnp.mean(x * x, axis=-1, keepdims=True)
    return x * lax.rsqrt(ms + RMS_EPS) * w


def _sigmoid(x):
    return 1.0 / (1.0 + jnp.exp(-x))


def _proj_a_kernel(x_ref, nw_ref, w_ref, c_ref, sm_ref, sp_ref, *rest, cw):
    q_ref, k_ref, v_ref = rest[-3:]
    xn = _rms(x_ref[...], nw_ref[...]).astype(BF16)
    d = q_ref.shape[1]
    rep = cw // LANES
    cos = jnp.tile(c_ref[...], (1, rep))
    sin_m = jnp.tile(sm_ref[...], (1, rep))
    sin_p = jnp.tile(sp_ref[...], (1, rep))
    for dst_i, dst in enumerate((q_ref, k_ref, v_ref)):
        for c in range(d // cw):
            col = dst_i * d + c * cw
            y = jnp.dot(xn, w_ref[:, col:col + cw], preferred_element_type=F32)
            if dst_i < 2:
                y = (y * cos + pltpu.roll(y, ROT_DIM // 2, 1) * sin_p
                     + pltpu.roll(y, cw - ROT_DIM // 2, 1) * sin_m)
            dst[:, c * cw:(c + 1) * cw] = y


def _proj_b_kernel(x_ref, nw_ref, w_ref, wgl_ref, wg2_ref, bg_ref,
                   q_ref, k_ref, v_ref, r_ref, g_ref, *, cw):
    xn = _rms(x_ref[...], nw_ref[...]).astype(BF16)
    col = 0
    for dst, scale in ((q_ref, DK ** -0.5), (k_ref, None), (v_ref, None), (r_ref, None)):
        for c in range(dst.shape[1] // cw):
            y = jnp.dot(xn, w_ref[:, col:col + cw], preferred_element_type=F32)
            if scale is not None:
                y = y * scale
            dst[:, c * cw:(c + 1) * cw] = y.astype(dst.dtype)
            col += cw
    gl = jnp.dot(xn, wgl_ref[...], preferred_element_type=F32).astype(BF16)
    z = jnp.dot(gl, wg2_ref[...], preferred_element_type=F32) + bg_ref[...]
    g_ref[...] = (jnp.minimum(z, 0.0) - jnp.log(1.0 + jnp.exp(-jnp.abs(z)))) * (1.0 / GATE_TEMP)


def _proj_c_kernel(x_ref, nw_ref, w_ref, lb_ref, q_ref, k_ref, g_ref, v_ref, r_ref, *, cw, layer):
    xn = _rms(x_ref[...], nw_ref[...]).astype(BF16)
    d = q_ref.shape[1]
    lbs = lb_ref[...]
    mx = jnp.max(lbs, axis=0, keepdims=True)
    e = jnp.exp(lbs - mx)
    lb = jnp.sum(e[1:layer + 1], axis=0, keepdims=True) / jnp.sum(e, axis=0, keepdims=True)
    for c in range(d // cw):
        sl = slice(c * cw, (c + 1) * cw)
        y = jnp.dot(xn, w_ref[:, c * cw:(c + 1) * cw], preferred_element_type=F32)
        q_ref[:, sl] = y * _sigmoid(y) * (DK ** -0.5)
        f = jnp.dot(xn, w_ref[:, d + c * cw:d + (c + 1) * cw], preferred_element_type=F32)
        lbc = lb[:, sl]
        forget = lbc + (1.0 - lbc) * _sigmoid(f)
        k_ref[:, sl] = 1.0 - forget
        g_ref[:, sl] = jnp.log(forget)
        v_ref[:, sl] = jnp.dot(xn, w_ref[:, 2 * d + c * cw:2 * d + (c + 1) * cw],
                               preferred_element_type=F32).astype(v_ref.dtype)
        r_ref[:, sl] = jnp.dot(xn, w_ref[:, 3 * d + c * cw:3 * d + (c + 1) * cw],
                               preferred_element_type=F32)


def _row_spec(tm, width):
    return pl.BlockSpec((tm, width), lambda i: (i, 0))


def _full_spec(shape):
    return pl.BlockSpec(shape, lambda i: (0,) * len(shape))


def _proj_a(x, nw, w, tables, tm, table_blocks, n_layers, layer_j, kv_prev):
    n, d = x.shape
    cw = min(512, d)
    tab_spec = pl.BlockSpec((tm, LANES), lambda i: (i % table_blocks, 0))
    kv_shape = jax.ShapeDtypeStruct((n_layers, n, d), F32)
    kv_spec = pl.BlockSpec((None, tm, d), lambda i: (layer_j, i, 0))
    in_specs = [_row_spec(tm, d), _full_spec((1, d)), _full_spec(w.shape),
                tab_spec, tab_spec, tab_spec]
    args = [x, nw, w, *tables]
    aliases = {}
    if kv_prev is not None:
        aliases = {len(args): 1, len(args) + 1: 2}
        in_specs += [pl.BlockSpec(memory_space=pl.ANY)] * 2
        args += list(kv_prev)
    return pl.pallas_call(
        functools.partial(_proj_a_kernel, cw=cw),
        out_shape=(jax.ShapeDtypeStruct((n, d), F32), kv_shape, kv_shape),
        grid=(n // tm,),
        in_specs=in_specs,
        out_specs=(_row_spec(tm, d), kv_spec, kv_spec),
        input_output_aliases=aliases,
        compiler_params=_cparams("parallel"),
        name="proj_a",
    )(*args)


def _proj_b(x, nw, w, wgl, wg2, bg, tm):
    n, d = x.shape
    kd, vd = wg2.shape[1], d
    cw = min(512, kd)
    return pl.pallas_call(
        functools.partial(_proj_b_kernel, cw=cw),
        out_shape=(jax.ShapeDtypeStruct((n, kd), F32), jax.ShapeDtypeStruct((n, kd), F32),
                   jax.ShapeDtypeStruct((n, vd), BF16), jax.ShapeDtypeStruct((n, vd), F32),
                   jax.ShapeDtypeStruct((n, kd), F32)),
        grid=(n // tm,),
        in_specs=[_row_spec(tm, d), _full_spec((1, d)), _full_spec(w.shape),
                  _full_spec(wgl.shape), _full_spec(wg2.shape), _full_spec((1, kd))],
        out_specs=(_row_spec(tm, kd), _row_spec(tm, kd), _row_spec(tm, vd), _row_spec(tm, vd),
                   _row_spec(tm, kd)),
        compiler_params=_cparams("parallel"),
        name="proj_b",
    )(x, nw, w, wgl, wg2, bg)


def _proj_c(x, nw, w, lb_c, layer, tm):
    n, d = x.shape
    cw = min(512, d)
    out = jax.ShapeDtypeStruct((n, d), F32)
    return pl.pallas_call(
        functools.partial(_proj_c_kernel, cw=cw, layer=layer),
        out_shape=(out, out, out, jax.ShapeDtypeStruct((n, d), BF16), out),
        grid=(n // tm,),
        in_specs=[_row_spec(tm, d), _full_spec((1, d)), _full_spec(w.shape),
                  _full_spec(lb_c.shape)],
        out_specs=(_row_spec(tm, d),) * 5,
        compiler_params=_cparams("parallel"),
        name="proj_c",
    )(x, nw, w, lb_c)


def _lambda_full(lam_ref, lam_init):
    l = lam_ref[...]
    a = jnp.sum(l[0:1] * l[1:2], axis=-1, keepdims=True)
    b = jnp.sum(l[2:3] * l[3:4], axis=-1, keepdims=True)
    return jnp.exp(a) - jnp.exp(b) + lam_init


def _head_finish(o, subln, lam_init):
    ms = jnp.mean(o * o, axis=-1, keepdims=True)
    return o * lax.rsqrt(ms + RMS_EPS) * subln * (1.0 - lam_init)


def _attn_prompt_kernel(lam_ref, subln_ref, q_ref, k_ref, v_ref, o_ref,
                        kb_ref, vt_ref, vt2_ref, km_ref, vtm_ref, m_ref, acc_ref, bias_ref,
                        *, tq, n_meta, lam_init, hb):
    t_all = q_ref.shape[0]
    nq = (t_all - n_meta) // tq
    hw = 2 * HEAD_A
    lam = _lambda_full(lam_ref, lam_init)
    subln = subln_ref[...]
    first_map = lax.broadcasted_iota(jnp.int32, (1, hw), 1) < HEAD_A

    def stack_q(qt, scale):
        qt = qt * scale
        return jnp.concatenate([jnp.where(first_map, qt, 0.0), jnp.where(first_map, 0.0, qt)],
                               axis=0).astype(BF16)

    r = lax.broadcasted_iota(jnp.int32, (2 * n_meta, n_meta), 0)
    c = lax.broadcasted_iota(jnp.int32, (2 * n_meta, n_meta), 1)
    meta_causal = c <= jnp.where(r >= n_meta, r - n_meta, r)
    pad = jnp.zeros((hw - n_meta, hw), F32)
    for h in range(hb):
        lanes = slice(h * hw, (h + 1) * hw)
        kmeta = k_ref[0:n_meta, lanes]
        vmeta = v_ref[0:n_meta, lanes]
        qs = stack_q(q_ref[0:n_meta, lanes], HEAD_A ** -0.5)
        s = lax.dot_general(qs, kmeta.astype(BF16), NT_DIMS, preferred_element_type=F32)
        s = jnp.where(meta_causal, s, -jnp.inf)
        p = jnp.exp(s - jnp.max(s, axis=-1, keepdims=True))
        acc = jnp.dot(p.astype(BF16), vmeta.astype(BF16), preferred_element_type=F32)
        acc = acc / jnp.sum(p, axis=-1, keepdims=True)
        o = acc[:n_meta] - lam * acc[n_meta:]
        o_ref[0:n_meta, lanes] = _head_finish(o, subln, lam_init).astype(o_ref.dtype)
        km_ref[h] = jnp.concatenate([kmeta, pad], axis=0).astype(BF16)
        vtm_ref[h] = jnp.concatenate(
            [jnp.concatenate([vmeta, pad], axis=0).T.astype(BF16), jnp.ones((ONES_ROWS, hw), BF16)],
            axis=0)
        for cidx in range(nq):
            rows = slice(n_meta + cidx * tq, n_meta + (cidx + 1) * tq)
            kb_ref[h, cidx] = k_ref[rows, lanes].astype(BF16)
            vte = jnp.concatenate([v_ref[rows, lanes].T.astype(BF16),
                                   jnp.ones((ONES_ROWS, tq), BF16)], axis=0)
            vt_ref[h, cidx] = vte
            if cidx < 2 * (nq // 2):
                vt2_ref[h, cidx // 2, :, (cidx % 2) * tq:(cidx % 2 + 1) * tq] = vte

    key_i = lax.broadcasted_iota(jnp.int32, (hw + tq, 2 * tq), 0)
    qry_i = lax.broadcasted_iota(jnp.int32, (hw + tq, 2 * tq), 1)
    key_limit = jnp.where(key_i < hw, n_meta - 1, hw + jnp.where(qry_i >= tq, qry_i - tq, qry_i))
    bias_ref[...] = jnp.where(key_i <= key_limit, 0.0, -jnp.inf)

    def online(h, s, vte):
        m_old = m_ref[h]
        m_new = jnp.maximum(m_old, jnp.max(s, axis=0, keepdims=True))
        alpha = jnp.exp2(m_old - m_new)
        p = jnp.exp2(s - m_new).astype(BF16)
        acc_ref[h] = alpha * acc_ref[h] + jnp.dot(vte, p, preferred_element_type=F32)
        m_ref[h] = m_new

    def q_body(qi, _):
        q0 = pl.multiple_of(n_meta + qi * tq, 16)
        qss = [stack_q(q_ref[pl.ds(q0, tq), h * hw:(h + 1) * hw], (HEAD_A ** -0.5) * LOG2E)
               for h in range(hb)]
        for h in range(hb):
            s = jnp.concatenate(
                [lax.dot_general(km_ref[h], qss[h], NT_DIMS, preferred_element_type=F32),
                 lax.dot_general(kb_ref[h, qi], qss[h], NT_DIMS, preferred_element_type=F32)],
                axis=0)
            s = s + bias_ref[...]
            m = jnp.max(s, axis=0, keepdims=True)
            p = jnp.exp2(s - m).astype(BF16)
            m_ref[h] = m
            acc_ref[h] = (jnp.dot(vtm_ref[h], p[:hw], preferred_element_type=F32)
                          + jnp.dot(vt_ref[h, qi], p[hw:], preferred_element_type=F32))

        def pair_body(j, _):
            for h in range(hb):
                kpair = kb_ref[h, pl.ds(2 * j, 2)].reshape(2 * tq, hw)
                s = lax.dot_general(kpair, qss[h], NT_DIMS, preferred_element_type=F32)
                online(h, s, vt2_ref[h, j])
            return 0

        lax.fori_loop(0, qi // 2, pair_body, 0)

        @pl.when(qi % 2 == 1)
        def _():
            for h in range(hb):
                s = lax.dot_general(kb_ref[h, qi - 1], qss[h], NT_DIMS,
                                    preferred_element_type=F32)
                online(h, s, vt_ref[h, qi - 1])

        for h in range(hb):
            acc = acc_ref[h]
            ot = acc[:hw] * (1.0 / acc[hw:hw + 1])
            od = ot[:, :tq] - lam * ot[:, tq:]
            ms = jnp.mean(od * od, axis=0, keepdims=True)
            od = od * lax.rsqrt(ms + RMS_EPS)
            o_ref[pl.ds(q0, tq), h * hw:(h + 1) * hw] = (
                od.T * (subln * (1.0 - lam_init))).astype(o_ref.dtype)
        return 0

    lax.fori_loop(0, nq, q_body, 0)


def _attn_prompt(q, k, v, layer_j, lam, subln, lam_init, n_meta, hb):
    b, t, d = q.shape
    hw = 2 * HEAD_A
    tq = _pick_tile(t - n_meta, 256, LANES)
    nq = (t - n_meta) // tq
    blk = pl.BlockSpec((None, t, hb * hw), lambda i, h: (i, 0, h))
    kvblk = pl.BlockSpec((None, None, t, hb * hw), lambda i, h: (layer_j, i, 0, h))
    return pl.pallas_call(
        functools.partial(_attn_prompt_kernel, tq=tq, n_meta=n_meta, lam_init=lam_init, hb=hb),
        out_shape=jax.ShapeDtypeStruct((b, t, d), BF16),
        grid=(b, d // (hb * hw)),
        in_specs=[pl.BlockSpec(lam.shape, lambda i, h: (0, 0)),
                  pl.BlockSpec((1, hw), lambda i, h: (0, 0)), blk, kvblk, kvblk],
        out_specs=blk,
        scratch_shapes=[pltpu.VMEM((hb, nq, tq, hw), BF16),
                        pltpu.VMEM((hb, nq, hw + ONES_ROWS, tq), BF16),
                        pltpu.VMEM((hb, max(nq // 2, 1), hw + ONES_ROWS, 2 * tq), BF16),
                        pltpu.VMEM((hb, hw, hw), BF16), pltpu.VMEM((hb, hw + ONES_ROWS, hw), BF16),
                        pltpu.VMEM((hb, 1, 2 * tq), F32),
                        pltpu.VMEM((hb, hw + ONES_ROWS, 2 * tq), F32),
                        pltpu.VMEM((hw + tq, 2 * tq), F32)],
        compiler_params=_cparams("parallel", "parallel"),
        name="attn_prompt",
    )(lam, subln, q, k, v)


def _attn_sample_kernel(pt_ref, lam_ref, subln_ref, q_ref, kn_ref, vn_ref, *rest,
                        pp, n_new, lam_init):
    del pt_ref
    k_refs, v_refs = rest[:pp], rest[pp:2 * pp]
    o_ref, qbd_ref, m_ref, l_ref, acc_ref = rest[2 * pp:]
    step = pl.program_id(1)
    rows = q_ref.shape[0]
    n_heads = rows // n_new
    n_vheads = n_heads // 2
    page = k_refs[0].shape[1]
    grp = 4 * n_new

    def scores(k_list):
        qbd = qbd_ref[...]
        return jnp.concatenate(
            [jnp.dot(qbd, kr[...].astype(BF16), preferred_element_type=F32) for kr in k_list],
            axis=1)

    def values(p, v_list):
        out = []
        for g in range(rows // grp):
            pg = p[g * grp:(g + 1) * grp]
            halves = []
            for vh in (2 * g, 2 * g + 1):
                r = None
                for i, vr in enumerate(v_list):
                    v = vr[pl.ds(vh, page, stride=n_vheads), :].astype(BF16)
                    t = jnp.dot(pg[:, i * page:(i + 1) * page], v, preferred_element_type=F32)
                    r = t if r is None else r + t
                halves.append(r)
            out.append(halves[0][:grp // 2])
            out.append(halves[1][grp // 2:])
        return jnp.concatenate(out, axis=0)

    @pl.when(step == 0)
    def _():
        d = n_heads * HEAD_A
        q = jnp.tile(q_ref[...] * ((HEAD_A ** -0.5) * LOG2E), (1, n_heads))
        row_head = lax.broadcasted_iota(jnp.int32, (rows, d), 0) // n_new
        lane_head = lax.broadcasted_iota(jnp.int32, (rows, d), 1) // HEAD_A
        qbd_ref[...] = jnp.where(row_head == lane_head, q, 0.0).astype(BF16)
        s = scores([kn_ref])
        tok = lax.broadcasted_iota(jnp.int32, (rows, page), 0) % n_new
        key = lax.broadcasted_iota(jnp.int32, (rows, page), 1)
        s = jnp.where(key <= tok, s, -jnp.inf)
        m = jnp.max(s, axis=-1, keepdims=True)
        p = jnp.exp2(s - m)
        m_ref[...] = m
        l_ref[...] = jnp.sum(p, axis=-1, keepdims=True)
        acc_ref[...] = values(p.astype(BF16), [vn_ref])

    s = scores(k_refs)
    m_old = m_ref[...]
    m_new = jnp.maximum(m_old, jnp.max(s, axis=-1, keepdims=True))
    alpha = jnp.exp2(m_old - m_new)
    p = jnp.exp2(s - m_new)
    l_ref[...] = alpha * l_ref[...] + jnp.sum(p, axis=-1, keepdims=True)
    acc_ref[...] = alpha * acc_ref[...] + values(p.astype(BF16), v_refs)
    m_ref[...] = m_new

    @pl.when(step == pl.num_programs(1) - 1)
    def _():
        lam = _lambda_full(lam_ref, lam_init)
        subln = subln_ref[...]
        hw = 2 * HEAD_A
        full = acc_ref[...] / l_ref[...]
        diff = full - lam * pltpu.roll(full, rows - n_new, 0)
        for h in range(n_heads // 2):
            tile = diff[2 * n_new * h:2 * n_new * (h + 1)]
            o_ref[:, h * hw:(h + 1) * hw] = _head_finish(tile, subln, lam_init).astype(o_ref.dtype)


def _attn_sample(q, k_new, v_new, cache_k, cache_v, page_table, layer_j, lam, subln, lam_init,
                 n_new, pp):
    nb, rows, _ = q.shape
    n_pages = page_table.shape[1]
    krows, page = cache_k.shape[2:]
    vrows = cache_v.shape[2]
    hw = 2 * HEAD_A
    d = (rows // n_new) * HEAD_A
    assert 2 * n_new == SAMPLE_ROWS and krows == d and vrows * 2 * HEAD_A == page * d

    def per_sample(shape):
        return pl.BlockSpec((None,) + shape, lambda b, s, pt: (b, 0, 0))

    def kpage(i):
        return pl.BlockSpec((None, None, krows, page),
                            lambda b, s, pt: (layer_j, pt[b, s * pp + i], 0, 0))

    def vpage(i):
        return pl.BlockSpec((None, None, vrows, hw),
                            lambda b, s, pt: (layer_j, pt[b, s * pp + i], 0, 0))

    grid_spec = pltpu.PrefetchScalarGridSpec(
        num_scalar_prefetch=1,
        grid=(nb, n_pages // pp),
        in_specs=[pl.BlockSpec(lam.shape, lambda b, s, pt: (0, 0)),
                  pl.BlockSpec((1, hw), lambda b, s, pt: (0, 0)),
                  per_sample(q.shape[1:]), per_sample(k_new.shape[1:]),
                  per_sample(v_new.shape[1:])]
                 + [kpage(i) for i in range(pp)] + [vpage(i) for i in range(pp)],
        out_specs=pl.BlockSpec((SAMPLE_ROWS, d), lambda b, s, pt: (b, 0)),
        scratch_shapes=[pltpu.VMEM((rows, d), BF16), pltpu.VMEM((rows, 1), F32),
                        pltpu.VMEM((rows, 1), F32), pltpu.VMEM((rows, hw), F32)],
    )
    return pl.pallas_call(
        functools.partial(_attn_sample_kernel, pp=pp, n_new=n_new, lam_init=lam_init),
        out_shape=jax.ShapeDtypeStruct((nb * SAMPLE_ROWS, d), BF16),
        grid_spec=grid_spec,
        compiler_params=_cparams("parallel", "arbitrary"),
        name="attn_sample",
    )(page_table, lam, subln, q, k_new, v_new, *([cache_k] * pp), *([cache_v] * pp))


def _chunk_constants(length):
    nlev = int(math.log2(length))
    assert 2 ** nlev == length
    w = np.zeros(((2 + nlev) * length, length), np.float32)
    masks = np.zeros((nlev + 1, length, length), np.float32)
    masks[0] = np.eye(length)
    for t in range(length):
        w[t, :t + 1] = 1.0
        w[length + t, t + 1:] = 1.0
    for lvl in range(1, nlev + 1):
        bs, half = 2 ** lvl, 2 ** (lvl - 1)
        for t in range(length):
            mid = t - t % bs + half
            row = (1 + lvl) * length + t
            if t >= mid:
                w[row, mid:t + 1] = 1.0
                masks[lvl, t, mid - half:mid] = 1.0
            else:
                w[row, t + 1:mid] = 1.0
    return jnp.asarray(np.tile(w, (1, 3)), BF16), jnp.asarray(masks, F32)


def _chunk_exponents(g, w3):
    g = g * LOG2E
    g1 = g.astype(BF16)
    rem = g - g1.astype(F32)
    g2 = rem.astype(BF16)
    g3 = (rem - g2.astype(F32)).astype(BF16)
    return jnp.exp2(jnp.dot(w3, jnp.concatenate([g1, g2, g3], axis=0),
                            preferred_element_type=F32))


def _chunk_head(q, k, vb, st, ex, masks):
    length = q.shape[0]
    nlev = masks.shape[0] - 1
    e_cum = ex[0:length]
    e_rem = ex[length:2 * length]
    e_last = e_cum[length - 1:length]
    o = lax.dot_general((q * e_cum).astype(BF16), st.astype(BF16), NT_DIMS,
                        preferred_element_type=F32)
    qb, kb = q.astype(BF16), k.astype(BF16)
    a = masks[0] * lax.dot_general(qb, kb, NT_DIMS, preferred_element_type=F32)
    row = lax.broadcasted_iota(jnp.int32, q.shape, 0)
    for lvl in range(1, nlev + 1):
        second_half = (row & (2 ** lvl - 1)) >= 2 ** (lvl - 1)
        x = (jnp.where(second_half, q, k) * ex[(1 + lvl) * length:(2 + lvl) * length]).astype(BF16)
        a = a + masks[lvl] * lax.dot_general(x, x, NT_DIMS, preferred_element_type=F32)
    o = o + jnp.dot(a.astype(BF16), vb, preferred_element_type=F32)
    kd = (k * e_rem).astype(BF16)
    st = st * e_last + lax.dot_general(vb, kd, TN_DIMS, preferred_element_type=F32)
    return o, st


def _gate_norm(o, r, gn):
    ms = jnp.mean(o * o, axis=-1, keepdims=True)
    return o * lax.rsqrt(ms + RMS_EPS) * gn * (r * _sigmoid(r))


def _recur_prompt_kernel(gn_ref, ws_ref, ms_ref, wm_ref, mm_ref, q_ref, k_ref, g_ref, v_ref, r_ref,
                         o_ref, s_ref, st_ref, *, n_meta, chunk, hb):
    t_all = q_ref.shape[0]
    dv = v_ref.shape[1] // hb
    gn = gn_ref[...]
    st_ref[...] = jnp.zeros_like(st_ref)

    def run(r0, length, w3, masks):
        rows = pl.ds(r0, length)
        ex = _chunk_exponents(g_ref[rows], w3)
        for h in range(hb):
            ksl = slice(h * DK, (h + 1) * DK)
            vsl = slice(h * dv, (h + 1) * dv)
            o, st = _chunk_head(q_ref[rows, ksl], k_ref[rows, ksl], v_ref[rows, vsl], st_ref[h],
                                ex[:, ksl], masks)
            st_ref[h] = st
            o_ref[rows, vsl] = _gate_norm(o, r_ref[rows, vsl], gn).astype(o_ref.dtype)

    run(0, n_meta, ws_ref[...], ms_ref[...])

    def body(c, _):
        run(pl.multiple_of(n_meta + c * chunk, 16), chunk, wm_ref[...], mm_ref[...])
        return 0

    lax.fori_loop(0, (t_all - n_meta) // chunk, body, 0, unroll=2)
    for h in range(hb):
        s_ref[h] = st_ref[h].T


def _recur_prompt(q, k, g, v, r, gn, n_meta, chunk, hb):
    b, t, kd = q.shape
    vd = v.shape[2]
    nh = kd // DK
    dv = vd // nh
    ws, ms = _chunk_constants(n_meta)
    wm, mm = _chunk_constants(chunk)
    kblk = pl.BlockSpec((None, t, hb * DK), lambda i, h: (i, 0, h))
    vblk = pl.BlockSpec((None, t, hb * dv), lambda i, h: (i, 0, h))

    def const(x):
        return pl.BlockSpec(x.shape, lambda i, h: (0,) * x.ndim)

    return pl.pallas_call(
        functools.partial(_recur_prompt_kernel, n_meta=n_meta, chunk=chunk, hb=hb),
        out_shape=(jax.ShapeDtypeStruct((b, t, vd), BF16),
                   jax.ShapeDtypeStruct((b, nh, DK, dv), F32)),
        grid=(b, nh // hb),
        in_specs=[pl.BlockSpec((1, dv), lambda i, h: (0, 0)), const(ws), const(ms), const(wm),
                  const(mm), kblk, kblk, kblk, vblk, vblk],
        out_specs=(vblk, pl.BlockSpec((None, hb, DK, dv), lambda i, h: (i, h, 0, 0))),
        scratch_shapes=[pltpu.VMEM((hb, dv, DK), F32)],
        compiler_params=_cparams("parallel", "parallel"),
        name="recur_prompt",
    )(gn, ws, ms, wm, mm, q, k, g, v, r)


def _recur_sample_kernel(gn_ref, w_ref, m_ref, q_ref, k_ref, g_ref, v_ref, r_ref, s0_ref,
                         o_ref, s_ref, *, n_new):
    nh = s0_ref.shape[0]
    dv = s0_ref.shape[2]
    gn = gn_ref[...]
    masks = m_ref[...]
    pad_rows = SMALL_CHUNK - SAMPLE_ROWS
    kd = q_ref.shape[1]
    valid = lax.broadcasted_iota(jnp.int32, (SAMPLE_ROWS, kd), 0) < n_new

    def padded(x):
        return jnp.concatenate([x, jnp.zeros((pad_rows, x.shape[1]), x.dtype)], axis=0)

    q = padded(q_ref[...])
    k = padded(jnp.where(valid, k_ref[...], 0.0))
    ex = _chunk_exponents(padded(jnp.where(valid, g_ref[...], 0.0)), w_ref[...])
    vb = padded(v_ref[...].astype(F32)).astype(BF16)
    for h in range(nh):
        ksl = slice(h * DK, (h + 1) * DK)
        vsl = slice(h * dv, (h + 1) * dv)
        o, st = _chunk_head(q[:, ksl], k[:, ksl], vb[:, vsl], s0_ref[h].T, ex[:, ksl], masks)
        s_ref[h] = st.T
        o_ref[:, vsl] = _gate_norm(o[0:SAMPLE_ROWS], r_ref[:, vsl], gn).astype(o_ref.dtype)


def _recur_sample(q, k, g, v, r, s0, gn, n_new):
    n, kd = q.shape
    vd = v.shape[1]
    nb, nh, _, dv = s0.shape
    w, masks = _chunk_constants(SMALL_CHUNK)
    kblk = pl.BlockSpec((SAMPLE_ROWS, kd), lambda i: (i, 0))
    vblk = pl.BlockSpec((SAMPLE_ROWS, vd), lambda i: (i, 0))
    sblk = pl.BlockSpec((None, nh, DK, dv), lambda i: (i, 0, 0, 0))
    return pl.pallas_call(
        functools.partial(_recur_sample_kernel, n_new=n_new),
        out_shape=(jax.ShapeDtypeStruct((n, vd), BF16), jax.ShapeDtypeStruct(s0.shape, F32)),
        grid=(nb,),
        in_specs=[_full_spec((1, dv)), _full_spec(w.shape), _full_spec(masks.shape),
                  kblk, kblk, kblk, vblk, vblk, sblk],
        out_specs=(vblk, sblk),
        compiler_params=_cparams("parallel"),
        name="recur_sample",
    )(gn, w, masks, q, k, g, v, r, s0)


def _mix_mlp_kernel(a_ref, wo_ref, x_ref, nw_ref, wu_ref, wd_ref, o_ref, xn_ref):
    j = pl.program_id(1)

    @pl.when(j == 0)
    def _():
        h = x_ref[...] + jnp.dot(a_ref[...], wo_ref[...], preferred_element_type=F32)
        xn_ref[...] = _rms(h, nw_ref[...]).astype(BF16)
        o_ref[...] = h

    u = jnp.dot(xn_ref[...], wu_ref[...], preferred_element_type=F32)
    u = jnp.square(jnp.maximum(u, 0.0)).astype(BF16)
    o_ref[...] += jnp.dot(u, wd_ref[...], preferred_element_type=F32)


def _mix_mlp(a, wo, x, nw, wu, wd, tm, tf):
    n, d = x.shape
    ff = wu.shape[1]
    return pl.pallas_call(
        _mix_mlp_kernel,
        out_shape=jax.ShapeDtypeStruct((n, d), F32),
        grid=(n // tm, ff // tf),
        in_specs=[pl.BlockSpec((tm, a.shape[1]), lambda i, j: (i, 0)),
                  pl.BlockSpec(wo.shape, lambda i, j: (0, 0)),
                  pl.BlockSpec((tm, d), lambda i, j: (i, 0)),
                  pl.BlockSpec((1, d), lambda i, j: (0, 0)),
                  pl.BlockSpec((d, tf), lambda i, j: (0, j)),
                  pl.BlockSpec((tf, d), lambda i, j: (j, 0))],
        out_specs=pl.BlockSpec((tm, d), lambda i, j: (i, 0)),
        scratch_shapes=[pltpu.VMEM((tm, d), BF16)],
        compiler_params=_cparams("parallel", "arbitrary"),
        name="mix_mlp",
    )(a, wo, x, nw, wu, wd)


def _final_norm_prompt_kernel(x_ref, nw_ref, o_ref, *, n_meta):
    tm = o_ref.shape[0]
    r0 = pl.multiple_of(n_meta + pl.program_id(1) * tm, 8)
    o_ref[...] = _rms(x_ref[pl.ds(r0, tm)], nw_ref[...])


def _final_norm_prompt(x, nw, n_meta):
    b, t, d = x.shape
    seq = t - n_meta
    tm = _pick_tile(seq, 512)
    return pl.pallas_call(
        functools.partial(_final_norm_prompt_kernel, n_meta=n_meta),
        out_shape=jax.ShapeDtypeStruct((b, seq, d), F32),
        grid=(b, seq // tm),
        in_specs=[pl.BlockSpec((None, t, d), lambda i, j: (i, 0, 0)),
                  pl.BlockSpec((1, d), lambda i, j: (0, 0))],
        out_specs=pl.BlockSpec((None, tm, d), lambda i, j: (i, j, 0)),
        compiler_params=_cparams("parallel", "arbitrary"),
        name="final_norm_prompt",
    )(x, nw)


def _final_norm_kernel(x_ref, nw_ref, o_ref):
    o_ref[...] = _rms(x_ref[...], nw_ref[...])


def _final_norm(x, nw):
    return pl.pallas_call(
        _final_norm_kernel,
        out_shape=jax.ShapeDtypeStruct(x.shape, F32),
        name="final_norm",
    )(x, nw)


def _rope_tables(pos):
    half = ROT_DIM // 2
    inv_freq = ROPE_THETA ** (-jnp.arange(half, dtype=F32) / half)
    ang = pos.astype(F32)[:, None] * inv_freq[None, :]
    cos, sin = jnp.cos(ang), jnp.sin(ang)
    n = pos.shape[0]
    zeros = jnp.zeros((n, half), F32)
    rest0 = jnp.zeros((n, HEAD_A - ROT_DIM), F32)
    cos_h = jnp.concatenate([cos, cos, jnp.ones((n, HEAD_A - ROT_DIM), F32)], axis=1)
    sin_m = jnp.concatenate([-sin, zeros, rest0], axis=1)
    sin_p = jnp.concatenate([zeros, sin, rest0], axis=1)
    return tuple(jnp.concatenate([t, t], axis=1) for t in (cos_h, sin_m, sin_p))


def kernel(x_prompt, x_sample, cache_k, cache_v, state_gla, state_hgrn, page_table, meta_tokens,
           norm_mix, norm_ffn, norm_final, w_in_a, lam_a, subln_a, w_out_a, w_in_b, w_gate2_b,
           b_gate_b, gnorm_b, w_out_b, w_in_c, lb_c, gnorm_c, w_out_c, w_up, w_down):
    bp, seq, d = x_prompt.shape
    db, n_new = x_sample.shape[:2]
    depth = norm_mix.shape[0]
    n_pages = page_table.shape[1]
    page = cache_k.shape[2]
    past_len = n_pages * page
    t_all = N_META_TOKENS + seq
    n_main = bp * t_all
    n_small = db * SAMPLE_ROWS
    h_a = d // (2 * HEAD_A)
    kd_b = w_gate2_b.shape[2]
    dv_b = d // H_B
    h_c = d // DK
    dv_c = d // h_c

    hp = jnp.concatenate(
        [jnp.broadcast_to(meta_tokens.astype(F32)[None], (bp, N_META_TOKENS, d)), x_prompt],
        axis=1).reshape(n_main, d)
    hs = jnp.pad(x_sample, ((0, 0), (0, SAMPLE_ROWS - n_new), (0, 0))).reshape(n_small, d)

    tm_main = _pick_tile(t_all, 704)
    tm_mlp = _pick_tile(n_main, 1400)
    tf = 512
    tab_p = _rope_tables(jnp.arange(t_all))
    tab_s = _rope_tables(jnp.tile(past_len + jnp.arange(SAMPLE_ROWS), db))
    ck = cache_k.transpose(0, 1, 3, 4, 2).reshape(cache_k.shape[0], cache_k.shape[1], d, page)
    cv = cache_v.reshape(cache_v.shape[0], cache_v.shape[1], page * h_a, 2 * HEAD_A)
    pp = 16 if n_pages % 16 == 0 else 1

    n_a = w_in_a.shape[0]
    kv_prompt = kv_sample = None
    gla_p, gla_s, hgrn_p, hgrn_s = [], [], [], []
    for i in range(depth):
        kind, j = i % N_MIXERS, i // N_MIXERS
        nw = norm_mix[i].reshape(1, d)
        if kind == 0:
            lam_init = 0.8 - 0.6 * math.exp(-0.3 * i)
            w = w_in_a[j].astype(BF16)
            lam = lam_a[j].astype(F32)
            subln = subln_a[j].reshape(1, 2 * HEAD_A)
            qp, *kv_prompt = _proj_a(hp, nw, w, tab_p, tm_main, t_all // tm_main, n_a, j,
                                     kv_prompt)
            qs, *kv_sample = _proj_a(hs, nw, w, tab_s, n_small, 1, n_a, j, kv_sample)
            ks, vs = kv_sample[0][j], kv_sample[1][j]
            shp = (n_a, bp, t_all, d)
            ap = _attn_prompt(qp.reshape(shp[1:]), kv_prompt[0].reshape(shp),
                              kv_prompt[1].reshape(shp), j, lam, subln, lam_init,
                              N_META_TOKENS, 4).reshape(n_main, d)
            ks4 = ks.reshape(db, SAMPLE_ROWS, 2 * h_a, HEAD_A)
            vs4 = vs.reshape(db, SAMPLE_ROWS, h_a, 2 * HEAD_A)
            q_rows = qs.reshape(db, SAMPLE_ROWS, 2 * h_a, HEAD_A)[:, :n_new]
            q_rows = q_rows.transpose(0, 2, 1, 3).reshape(db, 2 * h_a * n_new, HEAD_A)
            k_rows = jnp.pad(ks.reshape(db, SAMPLE_ROWS, d).transpose(0, 2, 1),
                             ((0, 0), (0, 0), (0, page - SAMPLE_ROWS)))
            v_rows = jnp.pad(vs4.reshape(db, SAMPLE_ROWS * h_a, 2 * HEAD_A),
                             ((0, 0), (0, (page - SAMPLE_ROWS) * h_a), (0, 0)))
            as_ = _attn_sample(q_rows, k_rows, v_rows, ck, cv, page_table, j, lam, subln,
                               lam_init, n_new, pp)
            w_out = w_out_a[j]
        elif kind == 1:
            w = w_in_b[j]
            n_main_cols = 2 * kd_b + 2 * d
            w_main = w[:, :n_main_cols].astype(BF16)
            w_gl = jnp.pad(w[:, n_main_cols:], ((0, 0), (0, LANES - GATE_RANK))).astype(BF16)
            w_g2 = jnp.pad(w_gate2_b[j], ((0, LANES - GATE_RANK), (0, 0))).astype(BF16)
            bg = b_gate_b[j].reshape(1, kd_b)
            gn = gnorm_b[j].reshape(1, dv_b)
            qp, kp, vp, rp, gp = _proj_b(hp, nw, w_main, w_gl, w_g2, bg, tm_main)
            qs, ks, vs, rs, gs = _proj_b(hs, nw, w_main, w_gl, w_g2, bg, n_small)
            r3 = lambda a: a.reshape(bp, t_all, a.shape[1])
            ap, sp = _recur_prompt(r3(qp), r3(kp), r3(gp), r3(vp), r3(rp), gn, N_META_TOKENS,
                                   MAIN_CHUNK, 2)
            ap = ap.reshape(n_main, d)
            as_, ss = _recur_sample(qs, ks, gs, vs, rs, state_gla[j].astype(F32), gn, n_new)
            gla_p.append(sp)
            gla_s.append(ss)
            w_out = w_out_b[j]
        else:
            w = w_in_c[j].astype(BF16)
            gn = gnorm_c[j].reshape(1, dv_c)
            lbf = lb_c.astype(F32)
            qp, kp, gp, vp, rp = _proj_c(hp, nw, w, lbf, i, tm_main)
            qs, ks, gs, vs, rs = _proj_c(hs, nw, w, lbf, i, n_small)
            r3 = lambda a: a.reshape(bp, t_all, a.shape[1])
            ap, sp = _recur_prompt(r3(qp), r3(kp), r3(gp), r3(vp), r3(rp), gn, N_META_TOKENS,
                                   MAIN_CHUNK, 4)
            ap = ap.reshape(n_main, d)
            as_, ss = _recur_sample(qs, ks, gs, vs, rs, state_hgrn[j].astype(F32), gn, n_new)
            hgrn_p.append(sp)
            hgrn_s.append(ss)
            w_out = w_out_c[j]
        w_out = w_out.astype(BF16)
        nf = norm_ffn[i].reshape(1, d)
        wu, wd = w_up[i].astype(BF16), w_down[i].astype(BF16)
        hp = _mix_mlp(ap, w_out, hp, nf, wu, wd, tm_mlp, tf)
        hs = _mix_mlp(as_, w_out, hs, nf, wu, wd, n_small, tf)

    nfin = norm_final.reshape(1, d)
    y_prompt = _final_norm_prompt(hp.reshape(bp, t_all, d), nfin, N_META_TOKENS)
    y_sample = _final_norm(hs, nfin).reshape(db, SAMPLE_ROWS, d)[:, :n_new]
    k_sample, v_sample = (a.reshape(n_a, db, SAMPLE_ROWS, d)[:, :, :n_new] for a in kv_sample)
    return (y_prompt, y_sample,
            kv_prompt[0].reshape(n_a, bp, t_all, 2 * h_a, HEAD_A),
            kv_prompt[1].reshape(n_a, bp, t_all, h_a, 2 * HEAD_A),
            k_sample.reshape(n_a, db, n_new, 2 * h_a, HEAD_A),
            v_sample.reshape(n_a, db, n_new, h_a, 2 * HEAD_A),
            jnp.stack(gla_p), jnp.stack(gla_s), jnp.stack(hgrn_p), jnp.stack(hgrn_s))
```

```python
import functools
import math

import numpy as np
import jax
import jax.numpy as jnp
from jax import lax
from jax.experimental import pallas as pl
from jax.experimental.pallas import tpu as pltpu

F32 = jnp.float32
BF16 = jnp.bfloat16

RMS_EPS = 1e-6
N_META_TOKENS = 16
N_MIXERS = 3
HEAD_A = 64
ROT_DIM = HEAD_A // 4
ROPE_THETA = 500000.0
H_B = 4
GATE_RANK = 16
GATE_TEMP = 16.0
DK = 128
SAMPLE_ROWS = 8
SMALL_CHUNK = 16
MAIN_CHUNK = 64
LANES = 128
ONES_ROWS = 16
VMEM_LIMIT = 56 * 1024 * 1024
LOG2E = 1.4426950408889634

NT_DIMS = (((1,), (1,)), ((), ()))
TN_DIMS = (((0,), (0,)), ((), ()))


def _cparams(*sem):
    return pltpu.CompilerParams(dimension_semantics=sem, vmem_limit_bytes=VMEM_LIMIT)


def _pick_tile(n, target, mult=16):
    best = None
    for t in range(mult, min(n, target) + 1, mult):
        if n % t == 0:
            best = t
    return n if best is None else best


def _rms(x, w):
    ms = jnp.mean(x * x, axis=-1, keepdims=True)
    return x * lax.rsqrt(ms + RMS_EPS) * w


def _sigmoid(x):
    return 1.0 / (1.0 + jnp.exp(-x))


def _proj_a_kernel(x_ref, nw_ref, w_ref, c_ref, sm_ref, sp_ref, *rest, cw):
    q_ref, k_ref, v_ref = rest[-3:]
    xn = _rms(x_ref[...], nw_ref[...]).astype(BF16)
    d = q_ref.shape[1]
    rep = cw // LANES
    cos = jnp.tile(c_ref[...], (1, rep))
    sin_m = jnp.tile(sm_ref[...], (1, rep))
    sin_p = jnp.tile(sp_ref[...], (1, rep))
    for dst_i, dst in enumerate((q_ref, k_ref, v_ref)):
        for c in range(d // cw):
            col = dst_i * d + c * cw
            y = jnp.dot(xn, w_ref[:, col:col + cw], preferred_element_type=F32)
            if dst_i < 2:
                y = (y * cos + pltpu.roll(y, ROT_DIM // 2, 1) * sin_p
                     + pltpu.roll(y, cw - ROT_DIM // 2, 1) * sin_m)
            dst[:, c * cw:(c + 1) * cw] = y


def _proj_b_kernel(x_ref, nw_ref, w_ref, wgl_ref, wg2_ref, bg_ref,
                   q_ref, k_ref, v_ref, r_ref, g_ref, *, cw):
    xn = _rms(x_ref[...], nw_ref[...]).astype(BF16)
    col = 0
    for dst, scale in ((q_ref, DK ** -0.5), (k_ref, None), (v_ref, None), (r_ref, None)):
        for c in range(dst.shape[1] // cw):
            y = jnp.dot(xn, w_ref[:, col:col + cw], preferred_element_type=F32)
            if scale is not None:
                y = y * scale
            dst[:, c * cw:(c + 1) * cw] = y.astype(dst.dtype)
            col += cw
    gl = jnp.dot(xn, wgl_ref[...], preferred_element_type=F32).astype(BF16)
    z = jnp.dot(gl, wg2_ref[...], preferred_element_type=F32) + bg_ref[...]
    g_ref[...] = (jnp.minimum(z, 0.0) - jnp.log(1.0 + jnp.exp(-jnp.abs(z)))) * (1.0 / GATE_TEMP)


def _proj_c_kernel(x_ref, nw_ref, w_ref, lb_ref, q_ref, k_ref, g_ref, v_ref, r_ref, *, cw, layer):
    xn = _rms(x_ref[...], nw_ref[...]).astype(BF16)
    d = q_ref.shape[1]
    lbs = lb_ref[...]
    mx = jnp.max(lbs, axis=0, keepdims=True)
    e = jnp.exp(lbs - mx)
    lb = jnp.sum(e[1:layer + 1], axis=0, keepdims=True) / jnp.sum(e, axis=0, keepdims=True)
    for c in range(d // cw):
        sl = slice(c * cw, (c + 1) * cw)
        y = jnp.dot(xn, w_ref[:, c * cw:(c + 1) * cw], preferred_element_type=F32)
        q_ref[:, sl] = y * _sigmoid(y) * (DK ** -0.5)
        f = jnp.dot(xn, w_ref[:, d + c * cw:d + (c + 1) * cw], preferred_element_type=F32)
        lbc = lb[:, sl]
        forget = lbc + (1.0 - lbc) * _sigmoid(f)
        k_ref[:, sl] = 1.0 - forget
        g_ref[:, sl] = jnp.log(forget)
        v_ref[:, sl] = jnp.dot(xn, w_ref[:, 2 * d + c * cw:2 * d + (c + 1) * cw],
                               preferred_element_type=F32).astype(v_ref.dtype)
        r_ref[:, sl] = jnp.dot(xn, w_ref[:, 3 * d + c * cw:3 * d + (c + 1) * cw],
                               preferred_element_type=F32)


def _row_spec(tm, width):
    return pl.BlockSpec((tm, width), lambda i: (i, 0))


def _full_spec(shape):
    return pl.BlockSpec(shape, lambda i: (0,) * len(shape))


def _proj_a(x, nw, w, tables, tm, table_blocks, n_layers, layer_j, kv_prev):
    n, d = x.shape
    cw = min(512, d)
    tab_spec = pl.BlockSpec((tm, LANES), lambda i: (i % table_blocks, 0))
    kv_shape = jax.ShapeDtypeStruct((n_layers, n, d), F32)
    kv_spec = pl.BlockSpec((None, tm, d), lambda i: (layer_j, i, 0))
    in_specs = [_row_spec(tm, d), _full_spec((1, d)), _full_spec(w.shape),
                tab_spec, tab_spec, tab_spec]
    args = [x, nw, w, *tables]
    aliases = {}
    if kv_prev is not None:
        aliases = {len(args): 1, len(args) + 1: 2}
        in_specs += [pl.BlockSpec(memory_space=pl.ANY)] * 2
        args += list(kv_prev)
    return pl.pallas_call(
        functools.partial(_proj_a_kernel, cw=cw),
        out_shape=(jax.ShapeDtypeStruct((n, d), F32), kv_shape, kv_shape),
        grid=(n // tm,),
        in_specs=in_specs,
        out_specs=(_row_spec(tm, d), kv_spec, kv_spec),
        input_output_aliases=aliases,
        compiler_params=_cparams("parallel"),
        name="proj_a",
    )(*args)


def _proj_b(x, nw, w, wgl, wg2, bg, tm):
    n, d = x.shape
    kd, vd = wg2.shape[1], d
    cw = min(512, kd)
    return pl.pallas_call(
        functools.partial(_proj_b_kernel, cw=cw),
        out_shape=(jax.ShapeDtypeStruct((n, kd), F32), jax.ShapeDtypeStruct((n, kd), F32),
                   jax.ShapeDtypeStruct((n, vd), BF16), jax.ShapeDtypeStruct((n, vd), F32),
                   jax.ShapeDtypeStruct((n, kd), F32)),
        grid=(n // tm,),
        in_specs=[_row_spec(tm, d), _full_spec((1, d)), _full_spec(w.shape),
                  _full_spec(wgl.shape), _full_spec(wg2.shape), _full_spec((1, kd))],
        out_specs=(_row_spec(tm, kd), _row_spec(tm, kd), _row_spec(tm, vd), _row_spec(tm, vd),
                   _row_spec(tm, kd)),
        compiler_params=_cparams("parallel"),
        name="proj_b",
    )(x, nw, w, wgl, wg2, bg)


def _proj_c(x, nw, w, lb_c, layer, tm):
    n, d = x.shape
    cw = min(512, d)
    out = jax.ShapeDtypeStruct((n, d), F32)
    return pl.pallas_call(
        functools.partial(_proj_c_kernel, cw=cw, layer=layer),
        out_shape=(out, out, out, jax.ShapeDtypeStruct((n, d), BF16), out),
        grid=(n // tm,),
        in_specs=[_row_spec(tm, d), _full_spec((1, d)), _full_spec(w.shape),
                  _full_spec(lb_c.shape)],
        out_specs=(_row_spec(tm, d),) * 5,
        compiler_params=_cparams("parallel"),
        name="proj_c",
    )(x, nw, w, lb_c)


def _lambda_full(lam_ref, lam_init):
    l = lam_ref[...]
    a = jnp.sum(l[0:1] * l[1:2], axis=-1, keepdims=True)
    b = jnp.sum(l[2:3] * l[3:4], axis=-1, keepdims=True)
    return jnp.exp(a) - jnp.exp(b) + lam_init


def _head_finish(o, subln, lam_init):
    ms = jnp.mean(o * o, axis=-1, keepdims=True)
    return o * lax.rsqrt(ms + RMS_EPS) * subln * (1.0 - lam_init)


def _attn_prompt_kernel(lam_ref, subln_ref, q_ref, k_ref, v_ref, o_ref,
                        kb_ref, vt_ref, vt2_ref, km_ref, vtm_ref, m_ref, acc_ref, bias_ref,
                        *, tq, n_meta, lam_init, hb):
    t_all = q_ref.shape[0]
    nq = (t_all - n_meta) // tq
    hw = 2 * HEAD_A
    lam = _lambda_full(lam_ref, lam_init)
    subln = subln_ref[...]
    first_map = lax.broadcasted_iota(jnp.int32, (1, hw), 1) < HEAD_A

    def stack_q(qt, scale):
        qt = qt * scale
        return jnp.concatenate([jnp.where(first_map, qt, 0.0), jnp.where(first_map, 0.0, qt)],
                               axis=0).astype(BF16)

    r = lax.broadcasted_iota(jnp.int32, (2 * n_meta, n_meta), 0)
    c = lax.broadcasted_iota(jnp.int32, (2 * n_meta, n_meta), 1)
    meta_causal = c <= jnp.where(r >= n_meta, r - n_meta, r)
    pad = jnp.zeros((hw - n_meta, hw), F32)
    for h in range(hb):
        lanes = slice(h * hw, (h + 1) * hw)
        kmeta = k_ref[0:n_meta, lanes]
        vmeta = v_ref[0:n_meta, lanes]
        qs = stack_q(q_ref[0:n_meta, lanes], HEAD_A ** -0.5)
        s = lax.dot_general(qs, kmeta.astype(BF16), NT_DIMS, preferred_element_type=F32)
        s = jnp.where(meta_causal, s, -jnp.inf)
        p = jnp.exp(s - jnp.max(s, axis=-1, keepdims=True))
        acc = jnp.dot(p.astype(BF16), vmeta.astype(BF16), preferred_element_type=F32)
        acc = acc / jnp.sum(p, axis=-1, keepdims=True)
        o = acc[:n_meta] - lam * acc[n_meta:]
        o_ref[0:n_meta, lanes] = _head_finish(o, subln, lam_init).astype(o_ref.dtype)
        km_ref[h] = jnp.concatenate([kmeta, pad], axis=0).astype(BF16)
        vtm_ref[h] = jnp.concatenate(
            [jnp.concatenate([vmeta, pad], axis=0).T.astype(BF16), jnp.ones((ONES_ROWS, hw), BF16)],
            axis=0)
        for cidx in range(nq):
            rows = slice(n_meta + cidx * tq, n_meta + (cidx + 1) * tq)
            kb_ref[h, cidx] = k_ref[rows, lanes].astype(BF16)
            vte = jnp.concatenate([v_ref[rows, lanes].T.astype(BF16),
                                   jnp.ones((ONES_ROWS, tq), BF16)], axis=0)
            vt_ref[h, cidx] = vte
            if cidx < 2 * (nq // 2):
                vt2_ref[h, cidx // 2, :, (cidx % 2) * tq:(cidx % 2 + 1) * tq] = vte

    key_i = lax.broadcasted_iota(jnp.int32, (hw + tq, 2 * tq), 0)
    qry_i = lax.broadcasted_iota(jnp.int32, (hw + tq, 2 * tq), 1)
    key_limit = jnp.where(key_i < hw, n_meta - 1, hw + jnp.where(qry_i >= tq, qry_i - tq, qry_i))
    bias_ref[...] = jnp.where(key_i <= key_limit, 0.0, -jnp.inf)

    def online(h, s, vte):
        m_old = m_ref[h]
        m_new = jnp.maximum(m_old, jnp.max(s, axis=0, keepdims=True))
        alpha = jnp.exp2(m_old - m_new)
        p = jnp.exp2(s - m_new).astype(BF16)
        acc_ref[h] = alpha * acc_ref[h] + jnp.dot(vte, p, preferred_element_type=F32)
        m_ref[h] = m_new

    def q_body(qi, _):
        q0 = pl.multiple_of(n_meta + qi * tq, 16)
        qss = [stack_q(q_ref[pl.ds(q0, tq), h * hw:(h + 1) * hw], (HEAD_A ** -0.5) * LOG2E)
               for h in range(hb)]
        for h in range(hb):
            s = jnp.concatenate(
                [lax.dot_general(km_ref[h], qss[h], NT_DIMS, preferred_element_type=F32),
                 lax.dot_general(kb_ref[h, qi], qss[h], NT_DIMS, preferred_element_type=F32)],
                axis=0)
            s = s + bias_ref[...]
            m = jnp.max(s, axis=0, keepdims=True)
            p = jnp.exp2(s - m).astype(BF16)
            m_ref[h] = m
            acc_ref[h] = (jnp.dot(vtm_ref[h], p[:hw], preferred_element_type=F32)
                          + jnp.dot(vt_ref[h, qi], p[hw:], preferred_element_type=F32))

        def pair_body(j, _):
            for h in range(hb):
                kpair = kb_ref[h, pl.ds(2 * j, 2)].reshape(2 * tq, hw)
                s = lax.dot_general(kpair, qss[h], NT_DIMS, preferred_element_type=F32)
                online(h, s, vt2_ref[h, j])
            return 0

        lax.fori_loop(0, qi // 2, pair_body, 0)

        @pl.when(qi % 2 == 1)
        def _():
            for h in range(hb):
                s = lax.dot_general(kb_ref[h, qi - 1], qss[h], NT_DIMS,
                                    preferred_element_type=F32)
                online(h, s, vt_ref[h, qi - 1])

        for h in range(hb):
            acc = acc_ref[h]
            ot = acc[:hw] * (1.0 / acc[hw:hw + 1])
            od = ot[:, :tq] - lam * ot[:, tq:]
            ms = jnp.mean(od * od, axis=0, keepdims=True)
            od = od * lax.rsqrt(ms + RMS_EPS)
            o_ref[pl.ds(q0, tq), h * hw:(h + 1) * hw] = (
                od.T * (subln * (1.0 - lam_init))).astype(o_ref.dtype)
        return 0

    lax.fori_loop(0, nq, q_body, 0)


def _attn_prompt(q, k, v, layer_j, lam, subln, lam_init, n_meta, hb):
    b, t, d = q.shape
    hw = 2 * HEAD_A
    tq = _pick_tile(t - n_meta, 512, LANES)
    nq = (t - n_meta) // tq
    blk = pl.BlockSpec((None, t, hb * hw), lambda i, h: (i, 0, h))
    kvblk = pl.BlockSpec((None, None, t, hb * hw), lambda i, h: (layer_j, i, 0, h))
    return pl.pallas_call(
        functools.partial(_attn_prompt_kernel, tq=tq, n_meta=n_meta, lam_init=lam_init, hb=hb),
        out_shape=jax.ShapeDtypeStruct((b, t, d), BF16),
        grid=(b, d // (hb * hw)),
        in_specs=[pl.BlockSpec(lam.shape, lambda i, h: (0, 0)),
                  pl.BlockSpec((1, hw), lambda i, h: (0, 0)), blk, kvblk, kvblk],
        out_specs=blk,
        scratch_shapes=[pltpu.VMEM((hb, nq, tq, hw), BF16),
                        pltpu.VMEM((hb, nq, hw + ONES_ROWS, tq), BF16),
                        pltpu.VMEM((hb, max(nq // 2, 1), hw + ONES_ROWS, 2 * tq), BF16),
                        pltpu.VMEM((hb, hw, hw), BF16), pltpu.VMEM((hb, hw + ONES_ROWS, hw), BF16),
                        pltpu.VMEM((hb, 1, 2 * tq), F32),
                        pltpu.VMEM((hb, hw + ONES_ROWS, 2 * tq), F32),
                        pltpu.VMEM((hw + tq, 2 * tq), F32)],
        compiler_params=_cparams("parallel", "parallel"),
        name="attn_prompt",
    )(lam, subln, q, k, v)


def _attn_sample_kernel(pt_ref, lam_ref, subln_ref, q_ref, kn_ref, vn_ref, *rest,
                        pp, n_new, lam_init):
    del pt_ref
    k_refs, v_refs = rest[:pp], rest[pp:2 * pp]
    o_ref, qbd_ref, m_ref, l_ref, acc_ref = rest[2 * pp:]
    step = pl.program_id(1)
    rows = q_ref.shape[0]
    n_heads = rows // n_new
    n_vheads = n_heads // 2
    page = k_refs[0].shape[1]
    grp = 4 * n_new

    def scores(k_list):
        qbd = qbd_ref[...]
        return jnp.concatenate(
            [jnp.dot(qbd, kr[...].astype(BF16), preferred_element_type=F32) for kr in k_list],
            axis=1)

    def values(p, v_list):
        out = []
        for g in range(rows // grp):
            pg = p[g * grp:(g + 1) * grp]
            halves = []
            for vh in (2 * g, 2 * g + 1):
                r = None
                for i, vr in enumerate(v_list):
                    v = vr[pl.ds(vh, page, stride=n_vheads), :].astype(BF16)
                    t = jnp.dot(pg[:, i * page:(i + 1) * page], v, preferred_element_type=F32)
                    r = t if r is None else r + t
                halves.append(r)
            out.append(halves[0][:grp // 2])
            out.append(halves[1][grp // 2:])
        return jnp.concatenate(out, axis=0)

    @pl.when(step == 0)
    def _():
        d = n_heads * HEAD_A
        q = jnp.tile(q_ref[...] * ((HEAD_A ** -0.5) * LOG2E), (1, n_heads))
        row_head = lax.broadcasted_iota(jnp.int32, (rows, d), 0) // n_new
        lane_head = lax.broadcasted_iota(jnp.int32, (rows, d), 1) // HEAD_A
        qbd_ref[...] = jnp.where(row_head == lane_head, q, 0.0).astype(BF16)
        s = scores([kn_ref])
        tok = lax.broadcasted_iota(jnp.int32, (rows, page), 0) % n_new
        key = lax.broadcasted_iota(jnp.int32, (rows, page), 1)
        s = jnp.where(key <= tok, s, -jnp.inf)
        m = jnp.max(s, axis=-1, keepdims=True)
        p = jnp.exp2(s - m)
        m_ref[...] = m
        l_ref[...] = jnp.sum(p, axis=-1, keepdims=True)
        acc_ref[...] = values(p.astype(BF16), [vn_ref])

    s = scores(k_refs)
    m_old = m_ref[...]
    m_new = jnp.maximum(m_old, jnp.max(s, axis=-1, keepdims=True))
    alpha = jnp.exp2(m_old - m_new)
    p = jnp.exp2(s - m_new)
    l_ref[...] = alpha * l_ref[...] + jnp.sum(p, axis=-1, keepdims=True)
    acc_ref[...] = alpha * acc_ref[...] + values(p.astype(BF16), v_refs)
    m_ref[...] = m_new

    @pl.when(step == pl.num_programs(1) - 1)
    def _():
        lam = _lambda_full(lam_ref, lam_init)
        subln = subln_ref[...]
        hw = 2 * HEAD_A
        full = acc_ref[...] / l_ref[...]
        diff = full - lam * pltpu.roll(full, rows - n_new, 0)
        for h in range(n_heads // 2):
            tile = diff[2 * n_new * h:2 * n_new * (h + 1)]
            o_ref[:, h * hw:(h + 1) * hw] = _head_finish(tile, subln, lam_init).astype(o_ref.dtype)


def _attn_sample(q, k_new, v_new, cache_k, cache_v, page_table, layer_j, lam, subln, lam_init,
                 n_new, pp):
    nb, rows, _ = q.shape
    n_pages = page_table.shape[1]
    krows, page = cache_k.shape[2:]
    vrows = cache_v.shape[2]
    hw = 2 * HEAD_A
    d = (rows // n_new) * HEAD_A
    assert 2 * n_new == SAMPLE_ROWS and krows == d and vrows * 2 * HEAD_A == page * d

    def per_sample(shape):
        return pl.BlockSpec((None,) + shape, lambda b, s, pt: (b, 0, 0))

    def kpage(i):
        return pl.BlockSpec((None, None, krows, page),
                            lambda b, s, pt: (layer_j, pt[b, s * pp + i], 0, 0))

    def vpage(i):
        return pl.BlockSpec((None, None, vrows, hw),
                            lambda b, s, pt: (layer_j, pt[b, s * pp + i], 0, 0))

    grid_spec = pltpu.PrefetchScalarGridSpec(
        num_scalar_prefetch=1,
        grid=(nb, n_pages // pp),
        in_specs=[pl.BlockSpec(lam.shape, lambda b, s, pt: (0, 0)),
                  pl.BlockSpec((1, hw), lambda b, s, pt: (0, 0)),
                  per_sample(q.shape[1:]), per_sample(k_new.shape[1:]),
                  per_sample(v_new.shape[1:])]
                 + [kpage(i) for i in range(pp)] + [vpage(i) for i in range(pp)],
        out_specs=pl.BlockSpec((SAMPLE_ROWS, d), lambda b, s, pt: (b, 0)),
        scratch_shapes=[pltpu.VMEM((rows, d), BF16), pltpu.VMEM((rows, 1), F32),
                        pltpu.VMEM((rows, 1), F32), pltpu.VMEM((rows, hw), F32)],
    )
    return pl.pallas_call(
        functools.partial(_attn_sample_kernel, pp=pp, n_new=n_new, lam_init=lam_init),
        out_shape=jax.ShapeDtypeStruct((nb * SAMPLE_ROWS, d), BF16),
        grid_spec=grid_spec,
        compiler_params=_cparams("parallel", "arbitrary"),
        name="attn_sample",
    )(page_table, lam, subln, q, k_new, v_new, *([cache_k] * pp), *([cache_v] * pp))


def _chunk_constants(length):
    nlev = int(math.log2(length))
    assert 2 ** nlev == length
    w = np.zeros(((2 + nlev) * length, length), np.float32)
    masks = np.zeros((nlev + 1, length, length), np.float32)
    masks[0] = np.eye(length)
    for t in range(length):
        w[t, :t + 1] = 1.0
        w[length + t, t + 1:] = 1.0
    for lvl in range(1, nlev + 1):
        bs, half = 2 ** lvl, 2 ** (lvl - 1)
        for t in range(length):
            mid = t - t % bs + half
            row = (1 + lvl) * length + t
            if t >= mid:
                w[row, mid:t + 1] = 1.0
                masks[lvl, t, mid - half:mid] = 1.0
            else:
                w[row, t + 1:mid] = 1.0
    return jnp.asarray(np.tile(w, (1, 3)), BF16), jnp.asarray(masks, F32)


def _chunk_exponents(g, w3):
    g = g * LOG2E
    g1 = g.astype(BF16)
    rem = g - g1.astype(F32)
    g2 = rem.astype(BF16)
    g3 = (rem - g2.astype(F32)).astype(BF16)
    return jnp.exp2(jnp.dot(w3, jnp.concatenate([g1, g2, g3], axis=0),
                            preferred_element_type=F32))


def _chunk_head(q, k, vb, st, ex, masks):
    length = q.shape[0]
    nlev = masks.shape[0] - 1
    e_cum = ex[0:length]
    e_rem = ex[length:2 * length]
    e_last = e_cum[length - 1:length]
    o = lax.dot_general((q * e_cum).astype(BF16), st.astype(BF16), NT_DIMS,
                        preferred_element_type=F32)
    qb, kb = q.astype(BF16), k.astype(BF16)
    a = masks[0] * lax.dot_general(qb, kb, NT_DIMS, preferred_element_type=F32)
    row = lax.broadcasted_iota(jnp.int32, q.shape, 0)
    for lvl in range(1, nlev + 1):
        second_half = (row & (2 ** lvl - 1)) >= 2 ** (lvl - 1)
        x = (jnp.where(second_half, q, k) * ex[(1 + lvl) * length:(2 + lvl) * length]).astype(BF16)
        a = a + masks[lvl] * lax.dot_general(x, x, NT_DIMS, preferred_element_type=F32)
    o = o + jnp.dot(a.astype(BF16), vb, preferred_element_type=F32)
    kd = (k * e_rem).astype(BF16)
    st = st * e_last + lax.dot_general(vb, kd, TN_DIMS, preferred_element_type=F32)
    return o, st


def _gate_norm(o, r, gn):
    ms = jnp.mean(o * o, axis=-1, keepdims=True)
    return o * lax.rsqrt(ms + RMS_EPS) * gn * (r * _sigmoid(r))


def _recur_prompt_kernel(gn_ref, ws_ref, ms_ref, wm_ref, mm_ref, q_ref, k_ref, g_ref, v_ref, r_ref,
                         o_ref, s_ref, st_ref, *, n_meta, chunk, hb):
    t_all = q_ref.shape[0]
    dv = v_ref.shape[1] // hb
    gn = gn_ref[...]
    st_ref[...] = jnp.zeros_like(st_ref)

    def run(r0, length, w3, masks):
        rows = pl.ds(r0, length)
        ex = _chunk_exponents(g_ref[rows], w3)
        for h in range(hb):
            ksl = slice(h * DK, (h + 1) * DK)
            vsl = slice(h * dv, (h + 1) * dv)
            o, st = _chunk_head(q_ref[rows, ksl], k_ref[rows, ksl], v_ref[rows, vsl], st_ref[h],
                                ex[:, ksl], masks)
            st_ref[h] = st
            o_ref[rows, vsl] = _gate_norm(o, r_ref[rows, vsl], gn).astype(o_ref.dtype)

    run(0, n_meta, ws_ref[...], ms_ref[...])

    def body(c, _):
        run(pl.multiple_of(n_meta + c * chunk, 16), chunk, wm_ref[...], mm_ref[...])
        return 0

    lax.fori_loop(0, (t_all - n_meta) // chunk, body, 0, unroll=4)
    for h in range(hb):
        s_ref[h] = st_ref[h].T


def _recur_prompt(q, k, g, v, r, gn, n_meta, chunk, hb):
    b, t, kd = q.shape
    vd = v.shape[2]
    nh = kd // DK
    dv = vd // nh
    ws, ms = _chunk_constants(n_meta)
    wm, mm = _chunk_constants(chunk)
    kblk = pl.BlockSpec((None, t, hb * DK), lambda i, h: (i, 0, h))
    vblk = pl.BlockSpec((None, t, hb * dv), lambda i, h: (i, 0, h))

    def const(x):
        return pl.BlockSpec(x.shape, lambda i, h: (0,) * x.ndim)

    return pl.pallas_call(
        functools.partial(_recur_prompt_kernel, n_meta=n_meta, chunk=chunk, hb=hb),
        out_shape=(jax.ShapeDtypeStruct((b, t, vd), BF16),
                   jax.ShapeDtypeStruct((b, nh, DK, dv), F32)),
        grid=(b, nh // hb),
        in_specs=[pl.BlockSpec((1, dv), lambda i, h: (0, 0)), const(ws), const(ms), const(wm),
                  const(mm), kblk, kblk, kblk, vblk, vblk],
        out_specs=(vblk, pl.BlockSpec((None, hb, DK, dv), lambda i, h: (i, h, 0, 0))),
        scratch_shapes=[pltpu.VMEM((hb, dv, DK), F32)],
        compiler_params=_cparams("parallel", "parallel"),
        name="recur_prompt",
    )(gn, ws, ms, wm, mm, q, k, g, v, r)


def _recur_sample_kernel(gn_ref, w_ref, m_ref, q_ref, k_ref, g_ref, v_ref, r_ref, s0_ref,
                         o_ref, s_ref, *, n_new):
    nh = s0_ref.shape[0]
    dv = s0_ref.shape[2]
    gn = gn_ref[...]
    masks = m_ref[...]
    pad_rows = SMALL_CHUNK - SAMPLE_ROWS
    kd = q_ref.shape[1]
    valid = lax.broadcasted_iota(jnp.int32, (SAMPLE_ROWS, kd), 0) < n_new

    def padded(x):
        return jnp.concatenate([x, jnp.zeros((pad_rows, x.shape[1]), x.dtype)], axis=0)

    q = padded(q_ref[...])
    k = padded(jnp.where(valid, k_ref[...], 0.0))
    ex = _chunk_exponents(padded(jnp.where(valid, g_ref[...], 0.0)), w_ref[...])
    vb = padded(v_ref[...].astype(F32)).astype(BF16)
    for h in range(nh):
        ksl = slice(h * DK, (h + 1) * DK)
        vsl = slice(h * dv, (h + 1) * dv)
        o, st = _chunk_head(q[:, ksl], k[:, ksl], vb[:, vsl], s0_ref[h].T, ex[:, ksl], masks)
        s_ref[h] = st.T
        o_ref[:, vsl] = _gate_norm(o[0:SAMPLE_ROWS], r_ref[:, vsl], gn).astype(o_ref.dtype)


def _recur_sample(q, k, g, v, r, s0, gn, n_new):
    n, kd = q.shape
    vd = v.shape[1]
    nb, nh, _, dv = s0.shape
    w, masks = _chunk_constants(SMALL_CHUNK)
    kblk = pl.BlockSpec((SAMPLE_ROWS, kd), lambda i: (i, 0))
    vblk = pl.BlockSpec((SAMPLE_ROWS, vd), lambda i: (i, 0))
    sblk = pl.BlockSpec((None, nh, DK, dv), lambda i: (i, 0, 0, 0))
    return pl.pallas_call(
        functools.partial(_recur_sample_kernel, n_new=n_new),
        out_shape=(jax.ShapeDtypeStruct((n, vd), BF16), jax.ShapeDtypeStruct(s0.shape, F32)),
        grid=(nb,),
        in_specs=[_full_spec((1, dv)), _full_spec(w.shape), _full_spec(masks.shape),
                  kblk, kblk, kblk, vblk, vblk, sblk],
        out_specs=(vblk, sblk),
        compiler_params=_cparams("parallel"),
        name="recur_sample",
    )(gn, w, masks, q, k, g, v, r, s0)


def _mix_mlp_kernel(a_ref, wo_ref, x_ref, nw_ref, wu_ref, wd_ref, o_ref, xn_ref):
    j = pl.program_id(1)

    @pl.when(j == 0)
    def _():
        h = x_ref[...] + jnp.dot(a_ref[...], wo_ref[...], preferred_element_type=F32)
        xn_ref[...] = _rms(h, nw_ref[...]).astype(BF16)
        o_ref[...] = h

    u = jnp.dot(xn_ref[...], wu_ref[...].astype(BF16), preferred_element_type=F32)
    u = jnp.square(jnp.maximum(u, 0.0)).astype(BF16)
    o_ref[...] += jnp.dot(u, wd_ref[...].astype(BF16), preferred_element_type=F32)


def _mix_mlp(a, wo, x, nw, wu, wd, layer, tm, tf):
    n, d = x.shape
    ff = wu.shape[2]
    return pl.pallas_call(
        _mix_mlp_kernel,
        out_shape=jax.ShapeDtypeStruct((n, d), F32),
        grid=(n // tm, ff // tf),
        in_specs=[pl.BlockSpec((tm, a.shape[1]), lambda i, j: (i, 0)),
                  pl.BlockSpec(wo.shape, lambda i, j: (0, 0)),
                  pl.BlockSpec((tm, d), lambda i, j: (i, 0)),
                  pl.BlockSpec((1, d), lambda i, j: (0, 0)),
                  pl.BlockSpec((None, d, tf), lambda i, j: (layer, 0, j)),
                  pl.BlockSpec((None, tf, d), lambda i, j: (layer, j, 0))],
        out_specs=pl.BlockSpec((tm, d), lambda i, j: (i, 0)),
        scratch_shapes=[pltpu.VMEM((tm, d), BF16)],
        compiler_params=_cparams("parallel", "arbitrary"),
        name="mix_mlp",
    )(a, wo, x, nw, wu, wd)


def _final_norm_prompt_kernel(x_ref, nw_ref, o_ref, *, n_meta):
    tm = o_ref.shape[0]
    r0 = pl.multiple_of(n_meta + pl.program_id(1) * tm, 8)
    o_ref[...] = _rms(x_ref[pl.ds(r0, tm)], nw_ref[...])


def _final_norm_prompt(x, nw, n_meta):
    b, t, d = x.shape
    seq = t - n_meta
    tm = _pick_tile(seq, 512)
    return pl.pallas_call(
        functools.partial(_final_norm_prompt_kernel, n_meta=n_meta),
        out_shape=jax.ShapeDtypeStruct((b, seq, d), F32),
        grid=(b, seq // tm),
        in_specs=[pl.BlockSpec((None, t, d), lambda i, j: (i, 0, 0)),
                  pl.BlockSpec((1, d), lambda i, j: (0, 0))],
        out_specs=pl.BlockSpec((None, tm, d), lambda i, j: (i, j, 0)),
        compiler_params=_cparams("parallel", "arbitrary"),
        name="final_norm_prompt",
    )(x, nw)


def _final_norm_kernel(x_ref, nw_ref, o_ref):
    o_ref[...] = _rms(x_ref[...], nw_ref[...])


def _final_norm(x, nw):
    return pl.pallas_call(
        _final_norm_kernel,
        out_shape=jax.ShapeDtypeStruct(x.shape, F32),
        name="final_norm",
    )(x, nw)


def _rope_tables(pos):
    half = ROT_DIM // 2
    inv_freq = ROPE_THETA ** (-jnp.arange(half, dtype=F32) / half)
    ang = pos.astype(F32)[:, None] * inv_freq[None, :]
    cos, sin = jnp.cos(ang), jnp.sin(ang)
    n = pos.shape[0]
    zeros = jnp.zeros((n, half), F32)
    rest0 = jnp.zeros((n, HEAD_A - ROT_DIM), F32)
    cos_h = jnp.concatenate([cos, cos, jnp.ones((n, HEAD_A - ROT_DIM), F32)], axis=1)
    sin_m = jnp.concatenate([-sin, zeros, rest0], axis=1)
    sin_p = jnp.concatenate([zeros, sin, rest0], axis=1)
    return tuple(jnp.concatenate([t, t], axis=1) for t in (cos_h, sin_m, sin_p))


def kernel(x_prompt, x_sample, cache_k, cache_v, state_gla, state_hgrn, page_table, meta_tokens,
           norm_mix, norm_ffn, norm_final, w_in_a, lam_a, subln_a, w_out_a, w_in_b, w_gate2_b,
           b_gate_b, gnorm_b, w_out_b, w_in_c, lb_c, gnorm_c, w_out_c, w_up, w_down):
    bp, seq, d = x_prompt.shape
    db, n_new = x_sample.shape[:2]
    depth = norm_mix.shape[0]
    n_pages = page_table.shape[1]
    page = cache_k.shape[2]
    past_len = n_pages * page
    t_all = N_META_TOKENS + seq
    n_main = bp * t_all
    n_small = db * SAMPLE_ROWS
    h_a = d // (2 * HEAD_A)
    kd_b = w_gate2_b.shape[2]
    dv_b = d // H_B
    h_c = d // DK
    dv_c = d // h_c

    hp = jnp.concatenate(
        [jnp.broadcast_to(meta_tokens.astype(F32)[None], (bp, N_META_TOKENS, d)), x_prompt],
        axis=1).reshape(n_main, d)
    hs = jnp.pad(x_sample, ((0, 0), (0, SAMPLE_ROWS - n_new), (0, 0))).reshape(n_small, d)

    tm_main = _pick_tile(t_all, 704)
    tm_mlp = _pick_tile(n_main, 1400)
    tf = 512
    tab_p = _rope_tables(jnp.arange(t_all))
    tab_s = _rope_tables(jnp.tile(past_len + jnp.arange(SAMPLE_ROWS), db))
    ck = cache_k.transpose(0, 1, 3, 4, 2).reshape(cache_k.shape[0], cache_k.shape[1], d, page)
    cv = cache_v.reshape(cache_v.shape[0], cache_v.shape[1], page * h_a, 2 * HEAD_A)
    pp = 16 if n_pages % 16 == 0 else 1

    n_a = w_in_a.shape[0]
    kv_prompt = kv_sample = None
    gla_p, gla_s, hgrn_p, hgrn_s = [], [], [], []
    for i in range(depth):
        kind, j = i % N_MIXERS, i // N_MIXERS
        nw = norm_mix[i].reshape(1, d)
        if kind == 0:
            lam_init = 0.8 - 0.6 * math.exp(-0.3 * i)
            w = w_in_a[j].astype(BF16)
            lam = lam_a[j].astype(F32)
            subln = subln_a[j].reshape(1, 2 * HEAD_A)
            qp, *kv_prompt = _proj_a(hp, nw, w, tab_p, tm_main, t_all // tm_main, n_a, j,
                                     kv_prompt)
            qs, *kv_sample = _proj_a(hs, nw, w, tab_s, n_small, 1, n_a, j, kv_sample)
            ks, vs = kv_sample[0][j], kv_sample[1][j]
            shp = (n_a, bp, t_all, d)
            ap = _attn_prompt(qp.reshape(shp[1:]), kv_prompt[0].reshape(shp),
                              kv_prompt[1].reshape(shp), j, lam, subln, lam_init,
                              N_META_TOKENS, 4).reshape(n_main, d)
            ks4 = ks.reshape(db, SAMPLE_ROWS, 2 * h_a, HEAD_A)
            vs4 = vs.reshape(db, SAMPLE_ROWS, h_a, 2 * HEAD_A)
            q_rows = qs.reshape(db, SAMPLE_ROWS, 2 * h_a, HEAD_A)[:, :n_new]
            q_rows = q_rows.transpose(0, 2, 1, 3).reshape(db, 2 * h_a * n_new, HEAD_A)
            k_rows = jnp.pad(ks.reshape(db, SAMPLE_ROWS, d).transpose(0, 2, 1),
                             ((0, 0), (0, 0), (0, page - SAMPLE_ROWS)))
            v_rows = jnp.pad(vs4.reshape(db, SAMPLE_ROWS * h_a, 2 * HEAD_A),
                             ((0, 0), (0, (page - SAMPLE_ROWS) * h_a), (0, 0)))
            as_ = _attn_sample(q_rows, k_rows, v_rows, ck, cv, page_table, j, lam, subln,
                               lam_init, n_new, pp)
            w_out = w_out_a[j]
        elif kind == 1:
            w = w_in_b[j]
            n_main_cols = 2 * kd_b + 2 * d
            w_main = w[:, :n_main_cols].astype(BF16)
            w_gl = jnp.pad(w[:, n_main_cols:], ((0, 0), (0, LANES - GATE_RANK))).astype(BF16)
            w_g2 = jnp.pad(w_gate2_b[j], ((0, LANES - GATE_RANK), (0, 0))).astype(BF16)
            bg = b_gate_b[j].reshape(1, kd_b)
            gn = gnorm_b[j].reshape(1, dv_b)
            qp, kp, vp, rp, gp = _proj_b(hp, nw, w_main, w_gl, w_g2, bg, tm_main)
            qs, ks, vs, rs, gs = _proj_b(hs, nw, w_main, w_gl, w_g2, bg, n_small)
            r3 = lambda a: a.reshape(bp, t_all, a.shape[1])
            ap, sp = _recur_prompt(r3(qp), r3(kp), r3(gp), r3(vp), r3(rp), gn, N_META_TOKENS,
                                   MAIN_CHUNK, 2)
            ap = ap.reshape(n_main, d)
            as_, ss = _recur_sample(qs, ks, gs, vs, rs, state_gla[j].astype(F32), gn, n_new)
            gla_p.append(sp)
            gla_s.append(ss)
            w_out = w_out_b[j]
        else:
            w = w_in_c[j].astype(BF16)
            gn = gnorm_c[j].reshape(1, dv_c)
            lbf = lb_c.astype(F32)
            qp, kp, gp, vp, rp = _proj_c(hp, nw, w, lbf, i, tm_main)
            qs, ks, gs, vs, rs = _proj_c(hs, nw, w, lbf, i, n_small)
            r3 = lambda a: a.reshape(bp, t_all, a.shape[1])
            ap, sp = _recur_prompt(r3(qp), r3(kp), r3(gp), r3(vp), r3(rp), gn, N_META_TOKENS,
                                   MAIN_CHUNK, 4)
            ap = ap.reshape(n_main, d)
            as_, ss = _recur_sample(qs, ks, gs, vs, rs, state_hgrn[j].astype(F32), gn, n_new)
            hgrn_p.append(sp)
            hgrn_s.append(ss)
            w_out = w_out_c[j]
        w_out = w_out.astype(BF16)
        nf = norm_ffn[i].reshape(1, d)
        hp = _mix_mlp(ap, w_out, hp, nf, w_up, w_down, i, tm_mlp, tf)
        hs = _mix_mlp(as_, w_out, hs, nf, w_up, w_down, i, n_small, tf)

    nfin = norm_final.reshape(1, d)
    y_prompt = _final_norm_prompt(hp.reshape(bp, t_all, d), nfin, N_META_TOKENS)
    y_sample = _final_norm(hs, nfin).reshape(db, SAMPLE_ROWS, d)[:, :n_new]
    k_sample, v_sample = (a.reshape(n_a, db, SAMPLE_ROWS, d)[:, :, :n_new] for a in kv_sample)
    return (y_prompt, y_sample,
            kv_prompt[0].reshape(n_a, bp, t_all, 2 * h_a, HEAD_A),
            kv_prompt[1].reshape(n_a, bp, t_all, h_a, 2 * HEAD_A),
            k_sample.reshape(n_a, db, n_new, 2 * h_a, HEAD_A),
            v_sample.reshape(n_a, db, n_new, h_a, 2 * HEAD_A),
            jnp.stack(gla_p), jnp.stack(gla_s), jnp.stack(hgrn_p), jnp.stack(hgrn_s))
```

```python
import functools
import math

import numpy as np
import jax
import jax.numpy as jnp
from jax import lax
from jax.experimental import pallas as pl
from jax.experimental.pallas import tpu as pltpu

F32 = jnp.float32
BF16 = jnp.bfloat16

RMS_EPS = 1e-6
N_META_TOKENS = 16
N_MIXERS = 3
HEAD_A = 64
ROT_DIM = HEAD_A // 4
ROPE_THETA = 500000.0
H_B = 4
GATE_RANK = 16
GATE_TEMP = 16.0
DK = 128
SAMPLE_ROWS = 8
SMALL_CHUNK = 16
GLA_TILING = (64, 2, 8)
HGRN_TILING = (128, 4, 4)
PROJ_ROWS = 704
PROJ_COLS = 512
MLP_ROWS = 1400
MLP_HIDDEN = 512
ATTN_Q_ROWS = 512
ATTN_HEADS = 4
PAGES_PER_STEP = 16
LANES = 128
ONES_ROWS = 16
VMEM_LIMIT = 56 * 1024 * 1024
LOG2E = 1.4426950408889634

NT_DIMS = (((1,), (1,)), ((), ()))
TN_DIMS = (((0,), (0,)), ((), ()))


def _cparams(*sem):
    return pltpu.CompilerParams(dimension_semantics=sem, vmem_limit_bytes=VMEM_LIMIT)


def _pick_tile(n, target, mult=16):
    best = None
    for t in range(mult, min(n, target) + 1, mult):
        if n % t == 0:
            best = t
    return n if best is None else best


def _rms(x, w):
    ms = jnp.mean(x * x, axis=-1, keepdims=True)
    return x * lax.rsqrt(ms + RMS_EPS) * w


def _sigmoid(x):
    return 1.0 / (1.0 + jnp.exp(-x))


def _proj_a_kernel(x_ref, nw_ref, w_ref, c_ref, sm_ref, sp_ref, *rest, cw):
    q_ref, k_ref, v_ref = rest[-3:]
    xn = _rms(x_ref[...], nw_ref[...]).astype(BF16)
    d = q_ref.shape[1]
    rep = cw // LANES
    cos = jnp.tile(c_ref[...], (1, rep))
    sin_m = jnp.tile(sm_ref[...], (1, rep))
    sin_p = jnp.tile(sp_ref[...], (1, rep))
    for dst_i, dst in enumerate((q_ref, k_ref, v_ref)):
        for c in range(d // cw):
            col = dst_i * d + c * cw
            y = jnp.dot(xn, w_ref[:, col:col + cw], preferred_element_type=F32)
            if dst_i < 2:
                y = (y * cos + pltpu.roll(y, ROT_DIM // 2, 1) * sin_p
                     + pltpu.roll(y, cw - ROT_DIM // 2, 1) * sin_m)
            dst[:, c * cw:(c + 1) * cw] = y


def _proj_b_kernel(x_ref, nw_ref, w_ref, wgl_ref, wg2_ref, bg_ref,
                   q_ref, k_ref, v_ref, r_ref, g_ref, *, cw):
    xn = _rms(x_ref[...], nw_ref[...]).astype(BF16)
    col = 0
    for dst, scale in ((q_ref, DK ** -0.5), (k_ref, None), (v_ref, None), (r_ref, None)):
        for c in range(dst.shape[1] // cw):
            y = jnp.dot(xn, w_ref[:, col:col + cw], preferred_element_type=F32)
            if scale is not None:
                y = y * scale
            dst[:, c * cw:(c + 1) * cw] = y.astype(dst.dtype)
            col += cw
    gl = jnp.dot(xn, wgl_ref[...], preferred_element_type=F32).astype(BF16)
    z = jnp.dot(gl, wg2_ref[...], preferred_element_type=F32) + bg_ref[...]
    g_ref[...] = (jnp.minimum(z, 0.0) - jnp.log(1.0 + jnp.exp(-jnp.abs(z)))) * (1.0 / GATE_TEMP)


def _proj_c_kernel(x_ref, nw_ref, w_ref, lb_ref, q_ref, k_ref, g_ref, v_ref, r_ref, *, cw, layer):
    xn = _rms(x_ref[...], nw_ref[...]).astype(BF16)
    d = q_ref.shape[1]
    lbs = lb_ref[...]
    mx = jnp.max(lbs, axis=0, keepdims=True)
    e = jnp.exp(lbs - mx)
    lb = jnp.sum(e[1:layer + 1], axis=0, keepdims=True) / jnp.sum(e, axis=0, keepdims=True)
    for c in range(d // cw):
        sl = slice(c * cw, (c + 1) * cw)
        y = jnp.dot(xn, w_ref[:, c * cw:(c + 1) * cw], preferred_element_type=F32)
        q_ref[:, sl] = y * _sigmoid(y) * (DK ** -0.5)
        f = jnp.dot(xn, w_ref[:, d + c * cw:d + (c + 1) * cw], preferred_element_type=F32)
        lbc = lb[:, sl]
        forget = lbc + (1.0 - lbc) * _sigmoid(f)
        k_ref[:, sl] = 1.0 - forget
        g_ref[:, sl] = jnp.log(forget)
        v_ref[:, sl] = jnp.dot(xn, w_ref[:, 2 * d + c * cw:2 * d + (c + 1) * cw],
                               preferred_element_type=F32).astype(v_ref.dtype)
        r_ref[:, sl] = jnp.dot(xn, w_ref[:, 3 * d + c * cw:3 * d + (c + 1) * cw],
                               preferred_element_type=F32)


def _row_spec(tm, width):
    return pl.BlockSpec((tm, width), lambda i: (i, 0))


def _full_spec(shape):
    return pl.BlockSpec(shape, lambda i: (0,) * len(shape))


def _proj_a(x, nw, w, tables, tm, table_blocks, n_layers, layer_j, kv_prev):
    n, d = x.shape
    cw = min(PROJ_COLS, d)
    tab_spec = pl.BlockSpec((tm, LANES), lambda i: (i % table_blocks, 0))
    kv_shape = jax.ShapeDtypeStruct((n_layers, n, d), F32)
    kv_spec = pl.BlockSpec((None, tm, d), lambda i: (layer_j, i, 0))
    in_specs = [_row_spec(tm, d), _full_spec((1, d)), _full_spec(w.shape),
                tab_spec, tab_spec, tab_spec]
    args = [x, nw, w, *tables]
    aliases = {}
    if kv_prev is not None:
        aliases = {len(args): 1, len(args) + 1: 2}
        in_specs += [pl.BlockSpec(memory_space=pl.ANY)] * 2
        args += list(kv_prev)
    return pl.pallas_call(
        functools.partial(_proj_a_kernel, cw=cw),
        out_shape=(jax.ShapeDtypeStruct((n, d), F32), kv_shape, kv_shape),
        grid=(n // tm,),
        in_specs=in_specs,
        out_specs=(_row_spec(tm, d), kv_spec, kv_spec),
        input_output_aliases=aliases,
        compiler_params=_cparams("parallel"),
        name="proj_a",
    )(*args)


def _proj_b(x, nw, w, wgl, wg2, bg, tm):
    n, d = x.shape
    kd, vd = wg2.shape[1], d
    cw = min(PROJ_COLS, kd)
    return pl.pallas_call(
        functools.partial(_proj_b_kernel, cw=cw),
        out_shape=(jax.ShapeDtypeStruct((n, kd), F32), jax.ShapeDtypeStruct((n, kd), F32),
                   jax.ShapeDtypeStruct((n, vd), BF16), jax.ShapeDtypeStruct((n, vd), F32),
                   jax.ShapeDtypeStruct((n, kd), F32)),
        grid=(n // tm,),
        in_specs=[_row_spec(tm, d), _full_spec((1, d)), _full_spec(w.shape),
                  _full_spec(wgl.shape), _full_spec(wg2.shape), _full_spec((1, kd))],
        out_specs=(_row_spec(tm, kd), _row_spec(tm, kd), _row_spec(tm, vd), _row_spec(tm, vd),
                   _row_spec(tm, kd)),
        compiler_params=_cparams("parallel"),
        name="proj_b",
    )(x, nw, w, wgl, wg2, bg)


def _proj_c(x, nw, w, lb_c, layer, tm):
    n, d = x.shape
    cw = min(PROJ_COLS, d)
    out = jax.ShapeDtypeStruct((n, d), F32)
    return pl.pallas_call(
        functools.partial(_proj_c_kernel, cw=cw, layer=layer),
        out_shape=(out, out, out, jax.ShapeDtypeStruct((n, d), BF16), out),
        grid=(n // tm,),
        in_specs=[_row_spec(tm, d), _full_spec((1, d)), _full_spec(w.shape),
                  _full_spec(lb_c.shape)],
        out_specs=(_row_spec(tm, d),) * 5,
        compiler_params=_cparams("parallel"),
        name="proj_c",
    )(x, nw, w, lb_c)


def _lambda_full(lam_ref, lam_init):
    l = lam_ref[...]
    a = jnp.sum(l[0:1] * l[1:2], axis=-1, keepdims=True)
    b = jnp.sum(l[2:3] * l[3:4], axis=-1, keepdims=True)
    return jnp.exp(a) - jnp.exp(b) + lam_init


def _head_finish(o, subln, lam_init):
    ms = jnp.mean(o * o, axis=-1, keepdims=True)
    return o * lax.rsqrt(ms + RMS_EPS) * subln * (1.0 - lam_init)


def _attn_prompt_kernel(lam_ref, subln_ref, q_ref, k_ref, v_ref, o_ref,
                        kb_ref, vt_ref, vt2_ref, km_ref, vtm_ref, m_ref, acc_ref, bias_ref,
                        *, tq, n_meta, lam_init, hb):
    t_all = q_ref.shape[0]
    nq = (t_all - n_meta) // tq
    hw = 2 * HEAD_A
    lam = _lambda_full(lam_ref, lam_init)
    subln = subln_ref[...]
    first_map = lax.broadcasted_iota(jnp.int32, (1, hw), 1) < HEAD_A

    def stack_q(qt, scale):
        qt = qt * scale
        return jnp.concatenate([jnp.where(first_map, qt, 0.0), jnp.where(first_map, 0.0, qt)],
                               axis=0).astype(BF16)

    r = lax.broadcasted_iota(jnp.int32, (2 * n_meta, n_meta), 0)
    c = lax.broadcasted_iota(jnp.int32, (2 * n_meta, n_meta), 1)
    meta_causal = c <= jnp.where(r >= n_meta, r - n_meta, r)
    pad = jnp.zeros((hw - n_meta, hw), F32)
    for h in range(hb):
        lanes = slice(h * hw, (h + 1) * hw)
        kmeta = k_ref[0:n_meta, lanes]
        vmeta = v_ref[0:n_meta, lanes]
        qs = stack_q(q_ref[0:n_meta, lanes], HEAD_A ** -0.5)
        s = lax.dot_general(qs, kmeta.astype(BF16), NT_DIMS, preferred_element_type=F32)
        s = jnp.where(meta_causal, s, -jnp.inf)
        p = jnp.exp(s - jnp.max(s, axis=-1, keepdims=True))
        acc = jnp.dot(p.astype(BF16), vmeta.astype(BF16), preferred_element_type=F32)
        acc = acc / jnp.sum(p, axis=-1, keepdims=True)
        o = acc[:n_meta] - lam * acc[n_meta:]
        o_ref[0:n_meta, lanes] = _head_finish(o, subln, lam_init).astype(o_ref.dtype)
        km_ref[h] = jnp.concatenate([kmeta, pad], axis=0).astype(BF16)
        vtm_ref[h] = jnp.concatenate(
            [jnp.concatenate([vmeta, pad], axis=0).T.astype(BF16), jnp.ones((ONES_ROWS, hw), BF16)],
            axis=0)
        for cidx in range(nq):
            rows = slice(n_meta + cidx * tq, n_meta + (cidx + 1) * tq)
            kb_ref[h, cidx] = k_ref[rows, lanes].astype(BF16)
            vte = jnp.concatenate([v_ref[rows, lanes].T.astype(BF16),
                                   jnp.ones((ONES_ROWS, tq), BF16)], axis=0)
            vt_ref[h, cidx] = vte
            if cidx < 2 * (nq // 2):
                vt2_ref[h, cidx // 2, :, (cidx % 2) * tq:(cidx % 2 + 1) * tq] = vte

    key_i = lax.broadcasted_iota(jnp.int32, (hw + tq, 2 * tq), 0)
    qry_i = lax.broadcasted_iota(jnp.int32, (hw + tq, 2 * tq), 1)
    key_limit = jnp.where(key_i < hw, n_meta - 1, hw + jnp.where(qry_i >= tq, qry_i - tq, qry_i))
    bias_ref[...] = jnp.where(key_i <= key_limit, 0.0, -jnp.inf)

    def online(h, s, vte):
        m_old = m_ref[h]
        m_new = jnp.maximum(m_old, jnp.max(s, axis=0, keepdims=True))
        alpha = jnp.exp2(m_old - m_new)
        p = jnp.exp2(s - m_new).astype(BF16)
        acc_ref[h] = alpha * acc_ref[h] + jnp.dot(vte, p, preferred_element_type=F32)
        m_ref[h] = m_new

    def q_body(qi, _):
        q0 = pl.multiple_of(n_meta + qi * tq, 16)
        qss = [stack_q(q_ref[pl.ds(q0, tq), h * hw:(h + 1) * hw], (HEAD_A ** -0.5) * LOG2E)
               for h in range(hb)]
        for h in range(hb):
            s = jnp.concatenate(
                [lax.dot_general(km_ref[h], qss[h], NT_DIMS, preferred_element_type=F32),
                 lax.dot_general(kb_ref[h, qi], qss[h], NT_DIMS, preferred_element_type=F32)],
                axis=0)
            s = s + bias_ref[...]
            m = jnp.max(s, axis=0, keepdims=True)
            p = jnp.exp2(s - m).astype(BF16)
            m_ref[h] = m
            acc_ref[h] = (jnp.dot(vtm_ref[h], p[:hw], preferred_element_type=F32)
                          + jnp.dot(vt_ref[h, qi], p[hw:], preferred_element_type=F32))

        def pair_body(j, _):
            for h in range(hb):
                kpair = kb_ref[h, pl.ds(2 * j, 2)].reshape(2 * tq, hw)
                s = lax.dot_general(kpair, qss[h], NT_DIMS, preferred_element_type=F32)
                online(h, s, vt2_ref[h, j])
            return 0

        lax.fori_loop(0, qi // 2, pair_body, 0)

        @pl.when(qi % 2 == 1)
        def _():
            for h in range(hb):
                s = lax.dot_general(kb_ref[h, qi - 1], qss[h], NT_DIMS,
                                    preferred_element_type=F32)
                online(h, s, vt_ref[h, qi - 1])

        for h in range(hb):
            acc = acc_ref[h]
            ot = acc[:hw] * (1.0 / acc[hw:hw + 1])
            od = ot[:, :tq] - lam * ot[:, tq:]
            ms = jnp.mean(od * od, axis=0, keepdims=True)
            od = od * lax.rsqrt(ms + RMS_EPS)
            o_ref[pl.ds(q0, tq), h * hw:(h + 1) * hw] = (
                od.T * (subln * (1.0 - lam_init))).astype(o_ref.dtype)
        return 0

    lax.fori_loop(0, nq, q_body, 0)


def _attn_prompt(q, k, v, layer_j, lam, subln, lam_init, n_meta, hb):
    b, t, d = q.shape
    hw = 2 * HEAD_A
    tq = _pick_tile(t - n_meta, ATTN_Q_ROWS, LANES)
    nq = (t - n_meta) // tq
    blk = pl.BlockSpec((None, t, hb * hw), lambda i, h: (i, 0, h))
    kvblk = pl.BlockSpec((None, None, t, hb * hw), lambda i, h: (layer_j, i, 0, h))
    return pl.pallas_call(
        functools.partial(_attn_prompt_kernel, tq=tq, n_meta=n_meta, lam_init=lam_init, hb=hb),
        out_shape=jax.ShapeDtypeStruct((b, t, d), BF16),
        grid=(b, d // (hb * hw)),
        in_specs=[pl.BlockSpec(lam.shape, lambda i, h: (0, 0)),
                  pl.BlockSpec((1, hw), lambda i, h: (0, 0)), blk, kvblk, kvblk],
        out_specs=blk,
        scratch_shapes=[pltpu.VMEM((hb, nq, tq, hw), BF16),
                        pltpu.VMEM((hb, nq, hw + ONES_ROWS, tq), BF16),
                        pltpu.VMEM((hb, max(nq // 2, 1), hw + ONES_ROWS, 2 * tq), BF16),
                        pltpu.VMEM((hb, hw, hw), BF16), pltpu.VMEM((hb, hw + ONES_ROWS, hw), BF16),
                        pltpu.VMEM((hb, 1, 2 * tq), F32),
                        pltpu.VMEM((hb, hw + ONES_ROWS, 2 * tq), F32),
                        pltpu.VMEM((hw + tq, 2 * tq), F32)],
        compiler_params=_cparams("parallel", "parallel"),
        name="attn_prompt",
    )(lam, subln, q, k, v)


def _attn_sample_kernel(pt_ref, lam_ref, subln_ref, q_ref, kn_ref, vn_ref, *rest,
                        pp, n_new, lam_init):
    del pt_ref
    k_refs, v_refs = rest[:pp], rest[pp:2 * pp]
    o_ref, qbd_ref, m_ref, l_ref, acc_ref = rest[2 * pp:]
    step = pl.program_id(1)
    rows = q_ref.shape[0]
    n_heads = rows // n_new
    n_vheads = n_heads // 2
    page = k_refs[0].shape[1]
    grp = 4 * n_new

    def scores(k_list):
        qbd = qbd_ref[...]
        return jnp.concatenate(
            [jnp.dot(qbd, kr[...].astype(BF16), preferred_element_type=F32) for kr in k_list],
            axis=1)

    def values(p, v_list):
        out = []
        for g in range(rows // grp):
            pg = p[g * grp:(g + 1) * grp]
            halves = []
            for vh in (2 * g, 2 * g + 1):
                r = None
                for i, vr in enumerate(v_list):
                    v = vr[pl.ds(vh, page, stride=n_vheads), :].astype(BF16)
                    t = jnp.dot(pg[:, i * page:(i + 1) * page], v, preferred_element_type=F32)
                    r = t if r is None else r + t
                halves.append(r)
            out.append(halves[0][:grp // 2])
            out.append(halves[1][grp // 2:])
        return jnp.concatenate(out, axis=0)

    @pl.when(step == 0)
    def _():
        d = n_heads * HEAD_A
        q = jnp.tile(q_ref[...] * ((HEAD_A ** -0.5) * LOG2E), (1, n_heads))
        row_head = lax.broadcasted_iota(jnp.int32, (rows, d), 0) // n_new
        lane_head = lax.broadcasted_iota(jnp.int32, (rows, d), 1) // HEAD_A
        qbd_ref[...] = jnp.where(row_head == lane_head, q, 0.0).astype(BF16)
        s = scores([kn_ref])
        tok = lax.broadcasted_iota(jnp.int32, (rows, page), 0) % n_new
        key = lax.broadcasted_iota(jnp.int32, (rows, page), 1)
        s = jnp.where(key <= tok, s, -jnp.inf)
        m = jnp.max(s, axis=-1, keepdims=True)
        p = jnp.exp2(s - m)
        m_ref[...] = m
        l_ref[...] = jnp.sum(p, axis=-1, keepdims=True)
        acc_ref[...] = values(p.astype(BF16), [vn_ref])

    s = scores(k_refs)
    m_old = m_ref[...]
    m_new = jnp.maximum(m_old, jnp.max(s, axis=-1, keepdims=True))
    alpha = jnp.exp2(m_old - m_new)
    p = jnp.exp2(s - m_new)
    l_ref[...] = alpha * l_ref[...] + jnp.sum(p, axis=-1, keepdims=True)
    acc_ref[...] = alpha * acc_ref[...] + values(p.astype(BF16), v_refs)
    m_ref[...] = m_new

    @pl.when(step == pl.num_programs(1) - 1)
    def _():
        lam = _lambda_full(lam_ref, lam_init)
        subln = subln_ref[...]
        hw = 2 * HEAD_A
        full = acc_ref[...] / l_ref[...]
        diff = full - lam * pltpu.roll(full, rows - n_new, 0)
        for h in range(n_heads // 2):
            tile = diff[2 * n_new * h:2 * n_new * (h + 1)]
            o_ref[:, h * hw:(h + 1) * hw] = _head_finish(tile, subln, lam_init).astype(o_ref.dtype)


def _attn_sample(q, k_new, v_new, cache_k, cache_v, page_table, layer_j, lam, subln, lam_init,
                 n_new, pp):
    nb, rows, _ = q.shape
    n_pages = page_table.shape[1]
    krows, page = cache_k.shape[2:]
    vrows = cache_v.shape[2]
    hw = 2 * HEAD_A
    d = (rows // n_new) * HEAD_A
    assert 2 * n_new == SAMPLE_ROWS and krows == d and vrows * 2 * HEAD_A == page * d

    def per_sample(shape):
        return pl.BlockSpec((None,) + shape, lambda b, s, pt: (b, 0, 0))

    def kpage(i):
        return pl.BlockSpec((None, None, krows, page),
                            lambda b, s, pt: (layer_j, pt[b, s * pp + i], 0, 0))

    def vpage(i):
        return pl.BlockSpec((None, None, vrows, hw),
                            lambda b, s, pt: (layer_j, pt[b, s * pp + i], 0, 0))

    grid_spec = pltpu.PrefetchScalarGridSpec(
        num_scalar_prefetch=1,
        grid=(nb, n_pages // pp),
        in_specs=[pl.BlockSpec(lam.shape, lambda b, s, pt: (0, 0)),
                  pl.BlockSpec((1, hw), lambda b, s, pt: (0, 0)),
                  per_sample(q.shape[1:]), per_sample(k_new.shape[1:]),
                  per_sample(v_new.shape[1:])]
                 + [kpage(i) for i in range(pp)] + [vpage(i) for i in range(pp)],
        out_specs=pl.BlockSpec((SAMPLE_ROWS, d), lambda b, s, pt: (b, 0)),
        scratch_shapes=[pltpu.VMEM((rows, d), BF16), pltpu.VMEM((rows, 1), F32),
                        pltpu.VMEM((rows, 1), F32), pltpu.VMEM((rows, hw), F32)],
    )
    return pl.pallas_call(
        functools.partial(_attn_sample_kernel, pp=pp, n_new=n_new, lam_init=lam_init),
        out_shape=jax.ShapeDtypeStruct((nb * SAMPLE_ROWS, d), BF16),
        grid_spec=grid_spec,
        compiler_params=_cparams("parallel", "arbitrary"),
        name="attn_sample",
    )(page_table, lam, subln, q, k_new, v_new, *([cache_k] * pp), *([cache_v] * pp))


def _chunk_constants(length):
    nlev = int(math.log2(length))
    assert 2 ** nlev == length
    w = np.zeros(((2 + nlev) * length, length), np.float32)
    masks = np.zeros((nlev + 1, length, length), np.float32)
    masks[0] = np.eye(length)
    for t in range(length):
        w[t, :t + 1] = 1.0
        w[length + t, t + 1:] = 1.0
    for lvl in range(1, nlev + 1):
        bs, half = 2 ** lvl, 2 ** (lvl - 1)
        for t in range(length):
            mid = t - t % bs + half
            row = (1 + lvl) * length + t
            if t >= mid:
                w[row, mid:t + 1] = 1.0
                masks[lvl, t, mid - half:mid] = 1.0
            else:
                w[row, t + 1:mid] = 1.0
    return jnp.asarray(np.tile(w, (1, 3)), BF16), jnp.asarray(masks, F32)


def _chunk_exponents(g, w3):
    g = g * LOG2E
    g1 = g.astype(BF16)
    rem = g - g1.astype(F32)
    g2 = rem.astype(BF16)
    g3 = (rem - g2.astype(F32)).astype(BF16)
    return jnp.exp2(jnp.dot(w3, jnp.concatenate([g1, g2, g3], axis=0),
                            preferred_element_type=F32))


def _chunk_head(q, k, vb, st, ex, masks):
    length = q.shape[0]
    nlev = masks.shape[0] - 1
    e_cum = ex[0:length]
    e_rem = ex[length:2 * length]
    e_last = e_cum[length - 1:length]
    o = lax.dot_general((q * e_cum).astype(BF16), st.astype(BF16), NT_DIMS,
                        preferred_element_type=F32)
    qb, kb = q.astype(BF16), k.astype(BF16)
    a = masks[0] * lax.dot_general(qb, kb, NT_DIMS, preferred_element_type=F32)
    row = lax.broadcasted_iota(jnp.int32, q.shape, 0)
    for lvl in range(1, nlev + 1):
        second_half = (row & (2 ** lvl - 1)) >= 2 ** (lvl - 1)
        x = (jnp.where(second_half, q, k) * ex[(1 + lvl) * length:(2 + lvl) * length]).astype(BF16)
        a = a + masks[lvl] * lax.dot_general(x, x, NT_DIMS, preferred_element_type=F32)
    o = o + jnp.dot(a.astype(BF16), vb, preferred_element_type=F32)
    kd = (k * e_rem).astype(BF16)
    st = st * e_last + lax.dot_general(vb, kd, TN_DIMS, preferred_element_type=F32)
    return o, st


def _gate_norm(o, r, gn):
    ms = jnp.mean(o * o, axis=-1, keepdims=True)
    return o * lax.rsqrt(ms + RMS_EPS) * gn * (r * _sigmoid(r))


def _recur_prompt_kernel(gn_ref, ws_ref, ms_ref, wm_ref, mm_ref, q_ref, k_ref, g_ref, v_ref, r_ref,
                         o_ref, s_ref, st_ref, *, n_meta, chunk, hb, unroll):
    t_all = q_ref.shape[0]
    dv = v_ref.shape[1] // hb
    gn = gn_ref[...]
    st_ref[...] = jnp.zeros_like(st_ref)

    def run(r0, length, w3, masks):
        rows = pl.ds(r0, length)
        ex = _chunk_exponents(g_ref[rows], w3)
        for h in range(hb):
            ksl = slice(h * DK, (h + 1) * DK)
            vsl = slice(h * dv, (h + 1) * dv)
            o, st = _chunk_head(q_ref[rows, ksl], k_ref[rows, ksl], v_ref[rows, vsl], st_ref[h],
                                ex[:, ksl], masks)
            st_ref[h] = st
            o_ref[rows, vsl] = _gate_norm(o, r_ref[rows, vsl], gn).astype(o_ref.dtype)

    run(0, n_meta, ws_ref[...], ms_ref[...])

    def body(c, _):
        run(pl.multiple_of(n_meta + c * chunk, 16), chunk, wm_ref[...], mm_ref[...])
        return 0

    lax.fori_loop(0, (t_all - n_meta) // chunk, body, 0, unroll=unroll)
    for h in range(hb):
        s_ref[h] = st_ref[h].T


def _recur_prompt(q, k, g, v, r, gn, n_meta, tiling):
    chunk, hb, unroll = tiling
    b, t, kd = q.shape
    vd = v.shape[2]
    nh = kd // DK
    dv = vd // nh
    ws, ms = _chunk_constants(n_meta)
    wm, mm = _chunk_constants(chunk)
    kblk = pl.BlockSpec((None, t, hb * DK), lambda i, h: (i, 0, h))
    vblk = pl.BlockSpec((None, t, hb * dv), lambda i, h: (i, 0, h))

    def const(x):
        return pl.BlockSpec(x.shape, lambda i, h: (0,) * x.ndim)

    return pl.pallas_call(
        functools.partial(_recur_prompt_kernel, n_meta=n_meta, chunk=chunk, hb=hb,
                          unroll=unroll),
        out_shape=(jax.ShapeDtypeStruct((b, t, vd), BF16),
                   jax.ShapeDtypeStruct((b, nh, DK, dv), F32)),
        grid=(b, nh // hb),
        in_specs=[pl.BlockSpec((1, dv), lambda i, h: (0, 0)), const(ws), const(ms), const(wm),
                  const(mm), kblk, kblk, kblk, vblk, vblk],
        out_specs=(vblk, pl.BlockSpec((None, hb, DK, dv), lambda i, h: (i, h, 0, 0))),
        scratch_shapes=[pltpu.VMEM((hb, dv, DK), F32)],
        compiler_params=_cparams("parallel", "parallel"),
        name="recur_prompt",
    )(gn, ws, ms, wm, mm, q, k, g, v, r)


def _recur_sample_kernel(gn_ref, w_ref, m_ref, q_ref, k_ref, g_ref, v_ref, r_ref, s0_ref,
                         o_ref, s_ref, *, n_new):
    nh = s0_ref.shape[0]
    dv = s0_ref.shape[2]
    gn = gn_ref[...]
    masks = m_ref[...]
    pad_rows = SMALL_CHUNK - SAMPLE_ROWS
    kd = q_ref.shape[1]
    valid = lax.broadcasted_iota(jnp.int32, (SAMPLE_ROWS, kd), 0) < n_new

    def padded(x):
        return jnp.concatenate([x, jnp.zeros((pad_rows, x.shape[1]), x.dtype)], axis=0)

    q = padded(q_ref[...])
    k = padded(jnp.where(valid, k_ref[...], 0.0))
    ex = _chunk_exponents(padded(jnp.where(valid, g_ref[...], 0.0)), w_ref[...])
    vb = padded(v_ref[...].astype(F32)).astype(BF16)
    for h in range(nh):
        ksl = slice(h * DK, (h + 1) * DK)
        vsl = slice(h * dv, (h + 1) * dv)
        o, st = _chunk_head(q[:, ksl], k[:, ksl], vb[:, vsl], s0_ref[h].T, ex[:, ksl], masks)
        s_ref[h] = st.T
        o_ref[:, vsl] = _gate_norm(o[0:SAMPLE_ROWS], r_ref[:, vsl], gn).astype(o_ref.dtype)


def _recur_sample(q, k, g, v, r, s0, gn, n_new):
    n, kd = q.shape
    vd = v.shape[1]
    nb, nh, _, dv = s0.shape
    w, masks = _chunk_constants(SMALL_CHUNK)
    kblk = pl.BlockSpec((SAMPLE_ROWS, kd), lambda i: (i, 0))
    vblk = pl.BlockSpec((SAMPLE_ROWS, vd), lambda i: (i, 0))
    sblk = pl.BlockSpec((None, nh, DK, dv), lambda i: (i, 0, 0, 0))
    return pl.pallas_call(
        functools.partial(_recur_sample_kernel, n_new=n_new),
        out_shape=(jax.ShapeDtypeStruct((n, vd), BF16), jax.ShapeDtypeStruct(s0.shape, F32)),
        grid=(nb,),
        in_specs=[_full_spec((1, dv)), _full_spec(w.shape), _full_spec(masks.shape),
                  kblk, kblk, kblk, vblk, vblk, sblk],
        out_specs=(vblk, sblk),
        compiler_params=_cparams("parallel"),
        name="recur_sample",
    )(gn, w, masks, q, k, g, v, r, s0)


def _mix_mlp_kernel(a_ref, wo_ref, x_ref, nw_ref, wu_ref, wd_ref, o_ref, xn_ref):
    j = pl.program_id(1)

    @pl.when(j == 0)
    def _():
        h = x_ref[...] + jnp.dot(a_ref[...], wo_ref[...], preferred_element_type=F32)
        xn_ref[...] = _rms(h, nw_ref[...]).astype(BF16)
        o_ref[...] = h

    u = jnp.dot(xn_ref[...], wu_ref[...].astype(BF16), preferred_element_type=F32)
    u = jnp.square(jnp.maximum(u, 0.0)).astype(BF16)
    o_ref[...] += jnp.dot(u, wd_ref[...].astype(BF16), preferred_element_type=F32)


def _mix_mlp(a, wo, x, nw, wu, wd, layer, tm, tf):
    n, d = x.shape
    ff = wu.shape[2]
    return pl.pallas_call(
        _mix_mlp_kernel,
        out_shape=jax.ShapeDtypeStruct((n, d), F32),
        grid=(n // tm, ff // tf),
        in_specs=[pl.BlockSpec((tm, a.shape[1]), lambda i, j: (i, 0)),
                  pl.BlockSpec(wo.shape, lambda i, j: (0, 0)),
                  pl.BlockSpec((tm, d), lambda i, j: (i, 0)),
                  pl.BlockSpec((1, d), lambda i, j: (0, 0)),
                  pl.BlockSpec((None, d, tf), lambda i, j: (layer, 0, j)),
                  pl.BlockSpec((None, tf, d), lambda i, j: (layer, j, 0))],
        out_specs=pl.BlockSpec((tm, d), lambda i, j: (i, 0)),
        scratch_shapes=[pltpu.VMEM((tm, d), BF16)],
        compiler_params=_cparams("parallel", "arbitrary"),
        name="mix_mlp",
    )(a, wo, x, nw, wu, wd)


def _final_norm_prompt_kernel(x_ref, nw_ref, o_ref, *, n_meta):
    tm = o_ref.shape[0]
    r0 = pl.multiple_of(n_meta + pl.program_id(1) * tm, 8)
    o_ref[...] = _rms(x_ref[pl.ds(r0, tm)], nw_ref[...])


def _final_norm_prompt(x, nw, n_meta):
    b, t, d = x.shape
    seq = t - n_meta
    tm = _pick_tile(seq, 512)
    return pl.pallas_call(
        functools.partial(_final_norm_prompt_kernel, n_meta=n_meta),
        out_shape=jax.ShapeDtypeStruct((b, seq, d), F32),
        grid=(b, seq // tm),
        in_specs=[pl.BlockSpec((None, t, d), lambda i, j: (i, 0, 0)),
                  pl.BlockSpec((1, d), lambda i, j: (0, 0))],
        out_specs=pl.BlockSpec((None, tm, d), lambda i, j: (i, j, 0)),
        compiler_params=_cparams("parallel", "arbitrary"),
        name="final_norm_prompt",
    )(x, nw)


def _final_norm_kernel(x_ref, nw_ref, o_ref):
    o_ref[...] = _rms(x_ref[...], nw_ref[...])


def _final_norm(x, nw):
    return pl.pallas_call(
        _final_norm_kernel,
        out_shape=jax.ShapeDtypeStruct(x.shape, F32),
        name="final_norm",
    )(x, nw)


def _rope_tables(pos):
    half = ROT_DIM // 2
    inv_freq = ROPE_THETA ** (-jnp.arange(half, dtype=F32) / half)
    ang = pos.astype(F32)[:, None] * inv_freq[None, :]
    cos, sin = jnp.cos(ang), jnp.sin(ang)
    n = pos.shape[0]
    zeros = jnp.zeros((n, half), F32)
    rest0 = jnp.zeros((n, HEAD_A - ROT_DIM), F32)
    cos_h = jnp.concatenate([cos, cos, jnp.ones((n, HEAD_A - ROT_DIM), F32)], axis=1)
    sin_m = jnp.concatenate([-sin, zeros, rest0], axis=1)
    sin_p = jnp.concatenate([zeros, sin, rest0], axis=1)
    return tuple(jnp.concatenate([t, t], axis=1) for t in (cos_h, sin_m, sin_p))


def kernel(x_prompt, x_sample, cache_k, cache_v, state_gla, state_hgrn, page_table, meta_tokens,
           norm_mix, norm_ffn, norm_final, w_in_a, lam_a, subln_a, w_out_a, w_in_b, w_gate2_b,
           b_gate_b, gnorm_b, w_out_b, w_in_c, lb_c, gnorm_c, w_out_c, w_up, w_down):
    bp, seq, d = x_prompt.shape
    db, n_new = x_sample.shape[:2]
    depth = norm_mix.shape[0]
    n_pages = page_table.shape[1]
    page = cache_k.shape[2]
    past_len = n_pages * page
    t_all = N_META_TOKENS + seq
    n_main = bp * t_all
    n_small = db * SAMPLE_ROWS
    h_a = d // (2 * HEAD_A)
    kd_b = w_gate2_b.shape[2]
    dv_b = d // H_B
    h_c = d // DK
    dv_c = d // h_c

    hp = jnp.concatenate(
        [jnp.broadcast_to(meta_tokens.astype(F32)[None], (bp, N_META_TOKENS, d)), x_prompt],
        axis=1).reshape(n_main, d)
    hs = jnp.pad(x_sample, ((0, 0), (0, SAMPLE_ROWS - n_new), (0, 0))).reshape(n_small, d)

    tm_main = _pick_tile(t_all, PROJ_ROWS)
    tm_mlp = _pick_tile(n_main, MLP_ROWS)
    tf = MLP_HIDDEN
    tab_p = _rope_tables(jnp.arange(t_all))
    tab_s = _rope_tables(jnp.tile(past_len + jnp.arange(SAMPLE_ROWS), db))
    ck = cache_k.transpose(0, 1, 3, 4, 2).reshape(cache_k.shape[0], cache_k.shape[1], d, page)
    cv = cache_v.reshape(cache_v.shape[0], cache_v.shape[1], page * h_a, 2 * HEAD_A)
    pp = PAGES_PER_STEP if n_pages % PAGES_PER_STEP == 0 else 1

    n_a = w_in_a.shape[0]
    kv_prompt = kv_sample = None
    gla_p, gla_s, hgrn_p, hgrn_s = [], [], [], []
    for i in range(depth):
        kind, j = i % N_MIXERS, i // N_MIXERS
        nw = norm_mix[i].reshape(1, d)
        if kind == 0:
            lam_init = 0.8 - 0.6 * math.exp(-0.3 * i)
            w = w_in_a[j].astype(BF16)
            lam = lam_a[j].astype(F32)
            subln = subln_a[j].reshape(1, 2 * HEAD_A)
            qp, *kv_prompt = _proj_a(hp, nw, w, tab_p, tm_main, t_all // tm_main, n_a, j,
                                     kv_prompt)
            qs, *kv_sample = _proj_a(hs, nw, w, tab_s, n_small, 1, n_a, j, kv_sample)
            ks, vs = kv_sample[0][j], kv_sample[1][j]
            shp = (n_a, bp, t_all, d)
            ap = _attn_prompt(qp.reshape(shp[1:]), kv_prompt[0].reshape(shp),
                              kv_prompt[1].reshape(shp), j, lam, subln, lam_init,
                              N_META_TOKENS, ATTN_HEADS).reshape(n_main, d)
            ks4 = ks.reshape(db, SAMPLE_ROWS, 2 * h_a, HEAD_A)
            vs4 = vs.reshape(db, SAMPLE_ROWS, h_a, 2 * HEAD_A)
            q_rows = qs.reshape(db, SAMPLE_ROWS, 2 * h_a, HEAD_A)[:, :n_new]
            q_rows = q_rows.transpose(0, 2, 1, 3).reshape(db, 2 * h_a * n_new, HEAD_A)
            k_rows = jnp.pad(ks.reshape(db, SAMPLE_ROWS, d).transpose(0, 2, 1),
                             ((0, 0), (0, 0), (0, page - SAMPLE_ROWS)))
            v_rows = jnp.pad(vs4.reshape(db, SAMPLE_ROWS * h_a, 2 * HEAD_A),
                             ((0, 0), (0, (page - SAMPLE_ROWS) * h_a), (0, 0)))
            as_ = _attn_sample(q_rows, k_rows, v_rows, ck, cv, page_table, j, lam, subln,
                               lam_init, n_new, pp)
            w_out = w_out_a[j]
        elif kind == 1:
            w = w_in_b[j]
            n_main_cols = 2 * kd_b + 2 * d
            w_main = w[:, :n_main_cols].astype(BF16)
            w_gl = jnp.pad(w[:, n_main_cols:], ((0, 0), (0, LANES - GATE_RANK))).astype(BF16)
            w_g2 = jnp.pad(w_gate2_b[j], ((0, LANES - GATE_RANK), (0, 0))).astype(BF16)
            bg = b_gate_b[j].reshape(1, kd_b)
            gn = gnorm_b[j].reshape(1, dv_b)
            qp, kp, vp, rp, gp = _proj_b(hp, nw, w_main, w_gl, w_g2, bg, tm_main)
            qs, ks, vs, rs, gs = _proj_b(hs, nw, w_main, w_gl, w_g2, bg, n_small)
            r3 = lambda a: a.reshape(bp, t_all, a.shape[1])
            ap, sp = _recur_prompt(r3(qp), r3(kp), r3(gp), r3(vp), r3(rp), gn, N_META_TOKENS,
                                   GLA_TILING)
            ap = ap.reshape(n_main, d)
            as_, ss = _recur_sample(qs, ks, gs, vs, rs, state_gla[j].astype(F32), gn, n_new)
            gla_p.append(sp)
            gla_s.append(ss)
            w_out = w_out_b[j]
        else:
            w = w_in_c[j].astype(BF16)
            gn = gnorm_c[j].reshape(1, dv_c)
            lbf = lb_c.astype(F32)
            qp, kp, gp, vp, rp = _proj_c(hp, nw, w, lbf, i, tm_main)
            qs, ks, gs, vs, rs = _proj_c(hs, nw, w, lbf, i, n_small)
            r3 = lambda a: a.reshape(bp, t_all, a.shape[1])
            ap, sp = _recur_prompt(r3(qp), r3(kp), r3(gp), r3(vp), r3(rp), gn, N_META_TOKENS,
                                   HGRN_TILING)
            ap = ap.reshape(n_main, d)
            as_, ss = _recur_sample(qs, ks, gs, vs, rs, state_hgrn[j].astype(F32), gn, n_new)
            hgrn_p.append(sp)
            hgrn_s.append(ss)
            w_out = w_out_c[j]
        w_out = w_out.astype(BF16)
        nf = norm_ffn[i].reshape(1, d)
        hp = _mix_mlp(ap, w_out, hp, nf, w_up, w_down, i, tm_mlp, tf)
        hs = _mix_mlp(as_, w_out, hs, nf, w_up, w_down, i, n_small, tf)

    nfin = norm_final.reshape(1, d)
    y_prompt = _final_norm_prompt(hp.reshape(bp, t_all, d), nfin, N_META_TOKENS)
    y_sample = _final_norm(hs, nfin).reshape(db, SAMPLE_ROWS, d)[:, :n_new]
    k_sample, v_sample = (a.reshape(n_a, db, SAMPLE_ROWS, d)[:, :, :n_new] for a in kv_sample)
    return (y_prompt, y_sample,
            kv_prompt[0].reshape(n_a, bp, t_all, 2 * h_a, HEAD_A),
            kv_prompt[1].reshape(n_a, bp, t_all, h_a, 2 * HEAD_A),
            k_sample.reshape(n_a, db, n_new, 2 * h_a, HEAD_A),
            v_sample.reshape(n_a, db, n_new, h_a, 2 * HEAD_A),
            jnp.stack(gla_p), jnp.stack(gla_s), jnp.stack(hgrn_p), jnp.stack(hgrn_s))
```

```python
import functools
import math

import numpy as np
import jax
import jax.numpy as jnp
from jax import lax
from jax.experimental import pallas as pl
from jax.experimental.pallas import tpu as pltpu

F32 = jnp.float32
BF16 = jnp.bfloat16

RMS_EPS = 1e-6
N_META_TOKENS = 16
N_MIXERS = 3
HEAD_A = 64
ROT_DIM = HEAD_A // 4
ROPE_THETA = 500000.0
H_B = 4
GATE_RANK = 16
GATE_TEMP = 16.0
DK = 128
SAMPLE_ROWS = 8
SMALL_CHUNK = 16
GLA_TILING = (64, 2, 8)
HGRN_TILING = (128, 4, 4)
PROJ_ROWS = 704
PROJ_COLS = 512
MLP_ROWS = 1400
MLP_HIDDEN = 512
ATTN_Q_ROWS = 512
ATTN_HEADS = 4
PAGES_PER_STEP = 16
LANES = 128
ONES_ROWS = 16
VMEM_LIMIT = 56 * 1024 * 1024
LOG2E = 1.4426950408889634

NT_DIMS = (((1,), (1,)), ((), ()))
TN_DIMS = (((0,), (0,)), ((), ()))


def _cparams(*sem):
    return pltpu.CompilerParams(dimension_semantics=sem, vmem_limit_bytes=VMEM_LIMIT)


def _pick_tile(n, target, mult=16):
    best = None
    for t in range(mult, min(n, target) + 1, mult):
        if n % t == 0:
            best = t
    return n if best is None else best


def _rms(x, w):
    ms = jnp.mean(x * x, axis=-1, keepdims=True)
    return x * lax.rsqrt(ms + RMS_EPS) * w


def _sigmoid(x):
    return 1.0 / (1.0 + jnp.exp(-x))


def _proj_a_kernel(x_ref, nw_ref, w_ref, c_ref, sm_ref, sp_ref, *rest, cw):
    q_ref, k_ref, v_ref = rest[-3:]
    xn = _rms(x_ref[...], nw_ref[...]).astype(BF16)
    d = q_ref.shape[1]
    rep = cw // LANES
    cos = jnp.tile(c_ref[...], (1, rep))
    sin_m = jnp.tile(sm_ref[...], (1, rep))
    sin_p = jnp.tile(sp_ref[...], (1, rep))
    for dst_i, dst in enumerate((q_ref, k_ref, v_ref)):
        for c in range(d // cw):
            col = dst_i * d + c * cw
            y = jnp.dot(xn, w_ref[:, col:col + cw], preferred_element_type=F32)
            if dst_i < 2:
                y = (y * cos + pltpu.roll(y, ROT_DIM // 2, 1) * sin_p
                     + pltpu.roll(y, cw - ROT_DIM // 2, 1) * sin_m)
            dst[:, c * cw:(c + 1) * cw] = y


def _proj_b_kernel(x_ref, nw_ref, w_ref, wgl_ref, wg2_ref, bg_ref,
                   q_ref, k_ref, v_ref, r_ref, g_ref, *, cw):
    xn = _rms(x_ref[...], nw_ref[...]).astype(BF16)
    col = 0
    for dst, scale in ((q_ref, DK ** -0.5), (k_ref, None), (v_ref, None), (r_ref, None)):
        for c in range(dst.shape[1] // cw):
            y = jnp.dot(xn, w_ref[:, col:col + cw], preferred_element_type=F32)
            if scale is not None:
                y = y * scale
            dst[:, c * cw:(c + 1) * cw] = y.astype(dst.dtype)
            col += cw
    gl = jnp.dot(xn, wgl_ref[...], preferred_element_type=F32).astype(BF16)
    z = jnp.dot(gl, wg2_ref[...], preferred_element_type=F32) + bg_ref[...]
    g_ref[...] = (jnp.minimum(z, 0.0) - jnp.log(1.0 + jnp.exp(-jnp.abs(z)))) * (1.0 / GATE_TEMP)


def _proj_c_kernel(x_ref, nw_ref, w_ref, lb_ref, q_ref, k_ref, g_ref, v_ref, r_ref, *, cw, layer):
    xn = _rms(x_ref[...], nw_ref[...]).astype(BF16)
    d = q_ref.shape[1]
    lbs = lb_ref[...]
    mx = jnp.max(lbs, axis=0, keepdims=True)
    e = jnp.exp(lbs - mx)
    lb = jnp.sum(e[1:layer + 1], axis=0, keepdims=True) / jnp.sum(e, axis=0, keepdims=True)
    for c in range(d // cw):
        sl = slice(c * cw, (c + 1) * cw)
        y = jnp.dot(xn, w_ref[:, c * cw:(c + 1) * cw], preferred_element_type=F32)
        q_ref[:, sl] = y * _sigmoid(y) * (DK ** -0.5)
        f = jnp.dot(xn, w_ref[:, d + c * cw:d + (c + 1) * cw], preferred_element_type=F32)
        lbc = lb[:, sl]
        forget = lbc + (1.0 - lbc) * _sigmoid(f)
        k_ref[:, sl] = 1.0 - forget
        g_ref[:, sl] = jnp.log(forget)
        v_ref[:, sl] = jnp.dot(xn, w_ref[:, 2 * d + c * cw:2 * d + (c + 1) * cw],
                               preferred_element_type=F32).astype(v_ref.dtype)
        r_ref[:, sl] = jnp.dot(xn, w_ref[:, 3 * d + c * cw:3 * d + (c + 1) * cw],
                               preferred_element_type=F32)


def _row_spec(tm, width):
    return pl.BlockSpec((tm, width), lambda i: (i, 0))


def _full_spec(shape):
    return pl.BlockSpec(shape, lambda i: (0,) * len(shape))


def _proj_a(x, nw, w, tables, tm, table_blocks, n_layers, layer_j, kv_prev):
    n, d = x.shape
    cw = min(PROJ_COLS, d)
    tab_spec = pl.BlockSpec((tm, LANES), lambda i: (i % table_blocks, 0))
    kv_shape = jax.ShapeDtypeStruct((n_layers, n, d), F32)
    kv_spec = pl.BlockSpec((None, tm, d), lambda i: (layer_j, i, 0))
    in_specs = [_row_spec(tm, d), _full_spec((1, d)), _full_spec(w.shape),
                tab_spec, tab_spec, tab_spec]
    args = [x, nw, w, *tables]
    aliases = {}
    if kv_prev is not None:
        aliases = {len(args): 1, len(args) + 1: 2}
        in_specs += [pl.BlockSpec(memory_space=pl.ANY)] * 2
        args += list(kv_prev)
    return pl.pallas_call(
        functools.partial(_proj_a_kernel, cw=cw),
        out_shape=(jax.ShapeDtypeStruct((n, d), F32), kv_shape, kv_shape),
        grid=(n // tm,),
        in_specs=in_specs,
        out_specs=(_row_spec(tm, d), kv_spec, kv_spec),
        input_output_aliases=aliases,
        compiler_params=_cparams("parallel"),
        name="proj_a",
    )(*args)


def _proj_b(x, nw, w, wgl, wg2, bg, tm):
    n, d = x.shape
    kd, vd = wg2.shape[1], d
    cw = min(PROJ_COLS, kd)
    return pl.pallas_call(
        functools.partial(_proj_b_kernel, cw=cw),
        out_shape=(jax.ShapeDtypeStruct((n, kd), F32), jax.ShapeDtypeStruct((n, kd), F32),
                   jax.ShapeDtypeStruct((n, vd), BF16), jax.ShapeDtypeStruct((n, vd), F32),
                   jax.ShapeDtypeStruct((n, kd), F32)),
        grid=(n // tm,),
        in_specs=[_row_spec(tm, d), _full_spec((1, d)), _full_spec(w.shape),
                  _full_spec(wgl.shape), _full_spec(wg2.shape), _full_spec((1, kd))],
        out_specs=(_row_spec(tm, kd), _row_spec(tm, kd), _row_spec(tm, vd), _row_spec(tm, vd),
                   _row_spec(tm, kd)),
        compiler_params=_cparams("parallel"),
        name="proj_b",
    )(x, nw, w, wgl, wg2, bg)


def _proj_c(x, nw, w, lb_c, layer, tm):
    n, d = x.shape
    cw = min(PROJ_COLS, d)
    out = jax.ShapeDtypeStruct((n, d), F32)
    return pl.pallas_call(
        functools.partial(_proj_c_kernel, cw=cw, layer=layer),
        out_shape=(out, out, out, jax.ShapeDtypeStruct((n, d), BF16), out),
        grid=(n // tm,),
        in_specs=[_row_spec(tm, d), _full_spec((1, d)), _full_spec(w.shape),
                  _full_spec(lb_c.shape)],
        out_specs=(_row_spec(tm, d),) * 5,
        compiler_params=_cparams("parallel"),
        name="proj_c",
    )(x, nw, w, lb_c)


def _lambda_full(lam_ref, lam_init):
    l = lam_ref[...]
    a = jnp.sum(l[0:1] * l[1:2], axis=-1, keepdims=True)
    b = jnp.sum(l[2:3] * l[3:4], axis=-1, keepdims=True)
    return jnp.exp(a) - jnp.exp(b) + lam_init


def _head_finish(o, subln, lam_init):
    ms = jnp.mean(o * o, axis=-1, keepdims=True)
    return o * lax.rsqrt(ms + RMS_EPS) * subln * (1.0 - lam_init)


def _attn_prompt_kernel(lam_ref, subln_ref, q_ref, k_ref, v_ref, o_ref,
                        kb_ref, vt_ref, vt2_ref, km_ref, vtm_ref, m_ref, acc_ref, bias_ref,
                        *, tq, n_meta, lam_init, hb):
    t_all = q_ref.shape[0]
    nq = (t_all - n_meta) // tq
    hw = 2 * HEAD_A
    lam = _lambda_full(lam_ref, lam_init)
    subln = subln_ref[...]
    first_map = lax.broadcasted_iota(jnp.int32, (1, hw), 1) < HEAD_A

    def stack_q(qt, scale):
        qt = qt * scale
        return jnp.concatenate([jnp.where(first_map, qt, 0.0), jnp.where(first_map, 0.0, qt)],
                               axis=0).astype(BF16)

    r = lax.broadcasted_iota(jnp.int32, (2 * n_meta, n_meta), 0)
    c = lax.broadcasted_iota(jnp.int32, (2 * n_meta, n_meta), 1)
    meta_causal = c <= jnp.where(r >= n_meta, r - n_meta, r)
    pad = jnp.zeros((hw - n_meta, hw), F32)
    for h in range(hb):
        lanes = slice(h * hw, (h + 1) * hw)
        kmeta = k_ref[0:n_meta, lanes]
        vmeta = v_ref[0:n_meta, lanes]
        qs = stack_q(q_ref[0:n_meta, lanes], HEAD_A ** -0.5)
        s = lax.dot_general(qs, kmeta.astype(BF16), NT_DIMS, preferred_element_type=F32)
        s = jnp.where(meta_causal, s, -jnp.inf)
        p = jnp.exp(s - jnp.max(s, axis=-1, keepdims=True))
        acc = jnp.dot(p.astype(BF16), vmeta.astype(BF16), preferred_element_type=F32)
        acc = acc / jnp.sum(p, axis=-1, keepdims=True)
        o = acc[:n_meta] - lam * acc[n_meta:]
        o_ref[0:n_meta, lanes] = _head_finish(o, subln, lam_init).astype(o_ref.dtype)
        km_ref[h] = kmeta.astype(BF16)
        vmeta_t = jnp.concatenate([vmeta, pad], axis=0).T[:, :n_meta]
        vtm_ref[h] = jnp.concatenate(
            [vmeta_t.astype(BF16), jnp.ones((ONES_ROWS, n_meta), BF16)], axis=0)
        for cidx in range(nq):
            rows = slice(n_meta + cidx * tq, n_meta + (cidx + 1) * tq)
            kb_ref[h, cidx] = k_ref[rows, lanes].astype(BF16)
            vte = jnp.concatenate([v_ref[rows, lanes].T.astype(BF16),
                                   jnp.ones((ONES_ROWS, tq), BF16)], axis=0)
            vt_ref[h, cidx] = vte
            if cidx < 2 * (nq // 2):
                vt2_ref[h, cidx // 2, :, (cidx % 2) * tq:(cidx % 2 + 1) * tq] = vte

    key_i = lax.broadcasted_iota(jnp.int32, (n_meta + tq, 2 * tq), 0)
    qry_i = lax.broadcasted_iota(jnp.int32, (n_meta + tq, 2 * tq), 1)
    key_limit = jnp.where(key_i < n_meta, n_meta,
                          n_meta + jnp.where(qry_i >= tq, qry_i - tq, qry_i))
    bias_ref[...] = jnp.where(key_i <= key_limit, 0.0, -jnp.inf)

    def online(h, s, vte):
        m_old = m_ref[h]
        m_new = jnp.maximum(m_old, jnp.max(s, axis=0, keepdims=True))
        alpha = jnp.exp2(m_old - m_new)
        p = jnp.exp2(s - m_new).astype(BF16)
        acc_ref[h] = alpha * acc_ref[h] + jnp.dot(vte, p, preferred_element_type=F32)
        m_ref[h] = m_new

    def q_body(qi, _):
        q0 = pl.multiple_of(n_meta + qi * tq, 16)
        qss = [stack_q(q_ref[pl.ds(q0, tq), h * hw:(h + 1) * hw], (HEAD_A ** -0.5) * LOG2E)
               for h in range(hb)]
        for h in range(hb):
            s = jnp.concatenate(
                [lax.dot_general(km_ref[h], qss[h], NT_DIMS, preferred_element_type=F32),
                 lax.dot_general(kb_ref[h, qi], qss[h], NT_DIMS, preferred_element_type=F32)],
                axis=0)
            s = s + bias_ref[...]
            m = jnp.max(s, axis=0, keepdims=True)
            p = jnp.exp2(s - m).astype(BF16)
            m_ref[h] = m
            acc_ref[h] = (jnp.dot(vtm_ref[h], p[:n_meta], preferred_element_type=F32)
                          + jnp.dot(vt_ref[h, qi], p[n_meta:], preferred_element_type=F32))

        def pair_body(j, _):
            for h in range(hb):
                kpair = kb_ref[h, pl.ds(2 * j, 2)].reshape(2 * tq, hw)
                s = lax.dot_general(kpair, qss[h], NT_DIMS, preferred_element_type=F32)
                online(h, s, vt2_ref[h, j])
            return 0

        lax.fori_loop(0, qi // 2, pair_body, 0)

        @pl.when(qi % 2 == 1)
        def _():
            for h in range(hb):
                s = lax.dot_general(kb_ref[h, qi - 1], qss[h], NT_DIMS,
                                    preferred_element_type=F32)
                online(h, s, vt_ref[h, qi - 1])

        for h in range(hb):
            acc = acc_ref[h]
            ot = acc[:hw] * (1.0 / acc[hw:hw + 1])
            od = ot[:, :tq] - lam * ot[:, tq:]
            ms = jnp.mean(od * od, axis=0, keepdims=True)
            od = od * lax.rsqrt(ms + RMS_EPS)
            o_ref[pl.ds(q0, tq), h * hw:(h + 1) * hw] = (
                od.T * (subln * (1.0 - lam_init))).astype(o_ref.dtype)
        return 0

    lax.fori_loop(0, nq, q_body, 0)


def _attn_prompt(q, k, v, layer_j, lam, subln, lam_init, n_meta, hb):
    b, t, d = q.shape
    hw = 2 * HEAD_A
    tq = _pick_tile(t - n_meta, ATTN_Q_ROWS, LANES)
    nq = (t - n_meta) // tq
    blk = pl.BlockSpec((None, t, hb * hw), lambda i, h: (i, 0, h))
    kvblk = pl.BlockSpec((None, None, t, hb * hw), lambda i, h: (layer_j, i, 0, h))
    return pl.pallas_call(
        functools.partial(_attn_prompt_kernel, tq=tq, n_meta=n_meta, lam_init=lam_init, hb=hb),
        out_shape=jax.ShapeDtypeStruct((b, t, d), BF16),
        grid=(b, d // (hb * hw)),
        in_specs=[pl.BlockSpec(lam.shape, lambda i, h: (0, 0)),
                  pl.BlockSpec((1, hw), lambda i, h: (0, 0)), blk, kvblk, kvblk],
        out_specs=blk,
        scratch_shapes=[pltpu.VMEM((hb, nq, tq, hw), BF16),
                        pltpu.VMEM((hb, nq, hw + ONES_ROWS, tq), BF16),
                        pltpu.VMEM((hb, max(nq // 2, 1), hw + ONES_ROWS, 2 * tq), BF16),
                        pltpu.VMEM((hb, n_meta, hw), BF16),
                        pltpu.VMEM((hb, hw + ONES_ROWS, n_meta), BF16),
                        pltpu.VMEM((hb, 1, 2 * tq), F32),
                        pltpu.VMEM((hb, hw + ONES_ROWS, 2 * tq), F32),
                        pltpu.VMEM((n_meta + tq, 2 * tq), F32)],
        compiler_params=_cparams("parallel", "parallel"),
        name="attn_prompt",
    )(lam, subln, q, k, v)


def _attn_sample_kernel(pt_ref, lam_ref, subln_ref, q_ref, kn_ref, vn_ref, *rest,
                        pp, n_new, lam_init):
    del pt_ref
    k_refs, v_refs = rest[:pp], rest[pp:2 * pp]
    o_ref, qbd_ref, m_ref, l_ref, acc_ref = rest[2 * pp:]
    step = pl.program_id(1)
    rows = q_ref.shape[0]
    n_heads = rows // n_new
    n_vheads = n_heads // 2
    page = k_refs[0].shape[1]
    grp = 4 * n_new

    def scores(k_list):
        qbd = qbd_ref[...]
        return jnp.concatenate(
            [jnp.dot(qbd, kr[...].astype(BF16), preferred_element_type=F32) for kr in k_list],
            axis=1)

    def values(p, v_list):
        out = []
        for g in range(rows // grp):
            pg = p[g * grp:(g + 1) * grp]
            halves = []
            for vh in (2 * g, 2 * g + 1):
                r = None
                for i, vr in enumerate(v_list):
                    tokens = vr.shape[0] // n_vheads
                    v = vr[pl.ds(vh, tokens, stride=n_vheads), :].astype(BF16)
                    t = jnp.dot(pg[:, i * page:i * page + tokens], v,
                                preferred_element_type=F32)
                    r = t if r is None else r + t
                halves.append(r)
            out.append(halves[0][:grp // 2])
            out.append(halves[1][grp // 2:])
        return jnp.concatenate(out, axis=0)

    @pl.when(step == 0)
    def _():
        d = n_heads * HEAD_A
        q = jnp.tile(q_ref[...] * ((HEAD_A ** -0.5) * LOG2E), (1, n_heads))
        row_head = lax.broadcasted_iota(jnp.int32, (rows, d), 0) // n_new
        lane_head = lax.broadcasted_iota(jnp.int32, (rows, d), 1) // HEAD_A
        qbd_ref[...] = jnp.where(row_head == lane_head, q, 0.0).astype(BF16)
        s = scores([kn_ref])
        tok = lax.broadcasted_iota(jnp.int32, (rows, page), 0) % n_new
        key = lax.broadcasted_iota(jnp.int32, (rows, page), 1)
        s = jnp.where(key <= tok, s, -jnp.inf)
        m = jnp.max(s, axis=-1, keepdims=True)
        p = jnp.exp2(s - m)
        m_ref[...] = m
        l_ref[...] = jnp.sum(p, axis=-1, keepdims=True)
        acc_ref[...] = values(p.astype(BF16), [vn_ref])

    s = scores(k_refs)
    m_old = m_ref[...]
    m_new = jnp.maximum(m_old, jnp.max(s, axis=-1, keepdims=True))
    alpha = jnp.exp2(m_old - m_new)
    p = jnp.exp2(s - m_new)
    l_ref[...] = alpha * l_ref[...] + jnp.sum(p, axis=-1, keepdims=True)
    acc_ref[...] = alpha * acc_ref[...] + values(p.astype(BF16), v_refs)
    m_ref[...] = m_new

    @pl.when(step == pl.num_programs(1) - 1)
    def _():
        lam = _lambda_full(lam_ref, lam_init)
        subln = subln_ref[...]
        hw = 2 * HEAD_A
        full = acc_ref[...] / l_ref[...]
        diff = full - lam * pltpu.roll(full, rows - n_new, 0)
        for h in range(n_heads // 2):
            tile = diff[2 * n_new * h:2 * n_new * (h + 1)]
            o_ref[:, h * hw:(h + 1) * hw] = _head_finish(tile, subln, lam_init).astype(o_ref.dtype)


def _attn_sample(q, k_new, v_new, cache_k, cache_v, page_table, layer_j, lam, subln, lam_init,
                 n_new, pp):
    nb, rows, _ = q.shape
    n_pages = page_table.shape[1]
    krows, page = cache_k.shape[2:]
    vrows = cache_v.shape[2]
    hw = 2 * HEAD_A
    d = (rows // n_new) * HEAD_A
    assert 2 * n_new == SAMPLE_ROWS and krows == d and vrows * 2 * HEAD_A == page * d

    def per_sample(shape):
        return pl.BlockSpec((None,) + shape, lambda b, s, pt: (b, 0, 0))

    def kpage(i):
        return pl.BlockSpec((None, None, krows, page),
                            lambda b, s, pt: (layer_j, pt[b, s * pp + i], 0, 0))

    def vpage(i):
        return pl.BlockSpec((None, None, vrows, hw),
                            lambda b, s, pt: (layer_j, pt[b, s * pp + i], 0, 0))

    grid_spec = pltpu.PrefetchScalarGridSpec(
        num_scalar_prefetch=1,
        grid=(nb, n_pages // pp),
        in_specs=[pl.BlockSpec(lam.shape, lambda b, s, pt: (0, 0)),
                  pl.BlockSpec((1, hw), lambda b, s, pt: (0, 0)),
                  per_sample(q.shape[1:]), per_sample(k_new.shape[1:]),
                  per_sample(v_new.shape[1:])]
                 + [kpage(i) for i in range(pp)] + [vpage(i) for i in range(pp)],
        out_specs=pl.BlockSpec((SAMPLE_ROWS, d), lambda b, s, pt: (b, 0)),
        scratch_shapes=[pltpu.VMEM((rows, d), BF16), pltpu.VMEM((rows, 1), F32),
                        pltpu.VMEM((rows, 1), F32), pltpu.VMEM((rows, hw), F32)],
    )
    return pl.pallas_call(
        functools.partial(_attn_sample_kernel, pp=pp, n_new=n_new, lam_init=lam_init),
        out_shape=jax.ShapeDtypeStruct((nb * SAMPLE_ROWS, d), BF16),
        grid_spec=grid_spec,
        compiler_params=_cparams("parallel", "arbitrary"),
        name="attn_sample",
    )(page_table, lam, subln, q, k_new, v_new, *([cache_k] * pp), *([cache_v] * pp))


def _chunk_constants(length):
    nlev = int(math.log2(length))
    assert 2 ** nlev == length
    w = np.zeros(((2 + nlev) * length, length), np.float32)
    masks = np.zeros((nlev + 1, length, length), np.float32)
    masks[0] = np.eye(length)
    for t in range(length):
        w[t, :t + 1] = 1.0
        w[length + t, t + 1:] = 1.0
    for lvl in range(1, nlev + 1):
        bs, half = 2 ** lvl, 2 ** (lvl - 1)
        for t in range(length):
            mid = t - t % bs + half
            row = (1 + lvl) * length + t
            if t >= mid:
                w[row, mid:t + 1] = 1.0
                masks[lvl, t, mid - half:mid] = 1.0
            else:
                w[row, t + 1:mid] = 1.0
    return jnp.asarray(np.tile(w, (1, 3)), BF16), jnp.asarray(masks, F32)


def _chunk_exponents(g, w3):
    g = g * LOG2E
    g1 = g.astype(BF16)
    rem = g - g1.astype(F32)
    g2 = rem.astype(BF16)
    g3 = (rem - g2.astype(F32)).astype(BF16)
    return jnp.exp2(jnp.dot(w3, jnp.concatenate([g1, g2, g3], axis=0),
                            preferred_element_type=F32))


def _chunk_head(q, k, vb, st, ex, masks):
    length = q.shape[0]
    nlev = masks.shape[0] - 1
    e_cum = ex[0:length]
    e_rem = ex[length:2 * length]
    e_last = e_cum[length - 1:length]
    o = lax.dot_general((q * e_cum).astype(BF16), st.astype(BF16), NT_DIMS,
                        preferred_element_type=F32)
    qb, kb = q.astype(BF16), k.astype(BF16)
    a = masks[0] * lax.dot_general(qb, kb, NT_DIMS, preferred_element_type=F32)
    row = lax.broadcasted_iota(jnp.int32, q.shape, 0)
    for lvl in range(1, nlev + 1):
        second_half = (row & (2 ** lvl - 1)) >= 2 ** (lvl - 1)
        x = (jnp.where(second_half, q, k) * ex[(1 + lvl) * length:(2 + lvl) * length]).astype(BF16)
        a = a + masks[lvl] * lax.dot_general(x, x, NT_DIMS, preferred_element_type=F32)
    o = o + jnp.dot(a.astype(BF16), vb, preferred_element_type=F32)
    kd = (k * e_rem).astype(BF16)
    st = st * e_last + lax.dot_general(vb, kd, TN_DIMS, preferred_element_type=F32)
    return o, st


def _gate_norm(o, r, gn):
    ms = jnp.mean(o * o, axis=-1, keepdims=True)
    return o * lax.rsqrt(ms + RMS_EPS) * gn * (r * _sigmoid(r))


def _recur_prompt_kernel(gn_ref, ws_ref, ms_ref, wm_ref, mm_ref, q_ref, k_ref, g_ref, v_ref, r_ref,
                         o_ref, s_ref, st_ref, *, n_meta, chunk, hb, unroll):
    t_all = q_ref.shape[0]
    dv = v_ref.shape[1] // hb
    gn = gn_ref[...]
    st_ref[...] = jnp.zeros_like(st_ref)

    def run(r0, length, w3, masks):
        rows = pl.ds(r0, length)
        ex = _chunk_exponents(g_ref[rows], w3)
        for h in range(hb):
            ksl = slice(h * DK, (h + 1) * DK)
            vsl = slice(h * dv, (h + 1) * dv)
            o, st = _chunk_head(q_ref[rows, ksl], k_ref[rows, ksl], v_ref[rows, vsl], st_ref[h],
                                ex[:, ksl], masks)
            st_ref[h] = st
            o_ref[rows, vsl] = _gate_norm(o, r_ref[rows, vsl], gn).astype(o_ref.dtype)

    run(0, n_meta, ws_ref[...], ms_ref[...])

    def body(c, _):
        run(pl.multiple_of(n_meta + c * chunk, 16), chunk, wm_ref[...], mm_ref[...])
        return 0

    lax.fori_loop(0, (t_all - n_meta) // chunk, body, 0, unroll=unroll)
    for h in range(hb):
        s_ref[h] = st_ref[h].T


def _recur_prompt(q, k, g, v, r, gn, n_meta, tiling):
    chunk, hb, unroll = tiling
    b, t, kd = q.shape
    vd = v.shape[2]
    nh = kd // DK
    dv = vd // nh
    ws, ms = _chunk_constants(n_meta)
    wm, mm = _chunk_constants(chunk)
    kblk = pl.BlockSpec((None, t, hb * DK), lambda i, h: (i, 0, h))
    vblk = pl.BlockSpec((None, t, hb * dv), lambda i, h: (i, 0, h))

    def const(x):
        return pl.BlockSpec(x.shape, lambda i, h: (0,) * x.ndim)

    return pl.pallas_call(
        functools.partial(_recur_prompt_kernel, n_meta=n_meta, chunk=chunk, hb=hb,
                          unroll=unroll),
        out_shape=(jax.ShapeDtypeStruct((b, t, vd), BF16),
                   jax.ShapeDtypeStruct((b, nh, DK, dv), F32)),
        grid=(b, nh // hb),
        in_specs=[pl.BlockSpec((1, dv), lambda i, h: (0, 0)), const(ws), const(ms), const(wm),
                  const(mm), kblk, kblk, kblk, vblk, vblk],
        out_specs=(vblk, pl.BlockSpec((None, hb, DK, dv), lambda i, h: (i, h, 0, 0))),
        scratch_shapes=[pltpu.VMEM((hb, dv, DK), F32)],
        compiler_params=_cparams("parallel", "parallel"),
        name="recur_prompt",
    )(gn, ws, ms, wm, mm, q, k, g, v, r)


def _recur_sample_kernel(gn_ref, w_ref, m_ref, q_ref, k_ref, g_ref, v_ref, r_ref, s0_ref,
                         o_ref, s_ref, *, n_new):
    nh = s0_ref.shape[0]
    dv = s0_ref.shape[2]
    gn = gn_ref[...]
    masks = m_ref[...]
    pad_rows = SMALL_CHUNK - SAMPLE_ROWS
    kd = q_ref.shape[1]
    valid = lax.broadcasted_iota(jnp.int32, (SAMPLE_ROWS, kd), 0) < n_new

    def padded(x):
        return jnp.concatenate([x, jnp.zeros((pad_rows, x.shape[1]), x.dtype)], axis=0)

    q = padded(q_ref[...])
    k = padded(jnp.where(valid, k_ref[...], 0.0))
    ex = _chunk_exponents(padded(jnp.where(valid, g_ref[...], 0.0)), w_ref[...])
    vb = padded(v_ref[...].astype(F32)).astype(BF16)
    for h in range(nh):
        ksl = slice(h * DK, (h + 1) * DK)
        vsl = slice(h * dv, (h + 1) * dv)
        o, st = _chunk_head(q[:, ksl], k[:, ksl], vb[:, vsl], s0_ref[h].T, ex[:, ksl], masks)
        s_ref[h] = st.T
        o_ref[:, vsl] = _gate_norm(o[0:SAMPLE_ROWS], r_ref[:, vsl], gn).astype(o_ref.dtype)


def _recur_sample(q, k, g, v, r, s0, gn, n_new):
    n, kd = q.shape
    vd = v.shape[1]
    nb, nh, _, dv = s0.shape
    w, masks = _chunk_constants(SMALL_CHUNK)
    kblk = pl.BlockSpec((SAMPLE_ROWS, kd), lambda i: (i, 0))
    vblk = pl.BlockSpec((SAMPLE_ROWS, vd), lambda i: (i, 0))
    sblk = pl.BlockSpec((None, nh, DK, dv), lambda i: (i, 0, 0, 0))
    return pl.pallas_call(
        functools.partial(_recur_sample_kernel, n_new=n_new),
        out_shape=(jax.ShapeDtypeStruct((n, vd), BF16), jax.ShapeDtypeStruct(s0.shape, F32)),
        grid=(nb,),
        in_specs=[_full_spec((1, dv)), _full_spec(w.shape), _full_spec(masks.shape),
                  kblk, kblk, kblk, vblk, vblk, sblk],
        out_specs=(vblk, sblk),
        compiler_params=_cparams("parallel"),
        name="recur_sample",
    )(gn, w, masks, q, k, g, v, r, s0)


def _mix_mlp_kernel(a_ref, wo_ref, x_ref, nw_ref, wu_ref, wd_ref, o_ref, xn_ref):
    j = pl.program_id(1)

    @pl.when(j == 0)
    def _():
        h = x_ref[...] + jnp.dot(a_ref[...], wo_ref[...], preferred_element_type=F32)
        xn_ref[...] = _rms(h, nw_ref[...]).astype(BF16)
        o_ref[...] = h

    u = jnp.dot(xn_ref[...], wu_ref[...].astype(BF16), preferred_element_type=F32)
    u = jnp.square(jnp.maximum(u, 0.0)).astype(BF16)
    o_ref[...] += jnp.dot(u, wd_ref[...].astype(BF16), preferred_element_type=F32)


def _mix_mlp(a, wo, x, nw, wu, wd, layer, tm, tf):
    n, d = x.shape
    ff = wu.shape[2]
    return pl.pallas_call(
        _mix_mlp_kernel,
        out_shape=jax.ShapeDtypeStruct((n, d), F32),
        grid=(n // tm, ff // tf),
        in_specs=[pl.BlockSpec((tm, a.shape[1]), lambda i, j: (i, 0)),
                  pl.BlockSpec(wo.shape, lambda i, j: (0, 0)),
                  pl.BlockSpec((tm, d), lambda i, j: (i, 0)),
                  pl.BlockSpec((1, d), lambda i, j: (0, 0)),
                  pl.BlockSpec((None, d, tf), lambda i, j: (layer, 0, j)),
                  pl.BlockSpec((None, tf, d), lambda i, j: (layer, j, 0))],
        out_specs=pl.BlockSpec((tm, d), lambda i, j: (i, 0)),
        scratch_shapes=[pltpu.VMEM((tm, d), BF16)],
        compiler_params=_cparams("parallel", "arbitrary"),
        name="mix_mlp",
    )(a, wo, x, nw, wu, wd)


def _final_norm_prompt_kernel(x_ref, nw_ref, o_ref, *, n_meta):
    tm = o_ref.shape[0]
    r0 = pl.multiple_of(n_meta + pl.program_id(1) * tm, 8)
    o_ref[...] = _rms(x_ref[pl.ds(r0, tm)], nw_ref[...])


def _final_norm_prompt(x, nw, n_meta):
    b, t, d = x.shape
    seq = t - n_meta
    tm = _pick_tile(seq, 512)
    return pl.pallas_call(
        functools.partial(_final_norm_prompt_kernel, n_meta=n_meta),
        out_shape=jax.ShapeDtypeStruct((b, seq, d), F32),
        grid=(b, seq // tm),
        in_specs=[pl.BlockSpec((None, t, d), lambda i, j: (i, 0, 0)),
                  pl.BlockSpec((1, d), lambda i, j: (0, 0))],
        out_specs=pl.BlockSpec((None, tm, d), lambda i, j: (i, j, 0)),
        compiler_params=_cparams("parallel", "arbitrary"),
        name="final_norm_prompt",
    )(x, nw)


def _final_norm_kernel(x_ref, nw_ref, o_ref):
    o_ref[...] = _rms(x_ref[...], nw_ref[...])


def _final_norm(x, nw):
    return pl.pallas_call(
        _final_norm_kernel,
        out_shape=jax.ShapeDtypeStruct(x.shape, F32),
        name="final_norm",
    )(x, nw)


def _rope_tables(pos):
    half = ROT_DIM // 2
    inv_freq = ROPE_THETA ** (-jnp.arange(half, dtype=F32) / half)
    ang = pos.astype(F32)[:, None] * inv_freq[None, :]
    cos, sin = jnp.cos(ang), jnp.sin(ang)
    n = pos.shape[0]
    zeros = jnp.zeros((n, half), F32)
    rest0 = jnp.zeros((n, HEAD_A - ROT_DIM), F32)
    cos_h = jnp.concatenate([cos, cos, jnp.ones((n, HEAD_A - ROT_DIM), F32)], axis=1)
    sin_m = jnp.concatenate([-sin, zeros, rest0], axis=1)
    sin_p = jnp.concatenate([zeros, sin, rest0], axis=1)
    return tuple(jnp.concatenate([t, t], axis=1) for t in (cos_h, sin_m, sin_p))


def kernel(x_prompt, x_sample, cache_k, cache_v, state_gla, state_hgrn, page_table, meta_tokens,
           norm_mix, norm_ffn, norm_final, w_in_a, lam_a, subln_a, w_out_a, w_in_b, w_gate2_b,
           b_gate_b, gnorm_b, w_out_b, w_in_c, lb_c, gnorm_c, w_out_c, w_up, w_down):
    bp, seq, d = x_prompt.shape
    db, n_new = x_sample.shape[:2]
    depth = norm_mix.shape[0]
    n_pages = page_table.shape[1]
    page = cache_k.shape[2]
    past_len = n_pages * page
    t_all = N_META_TOKENS + seq
    n_main = bp * t_all
    n_small = db * SAMPLE_ROWS
    h_a = d // (2 * HEAD_A)
    kd_b = w_gate2_b.shape[2]
    dv_b = d // H_B
    h_c = d // DK
    dv_c = d // h_c

    hp = jnp.concatenate(
        [jnp.broadcast_to(meta_tokens.astype(F32)[None], (bp, N_META_TOKENS, d)), x_prompt],
        axis=1).reshape(n_main, d)
    hs = jnp.pad(x_sample, ((0, 0), (0, SAMPLE_ROWS - n_new), (0, 0))).reshape(n_small, d)

    tm_main = _pick_tile(t_all, PROJ_ROWS)
    tm_mlp = _pick_tile(n_main, MLP_ROWS)
    tf = MLP_HIDDEN
    tab_p = _rope_tables(jnp.arange(t_all))
    tab_s = _rope_tables(jnp.tile(past_len + jnp.arange(SAMPLE_ROWS), db))
    ck = cache_k.transpose(0, 1, 3, 4, 2).reshape(cache_k.shape[0], cache_k.shape[1], d, page)
    cv = cache_v.reshape(cache_v.shape[0], cache_v.shape[1], page * h_a, 2 * HEAD_A)
    pp = PAGES_PER_STEP if n_pages % PAGES_PER_STEP == 0 else 1

    n_a = w_in_a.shape[0]
    kv_prompt = kv_sample = None
    gla_p, gla_s, hgrn_p, hgrn_s = [], [], [], []
    for i in range(depth):
        kind, j = i % N_MIXERS, i // N_MIXERS
        nw = norm_mix[i].reshape(1, d)
        if kind == 0:
            lam_init = 0.8 - 0.6 * math.exp(-0.3 * i)
            w = w_in_a[j].astype(BF16)
            lam = lam_a[j].astype(F32)
            subln = subln_a[j].reshape(1, 2 * HEAD_A)
            qp, *kv_prompt = _proj_a(hp, nw, w, tab_p, tm_main, t_all // tm_main, n_a, j,
                                     kv_prompt)
            qs, *kv_sample = _proj_a(hs, nw, w, tab_s, n_small, 1, n_a, j, kv_sample)
            ks, vs = kv_sample[0][j], kv_sample[1][j]
            shp = (n_a, bp, t_all, d)
            ap = _attn_prompt(qp.reshape(shp[1:]), kv_prompt[0].reshape(shp),
                              kv_prompt[1].reshape(shp), j, lam, subln, lam_init,
                              N_META_TOKENS, ATTN_HEADS).reshape(n_main, d)
            ks4 = ks.reshape(db, SAMPLE_ROWS, 2 * h_a, HEAD_A)
            vs4 = vs.reshape(db, SAMPLE_ROWS, h_a, 2 * HEAD_A)
            q_rows = qs.reshape(db, SAMPLE_ROWS, 2 * h_a, HEAD_A)[:, :n_new]
            q_rows = q_rows.transpose(0, 2, 1, 3).reshape(db, 2 * h_a * n_new, HEAD_A)
            k_rows = jnp.pad(ks.reshape(db, SAMPLE_ROWS, d).transpose(0, 2, 1),
                             ((0, 0), (0, 0), (0, page - SAMPLE_ROWS)))
            v_rows = jnp.pad(vs4.reshape(db, SAMPLE_ROWS * h_a, 2 * HEAD_A),
                             ((0, 0), (0, (SMALL_CHUNK - SAMPLE_ROWS) * h_a), (0, 0)))
            as_ = _attn_sample(q_rows, k_rows, v_rows, ck, cv, page_table, j, lam, subln,
                               lam_init, n_new, pp)
            w_out = w_out_a[j]
        elif kind == 1:
            w = w_in_b[j]
            n_main_cols = 2 * kd_b + 2 * d
            w_main = w[:, :n_main_cols].astype(BF16)
            w_gl = jnp.pad(w[:, n_main_cols:], ((0, 0), (0, LANES - GATE_RANK))).astype(BF16)
            w_g2 = jnp.pad(w_gate2_b[j], ((0, LANES - GATE_RANK), (0, 0))).astype(BF16)
            bg = b_gate_b[j].reshape(1, kd_b)
            gn = gnorm_b[j].reshape(1, dv_b)
            qp, kp, vp, rp, gp = _proj_b(hp, nw, w_main, w_gl, w_g2, bg, tm_main)
            qs, ks, vs, rs, gs = _proj_b(hs, nw, w_main, w_gl, w_g2, bg, n_small)
            r3 = lambda a: a.reshape(bp, t_all, a.shape[1])
            ap, sp = _recur_prompt(r3(qp), r3(kp), r3(gp), r3(vp), r3(rp), gn, N_META_TOKENS,
                                   GLA_TILING)
            ap = ap.reshape(n_main, d)
            as_, ss = _recur_sample(qs, ks, gs, vs, rs, state_gla[j].astype(F32), gn, n_new)
            gla_p.append(sp)
            gla_s.append(ss)
            w_out = w_out_b[j]
        else:
            w = w_in_c[j].astype(BF16)
            gn = gnorm_c[j].reshape(1, dv_c)
            lbf = lb_c.astype(F32)
            qp, kp, gp, vp, rp = _proj_c(hp, nw, w, lbf, i, tm_main)
            qs, ks, gs, vs, rs = _proj_c(hs, nw, w, lbf, i, n_small)
            r3 = lambda a: a.reshape(bp, t_all, a.shape[1])
            ap, sp = _recur_prompt(r3(qp), r3(kp), r3(gp), r3(vp), r3(rp), gn, N_META_TOKENS,
                                   HGRN_TILING)
            ap = ap.reshape(n_main, d)
            as_, ss = _recur_sample(qs, ks, gs, vs, rs, state_hgrn[j].astype(F32), gn, n_new)
            hgrn_p.append(sp)
            hgrn_s.append(ss)
            w_out = w_out_c[j]
        w_out = w_out.astype(BF16)
        nf = norm_ffn[i].reshape(1, d)
        hp = _mix_mlp(ap, w_out, hp, nf, w_up, w_down, i, tm_mlp, tf)
        hs = _mix_mlp(as_, w_out, hs, nf, w_up, w_down, i, n_small, tf)

    nfin = norm_final.reshape(1, d)
    y_prompt = _final_norm_prompt(hp.reshape(bp, t_all, d), nfin, N_META_TOKENS)
    y_sample = _final_norm(hs, nfin).reshape(db, SAMPLE_ROWS, d)[:, :n_new]
    k_sample, v_sample = (a.reshape(n_a, db, SAMPLE_ROWS, d)[:, :, :n_new] for a in kv_sample)
    return (y_prompt, y_sample,
            kv_prompt[0].reshape(n_a, bp, t_all, 2 * h_a, HEAD_A),
            kv_prompt[1].reshape(n_a, bp, t_all, h_a, 2 * HEAD_A),
            k_sample.reshape(n_a, db, n_new, 2 * h_a, HEAD_A),
            v_sample.reshape(n_a, db, n_new, h_a, 2 * HEAD_A),
            jnp.stack(gla_p), jnp.stack(gla_s), jnp.stack(hgrn_p), jnp.stack(hgrn_s))
```

```python
import functools
import math

import numpy as np
import jax
import jax.numpy as jnp
from jax import lax
from jax.experimental import pallas as pl
from jax.experimental.pallas import tpu as pltpu

F32 = jnp.float32
BF16 = jnp.bfloat16

RMS_EPS = 1e-6
N_META_TOKENS = 16
N_MIXERS = 3
HEAD_A = 64
ROT_DIM = HEAD_A // 4
ROPE_THETA = 500000.0
H_B = 4
GATE_RANK = 16
GATE_TEMP = 16.0
DK = 128
SAMPLE_ROWS = 8
SMALL_CHUNK = 16
GLA_TILING = (64, 2, 8)
HGRN_TILING = (128, 4, 4)
PROJ_ROWS = 704
PROJ_COLS = 512
MLP_ROWS = 1400
MLP_HIDDEN = 512
ATTN_Q_ROWS = 512
ATTN_HEADS = 4
PAGES_PER_STEP = 16
LANES = 128
ONES_ROWS = 16
VMEM_LIMIT = 56 * 1024 * 1024
LOG2E = 1.4426950408889634

NT_DIMS = (((1,), (1,)), ((), ()))
TN_DIMS = (((0,), (0,)), ((), ()))


def _cparams(*sem):
    return pltpu.CompilerParams(dimension_semantics=sem, vmem_limit_bytes=VMEM_LIMIT)


def _pick_tile(n, target, mult=16):
    best = None
    for t in range(mult, min(n, target) + 1, mult):
        if n % t == 0:
            best = t
    return n if best is None else best


def _rms(x, w):
    ms = jnp.mean(x * x, axis=-1, keepdims=True)
    return x * lax.rsqrt(ms + RMS_EPS) * w


def _sigmoid(x):
    return 1.0 / (1.0 + jnp.exp(-x))


def _proj_a_kernel(x_ref, nw_ref, w_ref, c_ref, sm_ref, sp_ref, *rest, cw):
    q_ref, k_ref, v_ref = rest[-3:]
    xn = _rms(x_ref[...], nw_ref[...]).astype(BF16)
    d = q_ref.shape[1]
    rep = cw // LANES
    cos = jnp.tile(c_ref[...], (1, rep))
    sin_m = jnp.tile(sm_ref[...], (1, rep))
    sin_p = jnp.tile(sp_ref[...], (1, rep))
    for dst_i, dst in enumerate((q_ref, k_ref, v_ref)):
        for c in range(d // cw):
            col = dst_i * d + c * cw
            y = jnp.dot(xn, w_ref[:, col:col + cw], preferred_element_type=F32)
            if dst_i < 2:
                y = (y * cos + pltpu.roll(y, ROT_DIM // 2, 1) * sin_p
                     + pltpu.roll(y, cw - ROT_DIM // 2, 1) * sin_m)
            dst[:, c * cw:(c + 1) * cw] = y


def _proj_b_kernel(x_ref, nw_ref, w_ref, wgl_ref, wg2_ref, bg_ref,
                   q_ref, k_ref, v_ref, r_ref, g_ref, *, cw):
    xn = _rms(x_ref[...], nw_ref[...]).astype(BF16)
    col = 0
    for dst, scale in ((q_ref, DK ** -0.5), (k_ref, None), (v_ref, None), (r_ref, None)):
        for c in range(dst.shape[1] // cw):
            y = jnp.dot(xn, w_ref[:, col:col + cw], preferred_element_type=F32)
            if scale is not None:
                y = y * scale
            dst[:, c * cw:(c + 1) * cw] = y.astype(dst.dtype)
            col += cw
    gl = jnp.dot(xn, wgl_ref[...], preferred_element_type=F32).astype(BF16)
    z = jnp.dot(gl, wg2_ref[...], preferred_element_type=F32) + bg_ref[...]
    g_ref[...] = (jnp.minimum(z, 0.0) - jnp.log(1.0 + jnp.exp(-jnp.abs(z)))) * (1.0 / GATE_TEMP)


def _proj_c_kernel(x_ref, nw_ref, w_ref, lb_ref, q_ref, k_ref, g_ref, v_ref, r_ref, *, cw, layer):
    xn = _rms(x_ref[...], nw_ref[...]).astype(BF16)
    d = q_ref.shape[1]
    lbs = lb_ref[...]
    mx = jnp.max(lbs, axis=0, keepdims=True)
    e = jnp.exp(lbs - mx)
    lb = jnp.sum(e[1:layer + 1], axis=0, keepdims=True) / jnp.sum(e, axis=0, keepdims=True)
    for c in range(d // cw):
        sl = slice(c * cw, (c + 1) * cw)
        y = jnp.dot(xn, w_ref[:, c * cw:(c + 1) * cw], preferred_element_type=F32)
        q_ref[:, sl] = y * _sigmoid(y) * (DK ** -0.5)
        f = jnp.dot(xn, w_ref[:, d + c * cw:d + (c + 1) * cw], preferred_element_type=F32)
        lbc = lb[:, sl]
        forget = lbc + (1.0 - lbc) * _sigmoid(f)
        k_ref[:, sl] = 1.0 - forget
        g_ref[:, sl] = jnp.log(forget)
        v_ref[:, sl] = jnp.dot(xn, w_ref[:, 2 * d + c * cw:2 * d + (c + 1) * cw],
                               preferred_element_type=F32).astype(v_ref.dtype)
        r_ref[:, sl] = jnp.dot(xn, w_ref[:, 3 * d + c * cw:3 * d + (c + 1) * cw],
                               preferred_element_type=F32)


def _row_spec(tm, width):
    return pl.BlockSpec((tm, width), lambda i: (i, 0))


def _full_spec(shape):
    return pl.BlockSpec(shape, lambda i: (0,) * len(shape))


def _proj_a(x, nw, w, tables, tm, table_blocks, n_layers, layer_j, kv_prev):
    n, d = x.shape
    cw = min(PROJ_COLS, d)
    tab_spec = pl.BlockSpec((tm, LANES), lambda i: (i % table_blocks, 0))
    kv_shape = jax.ShapeDtypeStruct((n_layers, n, d), F32)
    kv_spec = pl.BlockSpec((None, tm, d), lambda i: (layer_j, i, 0))
    in_specs = [_row_spec(tm, d), _full_spec((1, d)), _full_spec(w.shape),
                tab_spec, tab_spec, tab_spec]
    args = [x, nw, w, *tables]
    aliases = {}
    if kv_prev is not None:
        aliases = {len(args): 1, len(args) + 1: 2}
        in_specs += [pl.BlockSpec(memory_space=pl.ANY)] * 2
        args += list(kv_prev)
    return pl.pallas_call(
        functools.partial(_proj_a_kernel, cw=cw),
        out_shape=(jax.ShapeDtypeStruct((n, d), F32), kv_shape, kv_shape),
        grid=(n // tm,),
        in_specs=in_specs,
        out_specs=(_row_spec(tm, d), kv_spec, kv_spec),
        input_output_aliases=aliases,
        compiler_params=_cparams("parallel"),
        name="proj_a",
    )(*args)


def _proj_b(x, nw, w, wgl, wg2, bg, tm):
    n, d = x.shape
    kd, vd = wg2.shape[1], d
    cw = min(PROJ_COLS, kd)
    return pl.pallas_call(
        functools.partial(_proj_b_kernel, cw=cw),
        out_shape=(jax.ShapeDtypeStruct((n, kd), F32), jax.ShapeDtypeStruct((n, kd), F32),
                   jax.ShapeDtypeStruct((n, vd), BF16), jax.ShapeDtypeStruct((n, vd), F32),
                   jax.ShapeDtypeStruct((n, kd), F32)),
        grid=(n // tm,),
        in_specs=[_row_spec(tm, d), _full_spec((1, d)), _full_spec(w.shape),
                  _full_spec(wgl.shape), _full_spec(wg2.shape), _full_spec((1, kd))],
        out_specs=(_row_spec(tm, kd), _row_spec(tm, kd), _row_spec(tm, vd), _row_spec(tm, vd),
                   _row_spec(tm, kd)),
        compiler_params=_cparams("parallel"),
        name="proj_b",
    )(x, nw, w, wgl, wg2, bg)


def _proj_c(x, nw, w, lb_c, layer, tm):
    n, d = x.shape
    cw = min(PROJ_COLS, d)
    out = jax.ShapeDtypeStruct((n, d), F32)
    return pl.pallas_call(
        functools.partial(_proj_c_kernel, cw=cw, layer=layer),
        out_shape=(out, out, out, jax.ShapeDtypeStruct((n, d), BF16), out),
        grid=(n // tm,),
        in_specs=[_row_spec(tm, d), _full_spec((1, d)), _full_spec(w.shape),
                  _full_spec(lb_c.shape)],
        out_specs=(_row_spec(tm, d),) * 5,
        compiler_params=_cparams("parallel"),
        name="proj_c",
    )(x, nw, w, lb_c)


def _lambda_full(lam_ref, lam_init):
    l = lam_ref[...]
    a = jnp.sum(l[0:1] * l[1:2], axis=-1, keepdims=True)
    b = jnp.sum(l[2:3] * l[3:4], axis=-1, keepdims=True)
    return jnp.exp(a) - jnp.exp(b) + lam_init


def _head_finish(o, subln, lam_init):
    ms = jnp.mean(o * o, axis=-1, keepdims=True)
    return o * lax.rsqrt(ms + RMS_EPS) * subln * (1.0 - lam_init)


def _attn_prompt_kernel(lam_ref, subln_ref, q_ref, k_ref, v_ref, o_ref,
                        kb_ref, vt_ref, vt2_ref, km_ref, vtm_ref, m_ref, acc_ref, bias_ref,
                        *, tq, n_meta, lam_init, hb):
    t_all = q_ref.shape[0]
    nq = (t_all - n_meta) // tq
    hw = 2 * HEAD_A
    lam = _lambda_full(lam_ref, lam_init)
    subln = subln_ref[...]
    first_map = lax.broadcasted_iota(jnp.int32, (1, hw), 1) < HEAD_A

    def stack_q(qt, scale):
        qt = qt * scale
        return jnp.concatenate([jnp.where(first_map, qt, 0.0), jnp.where(first_map, 0.0, qt)],
                               axis=0).astype(BF16)

    r = lax.broadcasted_iota(jnp.int32, (2 * n_meta, n_meta), 0)
    c = lax.broadcasted_iota(jnp.int32, (2 * n_meta, n_meta), 1)
    meta_causal = c <= jnp.where(r >= n_meta, r - n_meta, r)
    pad = jnp.zeros((hw - n_meta, hw), F32)
    for h in range(hb):
        lanes = slice(h * hw, (h + 1) * hw)
        kmeta = k_ref[0:n_meta, lanes]
        vmeta = v_ref[0:n_meta, lanes]
        qs = stack_q(q_ref[0:n_meta, lanes], HEAD_A ** -0.5)
        s = lax.dot_general(qs, kmeta.astype(BF16), NT_DIMS, preferred_element_type=F32)
        s = jnp.where(meta_causal, s, -jnp.inf)
        p = jnp.exp(s - jnp.max(s, axis=-1, keepdims=True))
        acc = jnp.dot(p.astype(BF16), vmeta.astype(BF16), preferred_element_type=F32)
        acc = acc / jnp.sum(p, axis=-1, keepdims=True)
        o = acc[:n_meta] - lam * acc[n_meta:]
        o_ref[0:n_meta, lanes] = _head_finish(o, subln, lam_init).astype(o_ref.dtype)
        km_ref[h] = kmeta.astype(BF16)
        vmeta_t = jnp.concatenate([vmeta, pad], axis=0).T[:, :n_meta]
        vtm_ref[h] = jnp.concatenate(
            [vmeta_t.astype(BF16), jnp.ones((ONES_ROWS, n_meta), BF16)], axis=0)
        for cidx in range(nq):
            rows = slice(n_meta + cidx * tq, n_meta + (cidx + 1) * tq)
            kb_ref[h, cidx] = k_ref[rows, lanes].astype(BF16)
            vte = jnp.concatenate([v_ref[rows, lanes].T.astype(BF16),
                                   jnp.ones((ONES_ROWS, tq), BF16)], axis=0)
            vt_ref[h, cidx] = vte
            if cidx < 2 * (nq // 2):
                vt2_ref[h, cidx // 2, :, (cidx % 2) * tq:(cidx % 2 + 1) * tq] = vte

    key_i = lax.broadcasted_iota(jnp.int32, (n_meta + tq, 2 * tq), 0)
    qry_i = lax.broadcasted_iota(jnp.int32, (n_meta + tq, 2 * tq), 1)
    key_limit = jnp.where(key_i < n_meta, n_meta,
                          n_meta + jnp.where(qry_i >= tq, qry_i - tq, qry_i))
    bias_ref[...] = jnp.where(key_i <= key_limit, 0.0, -jnp.inf)

    def online(h, s, vte):
        m_old = m_ref[h]
        m_new = jnp.maximum(m_old, jnp.max(s, axis=0, keepdims=True))
        alpha = jnp.exp2(m_old - m_new)
        p = jnp.exp2(s - m_new).astype(BF16)
        acc_ref[h] = alpha * acc_ref[h] + jnp.dot(vte, p, preferred_element_type=F32)
        m_ref[h] = m_new

    def q_body(qi, _):
        q0 = n_meta + qi * tq
        qss = [stack_q(q_ref[pl.ds(q0, tq), h * hw:(h + 1) * hw], (HEAD_A ** -0.5) * LOG2E)
               for h in range(hb)]
        for h in range(hb):
            s = jnp.concatenate(
                [lax.dot_general(km_ref[h], qss[h], NT_DIMS, preferred_element_type=F32),
                 lax.dot_general(kb_ref[h, qi], qss[h], NT_DIMS, preferred_element_type=F32)],
                axis=0)
            s = s + bias_ref[...]
            m = jnp.max(s, axis=0, keepdims=True)
            p = jnp.exp2(s - m).astype(BF16)
            m_ref[h] = m
            acc_ref[h] = (jnp.dot(vtm_ref[h], p[:n_meta], preferred_element_type=F32)
                          + jnp.dot(vt_ref[h, qi], p[n_meta:], preferred_element_type=F32))

        def pair_body(j, _):
            for h in range(hb):
                kpair = kb_ref[h, pl.ds(2 * j, 2)].reshape(2 * tq, hw)
                s = lax.dot_general(kpair, qss[h], NT_DIMS, preferred_element_type=F32)
                online(h, s, vt2_ref[h, j])
            return 0

        for j in range(qi // 2):
            pair_body(j, 0)
        if qi % 2 == 1:
            for h in range(hb):
                s = lax.dot_general(kb_ref[h, qi - 1], qss[h], NT_DIMS,
                                    preferred_element_type=F32)
                online(h, s, vt_ref[h, qi - 1])

        for h in range(hb):
            acc = acc_ref[h]
            ot = acc[:hw] * (1.0 / acc[hw:hw + 1])
            od = ot[:, :tq] - lam * ot[:, tq:]
            ms = jnp.mean(od * od, axis=0, keepdims=True)
            od = od * lax.rsqrt(ms + RMS_EPS)
            o_ref[pl.ds(q0, tq), h * hw:(h + 1) * hw] = (
                od.T * (subln * (1.0 - lam_init))).astype(o_ref.dtype)
        return 0

    for qi in range(nq):
        q_body(qi, 0)


def _attn_prompt(q, k, v, layer_j, lam, subln, lam_init, n_meta, hb):
    b, t, d = q.shape
    hw = 2 * HEAD_A
    tq = _pick_tile(t - n_meta, ATTN_Q_ROWS, LANES)
    nq = (t - n_meta) // tq
    blk = pl.BlockSpec((None, t, hb * hw), lambda i, h: (i, 0, h))
    kvblk = pl.BlockSpec((None, None, t, hb * hw), lambda i, h: (layer_j, i, 0, h))
    return pl.pallas_call(
        functools.partial(_attn_prompt_kernel, tq=tq, n_meta=n_meta, lam_init=lam_init, hb=hb),
        out_shape=jax.ShapeDtypeStruct((b, t, d), BF16),
        grid=(b, d // (hb * hw)),
        in_specs=[pl.BlockSpec(lam.shape, lambda i, h: (0, 0)),
                  pl.BlockSpec((1, hw), lambda i, h: (0, 0)), blk, kvblk, kvblk],
        out_specs=blk,
        scratch_shapes=[pltpu.VMEM((hb, nq, tq, hw), BF16),
                        pltpu.VMEM((hb, nq, hw + ONES_ROWS, tq), BF16),
                        pltpu.VMEM((hb, max(nq // 2, 1), hw + ONES_ROWS, 2 * tq), BF16),
                        pltpu.VMEM((hb, n_meta, hw), BF16),
                        pltpu.VMEM((hb, hw + ONES_ROWS, n_meta), BF16),
                        pltpu.VMEM((hb, 1, 2 * tq), F32),
                        pltpu.VMEM((hb, hw + ONES_ROWS, 2 * tq), F32),
                        pltpu.VMEM((n_meta + tq, 2 * tq), F32)],
        compiler_params=_cparams("parallel", "parallel"),
        name="attn_prompt",
    )(lam, subln, q, k, v)


def _attn_sample_kernel(pt_ref, lam_ref, subln_ref, q_ref, kn_ref, vn_ref, *rest,
                        pp, n_new, lam_init):
    del pt_ref
    k_refs, v_refs = rest[:pp], rest[pp:2 * pp]
    o_ref, qbd_ref, m_ref, l_ref, acc_ref = rest[2 * pp:]
    step = pl.program_id(1)
    rows = q_ref.shape[0]
    n_heads = rows // n_new
    n_vheads = n_heads // 2
    page = k_refs[0].shape[1]
    grp = 4 * n_new

    def scores(k_list):
        qbd = qbd_ref[...]
        return jnp.concatenate(
            [jnp.dot(qbd, kr[...].astype(BF16), preferred_element_type=F32) for kr in k_list],
            axis=1)

    def values(p, v_list):
        out = []
        for g in range(rows // grp):
            pg = p[g * grp:(g + 1) * grp]
            halves = []
            for vh in (2 * g, 2 * g + 1):
                r = None
                for i, vr in enumerate(v_list):
                    tokens = vr.shape[0] // n_vheads
                    v = vr[pl.ds(vh, tokens, stride=n_vheads), :].astype(BF16)
                    t = jnp.dot(pg[:, i * page:i * page + tokens], v,
                                preferred_element_type=F32)
                    r = t if r is None else r + t
                halves.append(r)
            out.append(halves[0][:grp // 2])
            out.append(halves[1][grp // 2:])
        return jnp.concatenate(out, axis=0)

    @pl.when(step == 0)
    def _():
        d = n_heads * HEAD_A
        q = jnp.tile(q_ref[...] * ((HEAD_A ** -0.5) * LOG2E), (1, n_heads))
        row_head = lax.broadcasted_iota(jnp.int32, (rows, d), 0) // n_new
        lane_head = lax.broadcasted_iota(jnp.int32, (rows, d), 1) // HEAD_A
        qbd_ref[...] = jnp.where(row_head == lane_head, q, 0.0).astype(BF16)
        s = scores([kn_ref])
        tok = lax.broadcasted_iota(jnp.int32, (rows, page), 0) % n_new
        key = lax.broadcasted_iota(jnp.int32, (rows, page), 1)
        s = jnp.where(key <= tok, s, -jnp.inf)
        m = jnp.max(s, axis=-1, keepdims=True)
        p = jnp.exp2(s - m)
        m_ref[...] = m
        l_ref[...] = jnp.sum(p, axis=-1, keepdims=True)
        acc_ref[...] = values(p.astype(BF16), [vn_ref])

    s = scores(k_refs)
    m_old = m_ref[...]
    m_new = jnp.maximum(m_old, jnp.max(s, axis=-1, keepdims=True))
    alpha = jnp.exp2(m_old - m_new)
    p = jnp.exp2(s - m_new)
    l_ref[...] = alpha * l_ref[...] + jnp.sum(p, axis=-1, keepdims=True)
    acc_ref[...] = alpha * acc_ref[...] + values(p.astype(BF16), v_refs)
    m_ref[...] = m_new

    @pl.when(step == pl.num_programs(1) - 1)
    def _():
        lam = _lambda_full(lam_ref, lam_init)
        subln = subln_ref[...]
        hw = 2 * HEAD_A
        full = acc_ref[...] / l_ref[...]
        diff = full - lam * pltpu.roll(full, rows - n_new, 0)
        for h in range(n_heads // 2):
            tile = diff[2 * n_new * h:2 * n_new * (h + 1)]
            o_ref[:, h * hw:(h + 1) * hw] = _head_finish(tile, subln, lam_init).astype(o_ref.dtype)


def _attn_sample(q, k_new, v_new, cache_k, cache_v, page_table, layer_j, lam, subln, lam_init,
                 n_new, pp):
    nb, rows, _ = q.shape
    n_pages = page_table.shape[1]
    krows, page = cache_k.shape[2:]
    vrows = cache_v.shape[2]
    hw = 2 * HEAD_A
    d = (rows // n_new) * HEAD_A
    assert 2 * n_new == SAMPLE_ROWS and krows == d and vrows * 2 * HEAD_A == page * d

    def per_sample(shape):
        return pl.BlockSpec((None,) + shape, lambda b, s, pt: (b, 0, 0))

    def kpage(i):
        return pl.BlockSpec((None, None, krows, page),
                            lambda b, s, pt: (layer_j, pt[b, s * pp + i], 0, 0))

    def vpage(i):
        return pl.BlockSpec((None, None, vrows, hw),
                            lambda b, s, pt: (layer_j, pt[b, s * pp + i], 0, 0))

    grid_spec = pltpu.PrefetchScalarGridSpec(
        num_scalar_prefetch=1,
        grid=(nb, n_pages // pp),
        in_specs=[pl.BlockSpec(lam.shape, lambda b, s, pt: (0, 0)),
                  pl.BlockSpec((1, hw), lambda b, s, pt: (0, 0)),
                  per_sample(q.shape[1:]), per_sample(k_new.shape[1:]),
                  per_sample(v_new.shape[1:])]
                 + [kpage(i) for i in range(pp)] + [vpage(i) for i in range(pp)],
        out_specs=pl.BlockSpec((SAMPLE_ROWS, d), lambda b, s, pt: (b, 0)),
        scratch_shapes=[pltpu.VMEM((rows, d), BF16), pltpu.VMEM((rows, 1), F32),
                        pltpu.VMEM((rows, 1), F32), pltpu.VMEM((rows, hw), F32)],
    )
    return pl.pallas_call(
        functools.partial(_attn_sample_kernel, pp=pp, n_new=n_new, lam_init=lam_init),
        out_shape=jax.ShapeDtypeStruct((nb * SAMPLE_ROWS, d), BF16),
        grid_spec=grid_spec,
        compiler_params=_cparams("parallel", "arbitrary"),
        name="attn_sample",
    )(page_table, lam, subln, q, k_new, v_new, *([cache_k] * pp), *([cache_v] * pp))


def _chunk_constants(length):
    nlev = int(math.log2(length))
    assert 2 ** nlev == length
    w = np.zeros(((2 + nlev) * length, length), np.float32)
    masks = np.zeros((nlev + 1, length, length), np.float32)
    masks[0] = np.eye(length)
    for t in range(length):
        w[t, :t + 1] = 1.0
        w[length + t, t + 1:] = 1.0
    for lvl in range(1, nlev + 1):
        bs, half = 2 ** lvl, 2 ** (lvl - 1)
        for t in range(length):
            mid = t - t % bs + half
            row = (1 + lvl) * length + t
            if t >= mid:
                w[row, mid:t + 1] = 1.0
                masks[lvl, t, mid - half:mid] = 1.0
            else:
                w[row, t + 1:mid] = 1.0
    return jnp.asarray(np.tile(w, (1, 3)), BF16), jnp.asarray(masks, F32)


def _chunk_exponents(g, w3):
    g = g * LOG2E
    g1 = g.astype(BF16)
    rem = g - g1.astype(F32)
    g2 = rem.astype(BF16)
    g3 = (rem - g2.astype(F32)).astype(BF16)
    return jnp.exp2(jnp.dot(w3, jnp.concatenate([g1, g2, g3], axis=0),
                            preferred_element_type=F32))


def _chunk_head(q, k, vb, st, ex, masks):
    length = q.shape[0]
    nlev = masks.shape[0] - 1
    e_cum = ex[0:length]
    e_rem = ex[length:2 * length]
    e_last = e_cum[length - 1:length]
    o = lax.dot_general((q * e_cum).astype(BF16), st.astype(BF16), NT_DIMS,
                        preferred_element_type=F32)
    qb, kb = q.astype(BF16), k.astype(BF16)
    a = masks[0] * lax.dot_general(qb, kb, NT_DIMS, preferred_element_type=F32)
    row = lax.broadcasted_iota(jnp.int32, q.shape, 0)
    for lvl in range(1, nlev + 1):
        second_half = (row & (2 ** lvl - 1)) >= 2 ** (lvl - 1)
        x = (jnp.where(second_half, q, k) * ex[(1 + lvl) * length:(2 + lvl) * length]).astype(BF16)
        a = a + masks[lvl] * lax.dot_general(x, x, NT_DIMS, preferred_element_type=F32)
    o = o + jnp.dot(a.astype(BF16), vb, preferred_element_type=F32)
    kd = (k * e_rem).astype(BF16)
    st = st * e_last + lax.dot_general(vb, kd, TN_DIMS, preferred_element_type=F32)
    return o, st


def _gate_norm(o, r, gn):
    ms = jnp.mean(o * o, axis=-1, keepdims=True)
    return o * lax.rsqrt(ms + RMS_EPS) * gn * (r * _sigmoid(r))


def _recur_prompt_kernel(gn_ref, ws_ref, ms_ref, wm_ref, mm_ref, q_ref, k_ref, g_ref, v_ref, r_ref,
                         o_ref, s_ref, st_ref, *, n_meta, chunk, hb, unroll):
    t_all = q_ref.shape[0]
    dv = v_ref.shape[1] // hb
    gn = gn_ref[...]
    st_ref[...] = jnp.zeros_like(st_ref)

    def run(r0, length, w3, masks):
        rows = pl.ds(r0, length)
        ex = _chunk_exponents(g_ref[rows], w3)
        for h in range(hb):
            ksl = slice(h * DK, (h + 1) * DK)
            vsl = slice(h * dv, (h + 1) * dv)
            o, st = _chunk_head(q_ref[rows, ksl], k_ref[rows, ksl], v_ref[rows, vsl], st_ref[h],
                                ex[:, ksl], masks)
            st_ref[h] = st
            o_ref[rows, vsl] = _gate_norm(o, r_ref[rows, vsl], gn).astype(o_ref.dtype)

    run(0, n_meta, ws_ref[...], ms_ref[...])

    def body(c, _):
        run(pl.multiple_of(n_meta + c * chunk, 16), chunk, wm_ref[...], mm_ref[...])
        return 0

    lax.fori_loop(0, (t_all - n_meta) // chunk, body, 0, unroll=unroll)
    for h in range(hb):
        s_ref[h] = st_ref[h].T


def _recur_prompt(q, k, g, v, r, gn, n_meta, tiling):
    chunk, hb, unroll = tiling
    b, t, kd = q.shape
    vd = v.shape[2]
    nh = kd // DK
    dv = vd // nh
    ws, ms = _chunk_constants(n_meta)
    wm, mm = _chunk_constants(chunk)
    kblk = pl.BlockSpec((None, t, hb * DK), lambda i, h: (i, 0, h))
    vblk = pl.BlockSpec((None, t, hb * dv), lambda i, h: (i, 0, h))

    def const(x):
        return pl.BlockSpec(x.shape, lambda i, h: (0,) * x.ndim)

    return pl.pallas_call(
        functools.partial(_recur_prompt_kernel, n_meta=n_meta, chunk=chunk, hb=hb,
                          unroll=unroll),
        out_shape=(jax.ShapeDtypeStruct((b, t, vd), BF16),
                   jax.ShapeDtypeStruct((b, nh, DK, dv), F32)),
        grid=(b, nh // hb),
        in_specs=[pl.BlockSpec((1, dv), lambda i, h: (0, 0)), const(ws), const(ms), const(wm),
                  const(mm), kblk, kblk, kblk, vblk, vblk],
        out_specs=(vblk, pl.BlockSpec((None, hb, DK, dv), lambda i, h: (i, h, 0, 0))),
        scratch_shapes=[pltpu.VMEM((hb, dv, DK), F32)],
        compiler_params=_cparams("parallel", "parallel"),
        name="recur_prompt",
    )(gn, ws, ms, wm, mm, q, k, g, v, r)


def _recur_sample_kernel(gn_ref, w_ref, m_ref, q_ref, k_ref, g_ref, v_ref, r_ref, s0_ref,
                         o_ref, s_ref, *, n_new):
    nh = s0_ref.shape[0]
    dv = s0_ref.shape[2]
    gn = gn_ref[...]
    masks = m_ref[...]
    pad_rows = SMALL_CHUNK - SAMPLE_ROWS
    kd = q_ref.shape[1]
    valid = lax.broadcasted_iota(jnp.int32, (SAMPLE_ROWS, kd), 0) < n_new

    def padded(x):
        return jnp.concatenate([x, jnp.zeros((pad_rows, x.shape[1]), x.dtype)], axis=0)

    q = padded(q_ref[...])
    k = padded(jnp.where(valid, k_ref[...], 0.0))
    ex = _chunk_exponents(padded(jnp.where(valid, g_ref[...], 0.0)), w_ref[...])
    vb = padded(v_ref[...].astype(F32)).astype(BF16)
    for h in range(nh):
        ksl = slice(h * DK, (h + 1) * DK)
        vsl = slice(h * dv, (h + 1) * dv)
        o, st = _chunk_head(q[:, ksl], k[:, ksl], vb[:, vsl], s0_ref[h].T, ex[:, ksl], masks)
        s_ref[h] = st.T
        o_ref[:, vsl] = _gate_norm(o[0:SAMPLE_ROWS], r_ref[:, vsl], gn).astype(o_ref.dtype)


def _recur_sample(q, k, g, v, r, s0, gn, n_new):
    n, kd = q.shape
    vd = v.shape[1]
    nb, nh, _, dv = s0.shape
    w, masks = _chunk_constants(SMALL_CHUNK)
    kblk = pl.BlockSpec((SAMPLE_ROWS, kd), lambda i: (i, 0))
    vblk = pl.BlockSpec((SAMPLE_ROWS, vd), lambda i: (i, 0))
    sblk = pl.BlockSpec((None, nh, DK, dv), lambda i: (i, 0, 0, 0))
    return pl.pallas_call(
        functools.partial(_recur_sample_kernel, n_new=n_new),
        out_shape=(jax.ShapeDtypeStruct((n, vd), BF16), jax.ShapeDtypeStruct(s0.shape, F32)),
        grid=(nb,),
        in_specs=[_full_spec((1, dv)), _full_spec(w.shape), _full_spec(masks.shape),
                  kblk, kblk, kblk, vblk, vblk, sblk],
        out_specs=(vblk, sblk),
        compiler_params=_cparams("parallel"),
        name="recur_sample",
    )(gn, w, masks, q, k, g, v, r, s0)


def _mix_mlp_kernel(a_ref, wo_ref, x_ref, nw_ref, wu_ref, wd_ref, o_ref, xn_ref):
    j = pl.program_id(1)

    @pl.when(j == 0)
    def _():
        h = x_ref[...] + jnp.dot(a_ref[...], wo_ref[...], preferred_element_type=F32)
        xn_ref[...] = _rms(h, nw_ref[...]).astype(BF16)
        o_ref[...] = h

    u = jnp.dot(xn_ref[...], wu_ref[...].astype(BF16), preferred_element_type=F32)
    u = jnp.square(jnp.maximum(u, 0.0)).astype(BF16)
    o_ref[...] += jnp.dot(u, wd_ref[...].astype(BF16), preferred_element_type=F32)


def _mix_mlp(a, wo, x, nw, wu, wd, layer, tm, tf):
    n, d = x.shape
    ff = wu.shape[2]
    return pl.pallas_call(
        _mix_mlp_kernel,
        out_shape=jax.ShapeDtypeStruct((n, d), F32),
        grid=(n // tm, ff // tf),
        in_specs=[pl.BlockSpec((tm, a.shape[1]), lambda i, j: (i, 0)),
                  pl.BlockSpec(wo.shape, lambda i, j: (0, 0)),
                  pl.BlockSpec((tm, d), lambda i, j: (i, 0)),
                  pl.BlockSpec((1, d), lambda i, j: (0, 0)),
                  pl.BlockSpec((None, d, tf), lambda i, j: (layer, 0, j)),
                  pl.BlockSpec((None, tf, d), lambda i, j: (layer, j, 0))],
        out_specs=pl.BlockSpec((tm, d), lambda i, j: (i, 0)),
        scratch_shapes=[pltpu.VMEM((tm, d), BF16)],
        compiler_params=_cparams("parallel", "arbitrary"),
        name="mix_mlp",
    )(a, wo, x, nw, wu, wd)


def _final_norm_prompt_kernel(x_ref, nw_ref, o_ref, *, n_meta):
    tm = o_ref.shape[0]
    r0 = pl.multiple_of(n_meta + pl.program_id(1) * tm, 8)
    o_ref[...] = _rms(x_ref[pl.ds(r0, tm)], nw_ref[...])


def _final_norm_prompt(x, nw, n_meta):
    b, t, d = x.shape
    seq = t - n_meta
    tm = _pick_tile(seq, 512)
    return pl.pallas_call(
        functools.partial(_final_norm_prompt_kernel, n_meta=n_meta),
        out_shape=jax.ShapeDtypeStruct((b, seq, d), F32),
        grid=(b, seq // tm),
        in_specs=[pl.BlockSpec((None, t, d), lambda i, j: (i, 0, 0)),
                  pl.BlockSpec((1, d), lambda i, j: (0, 0))],
        out_specs=pl.BlockSpec((None, tm, d), lambda i, j: (i, j, 0)),
        compiler_params=_cparams("parallel", "arbitrary"),
        name="final_norm_prompt",
    )(x, nw)


def _final_norm_kernel(x_ref, nw_ref, o_ref):
    o_ref[...] = _rms(x_ref[...], nw_ref[...])


def _final_norm(x, nw):
    return pl.pallas_call(
        _final_norm_kernel,
        out_shape=jax.ShapeDtypeStruct(x.shape, F32),
        name="final_norm",
    )(x, nw)


def _rope_tables(pos):
    half = ROT_DIM // 2
    inv_freq = ROPE_THETA ** (-jnp.arange(half, dtype=F32) / half)
    ang = pos.astype(F32)[:, None] * inv_freq[None, :]
    cos, sin = jnp.cos(ang), jnp.sin(ang)
    n = pos.shape[0]
    zeros = jnp.zeros((n, half), F32)
    rest0 = jnp.zeros((n, HEAD_A - ROT_DIM), F32)
    cos_h = jnp.concatenate([cos, cos, jnp.ones((n, HEAD_A - ROT_DIM), F32)], axis=1)
    sin_m = jnp.concatenate([-sin, zeros, rest0], axis=1)
    sin_p = jnp.concatenate([zeros, sin, rest0], axis=1)
    return tuple(jnp.concatenate([t, t], axis=1) for t in (cos_h, sin_m, sin_p))


def kernel(x_prompt, x_sample, cache_k, cache_v, state_gla, state_hgrn, page_table, meta_tokens,
           norm_mix, norm_ffn, norm_final, w_in_a, lam_a, subln_a, w_out_a, w_in_b, w_gate2_b,
           b_gate_b, gnorm_b, w_out_b, w_in_c, lb_c, gnorm_c, w_out_c, w_up, w_down):
    bp, seq, d = x_prompt.shape
    db, n_new = x_sample.shape[:2]
    depth = norm_mix.shape[0]
    n_pages = page_table.shape[1]
    page = cache_k.shape[2]
    past_len = n_pages * page
    t_all = N_META_TOKENS + seq
    n_main = bp * t_all
    n_small = db * SAMPLE_ROWS
    h_a = d // (2 * HEAD_A)
    kd_b = w_gate2_b.shape[2]
    dv_b = d // H_B
    h_c = d // DK
    dv_c = d // h_c

    hp = jnp.concatenate(
        [jnp.broadcast_to(meta_tokens.astype(F32)[None], (bp, N_META_TOKENS, d)), x_prompt],
        axis=1).reshape(n_main, d)
    hs = jnp.pad(x_sample, ((0, 0), (0, SAMPLE_ROWS - n_new), (0, 0))).reshape(n_small, d)

    tm_main = _pick_tile(t_all, PROJ_ROWS)
    tm_mlp = _pick_tile(n_main, MLP_ROWS)
    tf = MLP_HIDDEN
    tab_p = _rope_tables(jnp.arange(t_all))
    tab_s = _rope_tables(jnp.tile(past_len + jnp.arange(SAMPLE_ROWS), db))
    ck = cache_k.transpose(0, 1, 3, 4, 2).reshape(cache_k.shape[0], cache_k.shape[1], d, page)
    cv = cache_v.reshape(cache_v.shape[0], cache_v.shape[1], page * h_a, 2 * HEAD_A)
    pp = PAGES_PER_STEP if n_pages % PAGES_PER_STEP == 0 else 1

    n_a = w_in_a.shape[0]
    kv_prompt = kv_sample = None
    gla_p, gla_s, hgrn_p, hgrn_s = [], [], [], []
    for i in range(depth):
        kind, j = i % N_MIXERS, i // N_MIXERS
        nw = norm_mix[i].reshape(1, d)
        if kind == 0:
            lam_init = 0.8 - 0.6 * math.exp(-0.3 * i)
            w = w_in_a[j].astype(BF16)
            lam = lam_a[j].astype(F32)
            subln = subln_a[j].reshape(1, 2 * HEAD_A)
            qp, *kv_prompt = _proj_a(hp, nw, w, tab_p, tm_main, t_all // tm_main, n_a, j,
                                     kv_prompt)
            qs, *kv_sample = _proj_a(hs, nw, w, tab_s, n_small, 1, n_a, j, kv_sample)
            ks, vs = kv_sample[0][j], kv_sample[1][j]
            shp = (n_a, bp, t_all, d)
            ap = _attn_prompt(qp.reshape(shp[1:]), kv_prompt[0].reshape(shp),
                              kv_prompt[1].reshape(shp), j, lam, subln, lam_init,
                              N_META_TOKENS, ATTN_HEADS).reshape(n_main, d)
            ks4 = ks.reshape(db, SAMPLE_ROWS, 2 * h_a, HEAD_A)
            vs4 = vs.reshape(db, SAMPLE_ROWS, h_a, 2 * HEAD_A)
            q_rows = qs.reshape(db, SAMPLE_ROWS, 2 * h_a, HEAD_A)[:, :n_new]
            q_rows = q_rows.transpose(0, 2, 1, 3).reshape(db, 2 * h_a * n_new, HEAD_A)
            k_rows = jnp.pad(ks.reshape(db, SAMPLE_ROWS, d).transpose(0, 2, 1),
                             ((0, 0), (0, 0), (0, page - SAMPLE_ROWS)))
            v_rows = jnp.pad(vs4.reshape(db, SAMPLE_ROWS * h_a, 2 * HEAD_A),
                             ((0, 0), (0, (SMALL_CHUNK - SAMPLE_ROWS) * h_a), (0, 0)))
            as_ = _attn_sample(q_rows, k_rows, v_rows, ck, cv, page_table, j, lam, subln,
                               lam_init, n_new, pp)
            w_out = w_out_a[j]
        elif kind == 1:
            w = w_in_b[j]
            n_main_cols = 2 * kd_b + 2 * d
            w_main = w[:, :n_main_cols].astype(BF16)
            w_gl = jnp.pad(w[:, n_main_cols:], ((0, 0), (0, LANES - GATE_RANK))).astype(BF16)
            w_g2 = jnp.pad(w_gate2_b[j], ((0, LANES - GATE_RANK), (0, 0))).astype(BF16)
            bg = b_gate_b[j].reshape(1, kd_b)
            gn = gnorm_b[j].reshape(1, dv_b)
            qp, kp, vp, rp, gp = _proj_b(hp, nw, w_main, w_gl, w_g2, bg, tm_main)
            qs, ks, vs, rs, gs = _proj_b(hs, nw, w_main, w_gl, w_g2, bg, n_small)
            r3 = lambda a: a.reshape(bp, t_all, a.shape[1])
            ap, sp = _recur_prompt(r3(qp), r3(kp), r3(gp), r3(vp), r3(rp), gn, N_META_TOKENS,
                                   GLA_TILING)
            ap = ap.reshape(n_main, d)
            as_, ss = _recur_sample(qs, ks, gs, vs, rs, state_gla[j].astype(F32), gn, n_new)
            gla_p.append(sp)
            gla_s.append(ss)
            w_out = w_out_b[j]
        else:
            w = w_in_c[j].astype(BF16)
            gn = gnorm_c[j].reshape(1, dv_c)
            lbf = lb_c.astype(F32)
            qp, kp, gp, vp, rp = _proj_c(hp, nw, w, lbf, i, tm_main)
            qs, ks, gs, vs, rs = _proj_c(hs, nw, w, lbf, i, n_small)
            r3 = lambda a: a.reshape(bp, t_all, a.shape[1])
            ap, sp = _recur_prompt(r3(qp), r3(kp), r3(gp), r3(vp), r3(rp), gn, N_META_TOKENS,
                                   HGRN_TILING)
            ap = ap.reshape(n_main, d)
            as_, ss = _recur_sample(qs, ks, gs, vs, rs, state_hgrn[j].astype(F32), gn, n_new)
            hgrn_p.append(sp)
            hgrn_s.append(ss)
            w_out = w_out_c[j]
        w_out = w_out.astype(BF16)
        nf = norm_ffn[i].reshape(1, d)
        hp = _mix_mlp(ap, w_out, hp, nf, w_up, w_down, i, tm_mlp, tf)
        hs = _mix_mlp(as_, w_out, hs, nf, w_up, w_down, i, n_small, tf)

    nfin = norm_final.reshape(1, d)
    y_prompt = _final_norm_prompt(hp.reshape(bp, t_all, d), nfin, N_META_TOKENS)
    y_sample = _final_norm(hs, nfin).reshape(db, SAMPLE_ROWS, d)[:, :n_new]
    k_sample, v_sample = (a.reshape(n_a, db, SAMPLE_ROWS, d)[:, :, :n_new] for a in kv_sample)
    return (y_prompt, y_sample,
            kv_prompt[0].reshape(n_a, bp, t_all, 2 * h_a, HEAD_A),
            kv_prompt[1].reshape(n_a, bp, t_all, h_a, 2 * HEAD_A),
            k_sample.reshape(n_a, db, n_new, 2 * h_a, HEAD_A),
            v_sample.reshape(n_a, db, n_new, h_a, 2 * HEAD_A),
            jnp.stack(gla_p), jnp.stack(gla_s), jnp.stack(hgrn_p), jnp.stack(hgrn_s))
```

```python
import functools
import math

import numpy as np
import jax
import jax.numpy as jnp
from jax import lax
from jax.experimental import pallas as pl
from jax.experimental.pallas import tpu as pltpu

F32 = jnp.float32
BF16 = jnp.bfloat16

RMS_EPS = 1e-6
N_META_TOKENS = 16
N_MIXERS = 3
HEAD_A = 64
ROT_DIM = HEAD_A // 4
ROPE_THETA = 500000.0
H_B = 4
GATE_RANK = 16
GATE_TEMP = 16.0
DK = 128
SAMPLE_ROWS = 8
SMALL_CHUNK = 16
GLA_TILING = (64, 2, 8)
HGRN_TILING = (128, 4, 4)
PROJ_ROWS = 704
PROJ_COLS = 512
MLP_ROWS = 1400
MLP_HIDDEN = 512
ATTN_Q_ROWS = 512
ATTN_HEADS = 4
PAGES_PER_STEP = 16
LANES = 128
ONES_ROWS = 16
VMEM_LIMIT = 56 * 1024 * 1024
LOG2E = 1.4426950408889634

NT_DIMS = (((1,), (1,)), ((), ()))
TN_DIMS = (((0,), (0,)), ((), ()))


def _cparams(*sem):
    return pltpu.CompilerParams(dimension_semantics=sem, vmem_limit_bytes=VMEM_LIMIT)


def _pick_tile(n, target, mult=16):
    best = None
    for t in range(mult, min(n, target) + 1, mult):
        if n % t == 0:
            best = t
    return n if best is None else best


def _rms(x, w):
    ms = jnp.mean(x * x, axis=-1, keepdims=True)
    return x * lax.rsqrt(ms + RMS_EPS) * w


def _sigmoid(x):
    return 1.0 / (1.0 + jnp.exp(-x))


def _proj_a_kernel(x_ref, nw_ref, w_ref, c_ref, sm_ref, sp_ref, *rest, cw):
    q_ref, k_ref, v_ref = rest[-3:]
    xn = _rms(x_ref[...], nw_ref[...]).astype(BF16)
    d = q_ref.shape[1]
    rep = cw // LANES
    cos = jnp.tile(c_ref[...], (1, rep))
    sin_m = jnp.tile(sm_ref[...], (1, rep))
    sin_p = jnp.tile(sp_ref[...], (1, rep))
    for dst_i, dst in enumerate((q_ref, k_ref, v_ref)):
        for c in range(d // cw):
            col = dst_i * d + c * cw
            y = jnp.dot(xn, w_ref[:, col:col + cw], preferred_element_type=F32)
            if dst_i < 2:
                y = (y * cos + pltpu.roll(y, ROT_DIM // 2, 1) * sin_p
                     + pltpu.roll(y, cw - ROT_DIM // 2, 1) * sin_m)
            dst[:, c * cw:(c + 1) * cw] = y


def _proj_b_kernel(x_ref, nw_ref, w_ref, wgl_ref, wg2_ref, bg_ref,
                   q_ref, k_ref, v_ref, r_ref, g_ref, *, cw):
    xn = _rms(x_ref[...], nw_ref[...]).astype(BF16)
    col = 0
    for dst, scale in ((q_ref, DK ** -0.5), (k_ref, None), (v_ref, None), (r_ref, None)):
        for c in range(dst.shape[1] // cw):
            y = jnp.dot(xn, w_ref[:, col:col + cw], preferred_element_type=F32)
            if scale is not None:
                y = y * scale
            dst[:, c * cw:(c + 1) * cw] = y.astype(dst.dtype)
            col += cw
    gl = jnp.dot(xn, wgl_ref[...], preferred_element_type=F32).astype(BF16)
    z = jnp.dot(gl, wg2_ref[...], preferred_element_type=F32) + bg_ref[...]
    g_ref[...] = (jnp.minimum(z, 0.0) - jnp.log(1.0 + jnp.exp(-jnp.abs(z)))) * (1.0 / GATE_TEMP)


def _proj_c_kernel(x_ref, nw_ref, w_ref, lb_ref, q_ref, k_ref, g_ref, v_ref, r_ref, *, cw, layer):
    xn = _rms(x_ref[...], nw_ref[...]).astype(BF16)
    d = q_ref.shape[1]
    lbs = lb_ref[...]
    mx = jnp.max(lbs, axis=0, keepdims=True)
    e = jnp.exp(lbs - mx)
    lb = jnp.sum(e[1:layer + 1], axis=0, keepdims=True) / jnp.sum(e, axis=0, keepdims=True)
    for c in range(d // cw):
        sl = slice(c * cw, (c + 1) * cw)
        y = jnp.dot(xn, w_ref[:, c * cw:(c + 1) * cw], preferred_element_type=F32)
        q_ref[:, sl] = y * _sigmoid(y) * (DK ** -0.5)
        f = jnp.dot(xn, w_ref[:, d + c * cw:d + (c + 1) * cw], preferred_element_type=F32)
        lbc = lb[:, sl]
        forget = lbc + (1.0 - lbc) * _sigmoid(f)
        k_ref[:, sl] = 1.0 - forget
        g_ref[:, sl] = jnp.log(forget)
        v_ref[:, sl] = jnp.dot(xn, w_ref[:, 2 * d + c * cw:2 * d + (c + 1) * cw],
                               preferred_element_type=F32).astype(v_ref.dtype)
        r_ref[:, sl] = jnp.dot(xn, w_ref[:, 3 * d + c * cw:3 * d + (c + 1) * cw],
                               preferred_element_type=F32)


def _row_spec(tm, width):
    return pl.BlockSpec((tm, width), lambda i: (i, 0))


def _full_spec(shape):
    return pl.BlockSpec(shape, lambda i: (0,) * len(shape))


def _proj_a(x, nw, w, tables, tm, table_blocks, n_layers, layer_j, kv_prev):
    n, d = x.shape
    cw = min(PROJ_COLS, d)
    tab_spec = pl.BlockSpec((tm, LANES), lambda i: (i % table_blocks, 0))
    kv_shape = jax.ShapeDtypeStruct((n_layers, n, d), F32)
    kv_spec = pl.BlockSpec((None, tm, d), lambda i: (layer_j, i, 0))
    in_specs = [_row_spec(tm, d), _full_spec((1, d)), _full_spec(w.shape),
                tab_spec, tab_spec, tab_spec]
    args = [x, nw, w, *tables]
    aliases = {}
    if kv_prev is not None:
        aliases = {len(args): 1, len(args) + 1: 2}
        in_specs += [pl.BlockSpec(memory_space=pl.ANY)] * 2
        args += list(kv_prev)
    return pl.pallas_call(
        functools.partial(_proj_a_kernel, cw=cw),
        out_shape=(jax.ShapeDtypeStruct((n, d), F32), kv_shape, kv_shape),
        grid=(n // tm,),
        in_specs=in_specs,
        out_specs=(_row_spec(tm, d), kv_spec, kv_spec),
        input_output_aliases=aliases,
        compiler_params=_cparams("parallel"),
        name="proj_a",
    )(*args)


def _proj_b(x, nw, w, wgl, wg2, bg, tm):
    n, d = x.shape
    kd, vd = wg2.shape[1], d
    cw = min(PROJ_COLS, kd)
    return pl.pallas_call(
        functools.partial(_proj_b_kernel, cw=cw),
        out_shape=(jax.ShapeDtypeStruct((n, kd), F32), jax.ShapeDtypeStruct((n, kd), F32),
                   jax.ShapeDtypeStruct((n, vd), BF16), jax.ShapeDtypeStruct((n, vd), F32),
                   jax.ShapeDtypeStruct((n, kd), F32)),
        grid=(n // tm,),
        in_specs=[_row_spec(tm, d), _full_spec((1, d)), _full_spec(w.shape),
                  _full_spec(wgl.shape), _full_spec(wg2.shape), _full_spec((1, kd))],
        out_specs=(_row_spec(tm, kd), _row_spec(tm, kd), _row_spec(tm, vd), _row_spec(tm, vd),
                   _row_spec(tm, kd)),
        compiler_params=_cparams("parallel"),
        name="proj_b",
    )(x, nw, w, wgl, wg2, bg)


def _proj_c(x, nw, w, lb_c, layer, tm):
    n, d = x.shape
    cw = min(PROJ_COLS, d)
    out = jax.ShapeDtypeStruct((n, d), F32)
    return pl.pallas_call(
        functools.partial(_proj_c_kernel, cw=cw, layer=layer),
        out_shape=(out, out, out, jax.ShapeDtypeStruct((n, d), BF16), out),
        grid=(n // tm,),
        in_specs=[_row_spec(tm, d), _full_spec((1, d)), _full_spec(w.shape),
                  _full_spec(lb_c.shape)],
        out_specs=(_row_spec(tm, d),) * 5,
        compiler_params=_cparams("parallel"),
        name="proj_c",
    )(x, nw, w, lb_c)


def _lambda_full(lam_ref, lam_init):
    l = lam_ref[...]
    a = jnp.sum(l[0:1] * l[1:2], axis=-1, keepdims=True)
    b = jnp.sum(l[2:3] * l[3:4], axis=-1, keepdims=True)
    return jnp.exp(a) - jnp.exp(b) + lam_init


def _head_finish(o, subln, lam_init):
    ms = jnp.mean(o * o, axis=-1, keepdims=True)
    return o * lax.rsqrt(ms + RMS_EPS) * subln * (1.0 - lam_init)


def _attn_prompt_kernel(lam_ref, subln_ref, q_ref, k_ref, v_ref, o_ref,
                        kb_ref, vt_ref, vt2_ref, km_ref, vtm_ref, m_ref, acc_ref, bias_ref,
                        *, tq, n_meta, lam_init, hb):
    t_all = q_ref.shape[0]
    nq = (t_all - n_meta) // tq
    hw = 2 * HEAD_A
    lam = _lambda_full(lam_ref, lam_init)
    subln = subln_ref[...]
    first_map = lax.broadcasted_iota(jnp.int32, (1, hw), 1) < HEAD_A

    def stack_q(qt, scale):
        qt = qt * scale
        return jnp.concatenate([jnp.where(first_map, qt, 0.0), jnp.where(first_map, 0.0, qt)],
                               axis=0).astype(BF16)

    r = lax.broadcasted_iota(jnp.int32, (2 * n_meta, n_meta), 0)
    c = lax.broadcasted_iota(jnp.int32, (2 * n_meta, n_meta), 1)
    meta_causal = c <= jnp.where(r >= n_meta, r - n_meta, r)
    pad = jnp.zeros((hw - n_meta, hw), F32)
    for h in range(hb):
        lanes = slice(h * hw, (h + 1) * hw)
        kmeta = k_ref[0:n_meta, lanes]
        vmeta = v_ref[0:n_meta, lanes]
        qs = stack_q(q_ref[0:n_meta, lanes], HEAD_A ** -0.5)
        s = lax.dot_general(qs, kmeta.astype(BF16), NT_DIMS, preferred_element_type=F32)
        s = jnp.where(meta_causal, s, -jnp.inf)
        p = jnp.exp(s - jnp.max(s, axis=-1, keepdims=True))
        acc = jnp.dot(p.astype(BF16), vmeta.astype(BF16), preferred_element_type=F32)
        acc = acc / jnp.sum(p, axis=-1, keepdims=True)
        o = acc[:n_meta] - lam * acc[n_meta:]
        o_ref[0:n_meta, lanes] = _head_finish(o, subln, lam_init).astype(o_ref.dtype)
        km_ref[h] = kmeta.astype(BF16)
        vmeta_t = jnp.concatenate([vmeta, pad], axis=0).T[:, :n_meta]
        vtm_ref[h] = jnp.concatenate(
            [vmeta_t.astype(BF16), jnp.ones((ONES_ROWS, n_meta), BF16)], axis=0)
        for cidx in range(nq):
            rows = slice(n_meta + cidx * tq, n_meta + (cidx + 1) * tq)
            kb_ref[h, cidx] = k_ref[rows, lanes].astype(BF16)
            vte = jnp.concatenate([v_ref[rows, lanes].T.astype(BF16),
                                   jnp.ones((ONES_ROWS, tq), BF16)], axis=0)
            vt_ref[h, cidx] = vte
            if cidx < 2 * (nq // 2):
                vt2_ref[h, cidx // 2, :, (cidx % 2) * tq:(cidx % 2 + 1) * tq] = vte

    key_i = lax.broadcasted_iota(jnp.int32, (n_meta + tq, 2 * tq), 0)
    qry_i = lax.broadcasted_iota(jnp.int32, (n_meta + tq, 2 * tq), 1)
    key_limit = jnp.where(key_i < n_meta, n_meta,
                          n_meta + jnp.where(qry_i >= tq, qry_i - tq, qry_i))
    bias_ref[...] = jnp.where(key_i <= key_limit, 0.0, -jnp.inf)

    def online(h, s, vte):
        m_old = m_ref[h]
        m_new = jnp.maximum(m_old, jnp.max(s, axis=0, keepdims=True))
        alpha = jnp.exp2(m_old - m_new)
        p = jnp.exp2(s - m_new).astype(BF16)
        acc_ref[h] = alpha * acc_ref[h] + jnp.dot(vte, p, preferred_element_type=F32)
        m_ref[h] = m_new

    def q_body(qi, _):
        q0 = n_meta + qi * tq
        qss = [stack_q(q_ref[pl.ds(q0, tq), h * hw:(h + 1) * hw], (HEAD_A ** -0.5) * LOG2E)
               for h in range(hb)]
        odd = qi % 2 == 1
        for h in range(hb):
            s = jnp.concatenate(
                [lax.dot_general(km_ref[h], qss[h], NT_DIMS, preferred_element_type=F32),
                 lax.dot_general(kb_ref[h, qi], qss[h], NT_DIMS, preferred_element_type=F32)],
                axis=0)
            s = s + bias_ref[...]
            if odd:
                s = jnp.concatenate(
                    [s, lax.dot_general(kb_ref[h, qi - 1], qss[h], NT_DIMS,
                                        preferred_element_type=F32)], axis=0)
            m = jnp.max(s, axis=0, keepdims=True)
            p = jnp.exp2(s - m).astype(BF16)
            m_ref[h] = m
            acc = (jnp.dot(vtm_ref[h], p[:n_meta], preferred_element_type=F32)
                   + jnp.dot(vt_ref[h, qi], p[n_meta:n_meta + tq], preferred_element_type=F32))
            if odd:
                acc = acc + jnp.dot(vt_ref[h, qi - 1], p[n_meta + tq:],
                                    preferred_element_type=F32)
            acc_ref[h] = acc

        def pair_body(j, _):
            for h in range(hb):
                kpair = kb_ref[h, pl.ds(2 * j, 2)].reshape(2 * tq, hw)
                s = lax.dot_general(kpair, qss[h], NT_DIMS, preferred_element_type=F32)
                online(h, s, vt2_ref[h, j])
            return 0

        for j in range(qi // 2):
            pair_body(j, 0)

        for h in range(hb):
            acc = acc_ref[h]
            ot = acc[:hw] * (1.0 / acc[hw:hw + 1])
            od = ot[:, :tq] - lam * ot[:, tq:]
            ms = jnp.mean(od * od, axis=0, keepdims=True)
            od = od * lax.rsqrt(ms + RMS_EPS)
            o_ref[pl.ds(q0, tq), h * hw:(h + 1) * hw] = (
                od.T * (subln * (1.0 - lam_init))).astype(o_ref.dtype)
        return 0

    for qi in range(nq):
        q_body(qi, 0)


def _attn_prompt(q, k, v, layer_j, lam, subln, lam_init, n_meta, hb):
    b, t, d = q.shape
    hw = 2 * HEAD_A
    tq = _pick_tile(t - n_meta, ATTN_Q_ROWS, LANES)
    nq = (t - n_meta) // tq
    blk = pl.BlockSpec((None, t, hb * hw), lambda i, h: (i, 0, h))
    kvblk = pl.BlockSpec((None, None, t, hb * hw), lambda i, h: (layer_j, i, 0, h))
    return pl.pallas_call(
        functools.partial(_attn_prompt_kernel, tq=tq, n_meta=n_meta, lam_init=lam_init, hb=hb),
        out_shape=jax.ShapeDtypeStruct((b, t, d), BF16),
        grid=(b, d // (hb * hw)),
        in_specs=[pl.BlockSpec(lam.shape, lambda i, h: (0, 0)),
                  pl.BlockSpec((1, hw), lambda i, h: (0, 0)), blk, kvblk, kvblk],
        out_specs=blk,
        scratch_shapes=[pltpu.VMEM((hb, nq, tq, hw), BF16),
                        pltpu.VMEM((hb, nq, hw + ONES_ROWS, tq), BF16),
                        pltpu.VMEM((hb, max(nq // 2, 1), hw + ONES_ROWS, 2 * tq), BF16),
                        pltpu.VMEM((hb, n_meta, hw), BF16),
                        pltpu.VMEM((hb, hw + ONES_ROWS, n_meta), BF16),
                        pltpu.VMEM((hb, 1, 2 * tq), F32),
                        pltpu.VMEM((hb, hw + ONES_ROWS, 2 * tq), F32),
                        pltpu.VMEM((n_meta + tq, 2 * tq), F32)],
        compiler_params=_cparams("parallel", "parallel"),
        name="attn_prompt",
    )(lam, subln, q, k, v)


def _attn_sample_kernel(pt_ref, lam_ref, subln_ref, q_ref, kn_ref, vn_ref, *rest,
                        pp, n_new, lam_init):
    del pt_ref
    k_refs, v_refs = rest[:pp], rest[pp:2 * pp]
    o_ref, qbd_ref, m_ref, l_ref, acc_ref = rest[2 * pp:]
    step = pl.program_id(1)
    rows = q_ref.shape[0]
    n_heads = rows // n_new
    n_vheads = n_heads // 2
    page = k_refs[0].shape[1]
    grp = 4 * n_new

    def scores(k_list):
        qbd = qbd_ref[...]
        return jnp.concatenate(
            [jnp.dot(qbd, kr[...].astype(BF16), preferred_element_type=F32) for kr in k_list],
            axis=1)

    def values(p, v_list):
        out = []
        for g in range(rows // grp):
            pg = p[g * grp:(g + 1) * grp]
            halves = []
            for vh in (2 * g, 2 * g + 1):
                r = None
                for i, vr in enumerate(v_list):
                    tokens = vr.shape[0] // n_vheads
                    v = vr[pl.ds(vh, tokens, stride=n_vheads), :].astype(BF16)
                    t = jnp.dot(pg[:, i * page:i * page + tokens], v,
                                preferred_element_type=F32)
                    r = t if r is None else r + t
                halves.append(r)
            out.append(halves[0][:grp // 2])
            out.append(halves[1][grp // 2:])
        return jnp.concatenate(out, axis=0)

    @pl.when(step == 0)
    def _():
        d = n_heads * HEAD_A
        q = jnp.tile(q_ref[...] * ((HEAD_A ** -0.5) * LOG2E), (1, n_heads))
        row_head = lax.broadcasted_iota(jnp.int32, (rows, d), 0) // n_new
        lane_head = lax.broadcasted_iota(jnp.int32, (rows, d), 1) // HEAD_A
        qbd_ref[...] = jnp.where(row_head == lane_head, q, 0.0).astype(BF16)
        s = scores([kn_ref])
        tok = lax.broadcasted_iota(jnp.int32, (rows, page), 0) % n_new
        key = lax.broadcasted_iota(jnp.int32, (rows, page), 1)
        s = jnp.where(key <= tok, s, -jnp.inf)
        m = jnp.max(s, axis=-1, keepdims=True)
        p = jnp.exp2(s - m)
        m_ref[...] = m
        l_ref[...] = jnp.sum(p, axis=-1, keepdims=True)
        acc_ref[...] = values(p.astype(BF16), [vn_ref])

    s = scores(k_refs)
    m_old = m_ref[...]
    m_new = jnp.maximum(m_old, jnp.max(s, axis=-1, keepdims=True))
    alpha = jnp.exp2(m_old - m_new)
    p = jnp.exp2(s - m_new)
    l_ref[...] = alpha * l_ref[...] + jnp.sum(p, axis=-1, keepdims=True)
    acc_ref[...] = alpha * acc_ref[...] + values(p.astype(BF16), v_refs)
    m_ref[...] = m_new

    @pl.when(step == pl.num_programs(1) - 1)
    def _():
        lam = _lambda_full(lam_ref, lam_init)
        subln = subln_ref[...]
        hw = 2 * HEAD_A
        full = acc_ref[...] / l_ref[...]
        diff = full - lam * pltpu.roll(full, rows - n_new, 0)
        for h in range(n_heads // 2):
            tile = diff[2 * n_new * h:2 * n_new * (h + 1)]
            o_ref[:, h * hw:(h + 1) * hw] = _head_finish(tile, subln, lam_init).astype(o_ref.dtype)


def _attn_sample(q, k_new, v_new, cache_k, cache_v, page_table, layer_j, lam, subln, lam_init,
                 n_new, pp):
    nb, rows, _ = q.shape
    n_pages = page_table.shape[1]
    krows, page = cache_k.shape[2:]
    vrows = cache_v.shape[2]
    hw = 2 * HEAD_A
    d = (rows // n_new) * HEAD_A
    assert 2 * n_new == SAMPLE_ROWS and krows == d and vrows * 2 * HEAD_A == page * d

    def per_sample(shape):
        return pl.BlockSpec((None,) + shape, lambda b, s, pt: (b, 0, 0))

    def kpage(i):
        return pl.BlockSpec((None, None, krows, page),
                            lambda b, s, pt: (layer_j, pt[b, s * pp + i], 0, 0))

    def vpage(i):
        return pl.BlockSpec((None, None, vrows, hw),
                            lambda b, s, pt: (layer_j, pt[b, s * pp + i], 0, 0))

    grid_spec = pltpu.PrefetchScalarGridSpec(
        num_scalar_prefetch=1,
        grid=(nb, n_pages // pp),
        in_specs=[pl.BlockSpec(lam.shape, lambda b, s, pt: (0, 0)),
                  pl.BlockSpec((1, hw), lambda b, s, pt: (0, 0)),
                  per_sample(q.shape[1:]), per_sample(k_new.shape[1:]),
                  per_sample(v_new.shape[1:])]
                 + [kpage(i) for i in range(pp)] + [vpage(i) for i in range(pp)],
        out_specs=pl.BlockSpec((SAMPLE_ROWS, d), lambda b, s, pt: (b, 0)),
        scratch_shapes=[pltpu.VMEM((rows, d), BF16), pltpu.VMEM((rows, 1), F32),
                        pltpu.VMEM((rows, 1), F32), pltpu.VMEM((rows, hw), F32)],
    )
    return pl.pallas_call(
        functools.partial(_attn_sample_kernel, pp=pp, n_new=n_new, lam_init=lam_init),
        out_shape=jax.ShapeDtypeStruct((nb * SAMPLE_ROWS, d), BF16),
        grid_spec=grid_spec,
        compiler_params=_cparams("parallel", "arbitrary"),
        name="attn_sample",
    )(page_table, lam, subln, q, k_new, v_new, *([cache_k] * pp), *([cache_v] * pp))


def _chunk_constants(length):
    nlev = int(math.log2(length))
    assert 2 ** nlev == length
    w = np.zeros(((2 + nlev) * length, length), np.float32)
    masks = np.zeros((nlev + 1, length, length), np.float32)
    masks[0] = np.eye(length)
    for t in range(length):
        w[t, :t + 1] = 1.0
        w[length + t, t + 1:] = 1.0
    for lvl in range(1, nlev + 1):
        bs, half = 2 ** lvl, 2 ** (lvl - 1)
        for t in range(length):
            mid = t - t % bs + half
            row = (1 + lvl) * length + t
            if t >= mid:
                w[row, mid:t + 1] = 1.0
                masks[lvl, t, mid - half:mid] = 1.0
            else:
                w[row, t + 1:mid] = 1.0
    return jnp.asarray(np.tile(w, (1, 3)), BF16), jnp.asarray(masks, F32)


def _chunk_exponents(g, w3):
    g = g * LOG2E
    g1 = g.astype(BF16)
    rem = g - g1.astype(F32)
    g2 = rem.astype(BF16)
    g3 = (rem - g2.astype(F32)).astype(BF16)
    return jnp.exp2(jnp.dot(w3, jnp.concatenate([g1, g2, g3], axis=0),
                            preferred_element_type=F32))


def _chunk_head(q, k, vb, st, ex, masks):
    length = q.shape[0]
    nlev = masks.shape[0] - 1
    e_cum = ex[0:length]
    e_rem = ex[length:2 * length]
    e_last = e_cum[length - 1:length]
    o = lax.dot_general((q * e_cum).astype(BF16), st.astype(BF16), NT_DIMS,
                        preferred_element_type=F32)
    qb, kb = q.astype(BF16), k.astype(BF16)
    a = masks[0] * lax.dot_general(qb, kb, NT_DIMS, preferred_element_type=F32)
    row = lax.broadcasted_iota(jnp.int32, q.shape, 0)
    for lvl in range(1, nlev + 1):
        second_half = (row & (2 ** lvl - 1)) >= 2 ** (lvl - 1)
        x = (jnp.where(second_half, q, k) * ex[(1 + lvl) * length:(2 + lvl) * length]).astype(BF16)
        a = a + masks[lvl] * lax.dot_general(x, x, NT_DIMS, preferred_element_type=F32)
    o = o + jnp.dot(a.astype(BF16), vb, preferred_element_type=F32)
    kd = (k * e_rem).astype(BF16)
    st = st * e_last + lax.dot_general(vb, kd, TN_DIMS, preferred_element_type=F32)
    return o, st


def _gate_norm(o, r, gn):
    ms = jnp.mean(o * o, axis=-1, keepdims=True)
    return o * lax.rsqrt(ms + RMS_EPS) * gn * (r * _sigmoid(r))


def _recur_prompt_kernel(gn_ref, ws_ref, ms_ref, wm_ref, mm_ref, q_ref, k_ref, g_ref, v_ref, r_ref,
                         o_ref, s_ref, st_ref, *, n_meta, chunk, hb, unroll):
    t_all = q_ref.shape[0]
    dv = v_ref.shape[1] // hb
    gn = gn_ref[...]
    st_ref[...] = jnp.zeros_like(st_ref)

    def run(r0, length, w3, masks):
        rows = pl.ds(r0, length)
        ex = _chunk_exponents(g_ref[rows], w3)
        for h in range(hb):
            ksl = slice(h * DK, (h + 1) * DK)
            vsl = slice(h * dv, (h + 1) * dv)
            o, st = _chunk_head(q_ref[rows, ksl], k_ref[rows, ksl], v_ref[rows, vsl], st_ref[h],
                                ex[:, ksl], masks)
            st_ref[h] = st
            o_ref[rows, vsl] = _gate_norm(o, r_ref[rows, vsl], gn).astype(o_ref.dtype)

    run(0, n_meta, ws_ref[...], ms_ref[...])

    def body(c, _):
        run(pl.multiple_of(n_meta + c * chunk, 16), chunk, wm_ref[...], mm_ref[...])
        return 0

    lax.fori_loop(0, (t_all - n_meta) // chunk, body, 0, unroll=unroll)
    for h in range(hb):
        s_ref[h] = st_ref[h].T


def _recur_prompt(q, k, g, v, r, gn, n_meta, tiling):
    chunk, hb, unroll = tiling
    b, t, kd = q.shape
    vd = v.shape[2]
    nh = kd // DK
    dv = vd // nh
    ws, ms = _chunk_constants(n_meta)
    wm, mm = _chunk_constants(chunk)
    kblk = pl.BlockSpec((None, t, hb * DK), lambda i, h: (i, 0, h))
    vblk = pl.BlockSpec((None, t, hb * dv), lambda i, h: (i, 0, h))

    def const(x):
        return pl.BlockSpec(x.shape, lambda i, h: (0,) * x.ndim)

    return pl.pallas_call(
        functools.partial(_recur_prompt_kernel, n_meta=n_meta, chunk=chunk, hb=hb,
                          unroll=unroll),
        out_shape=(jax.ShapeDtypeStruct((b, t, vd), BF16),
                   jax.ShapeDtypeStruct((b, nh, DK, dv), F32)),
        grid=(b, nh // hb),
        in_specs=[pl.BlockSpec((1, dv), lambda i, h: (0, 0)), const(ws), const(ms), const(wm),
                  const(mm), kblk, kblk, kblk, vblk, vblk],
        out_specs=(vblk, pl.BlockSpec((None, hb, DK, dv), lambda i, h: (i, h, 0, 0))),
        scratch_shapes=[pltpu.VMEM((hb, dv, DK), F32)],
        compiler_params=_cparams("parallel", "parallel"),
        name="recur_prompt",
    )(gn, ws, ms, wm, mm, q, k, g, v, r)


def _recur_sample_kernel(gn_ref, w_ref, m_ref, q_ref, k_ref, g_ref, v_ref, r_ref, s0_ref,
                         o_ref, s_ref, *, n_new):
    nh = s0_ref.shape[0]
    dv = s0_ref.shape[2]
    gn = gn_ref[...]
    masks = m_ref[...]
    pad_rows = SMALL_CHUNK - SAMPLE_ROWS
    kd = q_ref.shape[1]
    valid = lax.broadcasted_iota(jnp.int32, (SAMPLE_ROWS, kd), 0) < n_new

    def padded(x):
        return jnp.concatenate([x, jnp.zeros((pad_rows, x.shape[1]), x.dtype)], axis=0)

    q = padded(q_ref[...])
    k = padded(jnp.where(valid, k_ref[...], 0.0))
    ex = _chunk_exponents(padded(jnp.where(valid, g_ref[...], 0.0)), w_ref[...])
    vb = padded(v_ref[...].astype(F32)).astype(BF16)
    for h in range(nh):
        ksl = slice(h * DK, (h + 1) * DK)
        vsl = slice(h * dv, (h + 1) * dv)
        o, st = _chunk_head(q[:, ksl], k[:, ksl], vb[:, vsl], s0_ref[h].T, ex[:, ksl], masks)
        s_ref[h] = st.T
        o_ref[:, vsl] = _gate_norm(o[0:SAMPLE_ROWS], r_ref[:, vsl], gn).astype(o_ref.dtype)


def _recur_sample(q, k, g, v, r, s0, gn, n_new):
    n, kd = q.shape
    vd = v.shape[1]
    nb, nh, _, dv = s0.shape
    w, masks = _chunk_constants(SMALL_CHUNK)
    kblk = pl.BlockSpec((SAMPLE_ROWS, kd), lambda i: (i, 0))
    vblk = pl.BlockSpec((SAMPLE_ROWS, vd), lambda i: (i, 0))
    sblk = pl.BlockSpec((None, nh, DK, dv), lambda i: (i, 0, 0, 0))
    return pl.pallas_call(
        functools.partial(_recur_sample_kernel, n_new=n_new),
        out_shape=(jax.ShapeDtypeStruct((n, vd), BF16), jax.ShapeDtypeStruct(s0.shape, F32)),
        grid=(nb,),
        in_specs=[_full_spec((1, dv)), _full_spec(w.shape), _full_spec(masks.shape),
                  kblk, kblk, kblk, vblk, vblk, sblk],
        out_specs=(vblk, sblk),
        compiler_params=_cparams("parallel"),
        name="recur_sample",
    )(gn, w, masks, q, k, g, v, r, s0)


def _mix_mlp_kernel(a_ref, wo_ref, x_ref, nw_ref, wu_ref, wd_ref, o_ref, xn_ref):
    j = pl.program_id(1)

    @pl.when(j == 0)
    def _():
        h = x_ref[...] + jnp.dot(a_ref[...], wo_ref[...], preferred_element_type=F32)
        xn_ref[...] = _rms(h, nw_ref[...]).astype(BF16)
        o_ref[...] = h

    u = jnp.dot(xn_ref[...], wu_ref[...].astype(BF16), preferred_element_type=F32)
    u = jnp.square(jnp.maximum(u, 0.0)).astype(BF16)
    o_ref[...] += jnp.dot(u, wd_ref[...].astype(BF16), preferred_element_type=F32)


def _mix_mlp(a, wo, x, nw, wu, wd, layer, tm, tf):
    n, d = x.shape
    ff = wu.shape[2]
    return pl.pallas_call(
        _mix_mlp_kernel,
        out_shape=jax.ShapeDtypeStruct((n, d), F32),
        grid=(n // tm, ff // tf),
        in_specs=[pl.BlockSpec((tm, a.shape[1]), lambda i, j: (i, 0)),
                  pl.BlockSpec(wo.shape, lambda i, j: (0, 0)),
                  pl.BlockSpec((tm, d), lambda i, j: (i, 0)),
                  pl.BlockSpec((1, d), lambda i, j: (0, 0)),
                  pl.BlockSpec((None, d, tf), lambda i, j: (layer, 0, j)),
                  pl.BlockSpec((None, tf, d), lambda i, j: (layer, j, 0))],
        out_specs=pl.BlockSpec((tm, d), lambda i, j: (i, 0)),
        scratch_shapes=[pltpu.VMEM((tm, d), BF16)],
        compiler_params=_cparams("parallel", "arbitrary"),
        name="mix_mlp",
    )(a, wo, x, nw, wu, wd)


def _final_norm_prompt_kernel(x_ref, nw_ref, o_ref, *, n_meta):
    tm = o_ref.shape[0]
    r0 = pl.multiple_of(n_meta + pl.program_id(1) * tm, 8)
    o_ref[...] = _rms(x_ref[pl.ds(r0, tm)], nw_ref[...])


def _final_norm_prompt(x, nw, n_meta):
    b, t, d = x.shape
    seq = t - n_meta
    tm = _pick_tile(seq, 512)
    return pl.pallas_call(
        functools.partial(_final_norm_prompt_kernel, n_meta=n_meta),
        out_shape=jax.ShapeDtypeStruct((b, seq, d), F32),
        grid=(b, seq // tm),
        in_specs=[pl.BlockSpec((None, t, d), lambda i, j: (i, 0, 0)),
                  pl.BlockSpec((1, d), lambda i, j: (0, 0))],
        out_specs=pl.BlockSpec((None, tm, d), lambda i, j: (i, j, 0)),
        compiler_params=_cparams("parallel", "arbitrary"),
        name="final_norm_prompt",
    )(x, nw)


def _final_norm_kernel(x_ref, nw_ref, o_ref):
    o_ref[...] = _rms(x_ref[...], nw_ref[...])


def _final_norm(x, nw):
    return pl.pallas_call(
        _final_norm_kernel,
        out_shape=jax.ShapeDtypeStruct(x.shape, F32),
        name="final_norm",
    )(x, nw)


def _rope_tables(pos):
    half = ROT_DIM // 2
    inv_freq = ROPE_THETA ** (-jnp.arange(half, dtype=F32) / half)
    ang = pos.astype(F32)[:, None] * inv_freq[None, :]
    cos, sin = jnp.cos(ang), jnp.sin(ang)
    n = pos.shape[0]
    zeros = jnp.zeros((n, half), F32)
    rest0 = jnp.zeros((n, HEAD_A - ROT_DIM), F32)
    cos_h = jnp.concatenate([cos, cos, jnp.ones((n, HEAD_A - ROT_DIM), F32)], axis=1)
    sin_m = jnp.concatenate([-sin, zeros, rest0], axis=1)
    sin_p = jnp.concatenate([zeros, sin, rest0], axis=1)
    return tuple(jnp.concatenate([t, t], axis=1) for t in (cos_h, sin_m, sin_p))


def kernel(x_prompt, x_sample, cache_k, cache_v, state_gla, state_hgrn, page_table, meta_tokens,
           norm_mix, norm_ffn, norm_final, w_in_a, lam_a, subln_a, w_out_a, w_in_b, w_gate2_b,
           b_gate_b, gnorm_b, w_out_b, w_in_c, lb_c, gnorm_c, w_out_c, w_up, w_down):
    bp, seq, d = x_prompt.shape
    db, n_new = x_sample.shape[:2]
    depth = norm_mix.shape[0]
    n_pages = page_table.shape[1]
    page = cache_k.shape[2]
    past_len = n_pages * page
    t_all = N_META_TOKENS + seq
    n_main = bp * t_all
    n_small = db * SAMPLE_ROWS
    h_a = d // (2 * HEAD_A)
    kd_b = w_gate2_b.shape[2]
    dv_b = d // H_B
    h_c = d // DK
    dv_c = d // h_c

    hp = jnp.concatenate(
        [jnp.broadcast_to(meta_tokens.astype(F32)[None], (bp, N_META_TOKENS, d)), x_prompt],
        axis=1).reshape(n_main, d)
    hs = jnp.pad(x_sample, ((0, 0), (0, SAMPLE_ROWS - n_new), (0, 0))).reshape(n_small, d)

    tm_main = _pick_tile(t_all, PROJ_ROWS)
    tm_mlp = _pick_tile(n_main, MLP_ROWS)
    tf = MLP_HIDDEN
    tab_p = _rope_tables(jnp.arange(t_all))
    tab_s = _rope_tables(jnp.tile(past_len + jnp.arange(SAMPLE_ROWS), db))
    ck = cache_k.transpose(0, 1, 3, 4, 2).reshape(cache_k.shape[0], cache_k.shape[1], d, page)
    cv = cache_v.reshape(cache_v.shape[0], cache_v.shape[1], page * h_a, 2 * HEAD_A)
    pp = PAGES_PER_STEP if n_pages % PAGES_PER_STEP == 0 else 1

    n_a = w_in_a.shape[0]
    kv_prompt = kv_sample = None
    gla_p, gla_s, hgrn_p, hgrn_s = [], [], [], []
    for i in range(depth):
        kind, j = i % N_MIXERS, i // N_MIXERS
        nw = norm_mix[i].reshape(1, d)
        if kind == 0:
            lam_init = 0.8 - 0.6 * math.exp(-0.3 * i)
            w = w_in_a[j].astype(BF16)
            lam = lam_a[j].astype(F32)
            subln = subln_a[j].reshape(1, 2 * HEAD_A)
            qp, *kv_prompt = _proj_a(hp, nw, w, tab_p, tm_main, t_all // tm_main, n_a, j,
                                     kv_prompt)
            qs, *kv_sample = _proj_a(hs, nw, w, tab_s, n_small, 1, n_a, j, kv_sample)
            ks, vs = kv_sample[0][j], kv_sample[1][j]
            shp = (n_a, bp, t_all, d)
            ap = _attn_prompt(qp.reshape(shp[1:]), kv_prompt[0].reshape(shp),
                              kv_prompt[1].reshape(shp), j, lam, subln, lam_init,
                              N_META_TOKENS, ATTN_HEADS).reshape(n_main, d)
            ks4 = ks.reshape(db, SAMPLE_ROWS, 2 * h_a, HEAD_A)
            vs4 = vs.reshape(db, SAMPLE_ROWS, h_a, 2 * HEAD_A)
            q_rows = qs.reshape(db, SAMPLE_ROWS, 2 * h_a, HEAD_A)[:, :n_new]
            q_rows = q_rows.transpose(0, 2, 1, 3).reshape(db, 2 * h_a * n_new, HEAD_A)
            k_rows = jnp.pad(ks.reshape(db, SAMPLE_ROWS, d).transpose(0, 2, 1),
                             ((0, 0), (0, 0), (0, page - SAMPLE_ROWS)))
            v_rows = jnp.pad(vs4.reshape(db, SAMPLE_ROWS * h_a, 2 * HEAD_A),
                             ((0, 0), (0, (SMALL_CHUNK - SAMPLE_ROWS) * h_a), (0, 0)))
            as_ = _attn_sample(q_rows, k_rows, v_rows, ck, cv, page_table, j, lam, subln,
                               lam_init, n_new, pp)
            w_out = w_out_a[j]
        elif kind == 1:
            w = w_in_b[j]
            n_main_cols = 2 * kd_b + 2 * d
            w_main = w[:, :n_main_cols].astype(BF16)
            w_gl = jnp.pad(w[:, n_main_cols:], ((0, 0), (0, LANES - GATE_RANK))).astype(BF16)
            w_g2 = jnp.pad(w_gate2_b[j], ((0, LANES - GATE_RANK), (0, 0))).astype(BF16)
            bg = b_gate_b[j].reshape(1, kd_b)
            gn = gnorm_b[j].reshape(1, dv_b)
            qp, kp, vp, rp, gp = _proj_b(hp, nw, w_main, w_gl, w_g2, bg, tm_main)
            qs, ks, vs, rs, gs = _proj_b(hs, nw, w_main, w_gl, w_g2, bg, n_small)
            r3 = lambda a: a.reshape(bp, t_all, a.shape[1])
            ap, sp = _recur_prompt(r3(qp), r3(kp), r3(gp), r3(vp), r3(rp), gn, N_META_TOKENS,
                                   GLA_TILING)
            ap = ap.reshape(n_main, d)
            as_, ss = _recur_sample(qs, ks, gs, vs, rs, state_gla[j].astype(F32), gn, n_new)
            gla_p.append(sp)
            gla_s.append(ss)
            w_out = w_out_b[j]
        else:
            w = w_in_c[j].astype(BF16)
            gn = gnorm_c[j].reshape(1, dv_c)
            lbf = lb_c.astype(F32)
            qp, kp, gp, vp, rp = _proj_c(hp, nw, w, lbf, i, tm_main)
            qs, ks, gs, vs, rs = _proj_c(hs, nw, w, lbf, i, n_small)
            r3 = lambda a: a.reshape(bp, t_all, a.shape[1])
            ap, sp = _recur_prompt(r3(qp), r3(kp), r3(gp), r3(vp), r3(rp), gn, N_META_TOKENS,
                                   HGRN_TILING)
            ap = ap.reshape(n_main, d)
            as_, ss = _recur_sample(qs, ks, gs, vs, rs, state_hgrn[j].astype(F32), gn, n_new)
            hgrn_p.append(sp)
            hgrn_s.append(ss)
            w_out = w_out_c[j]
        w_out = w_out.astype(BF16)
        nf = norm_ffn[i].reshape(1, d)
        hp = _mix_mlp(ap, w_out, hp, nf, w_up, w_down, i, tm_mlp, tf)
        hs = _mix_mlp(as_, w_out, hs, nf, w_up, w_down, i, n_small, tf)

    nfin = norm_final.reshape(1, d)
    y_prompt = _final_norm_prompt(hp.reshape(bp, t_all, d), nfin, N_META_TOKENS)
    y_sample = _final_norm(hs, nfin).reshape(db, SAMPLE_ROWS, d)[:, :n_new]
    k_sample, v_sample = (a.reshape(n_a, db, SAMPLE_ROWS, d)[:, :, :n_new] for a in kv_sample)
    return (y_prompt, y_sample,
            kv_prompt[0].reshape(n_a, bp, t_all, 2 * h_a, HEAD_A),
            kv_prompt[1].reshape(n_a, bp, t_all, h_a, 2 * HEAD_A),
            k_sample.reshape(n_a, db, n_new, 2 * h_a, HEAD_A),
            v_sample.reshape(n_a, db, n_new, h_a, 2 * HEAD_A),
            jnp.stack(gla_p), jnp.stack(gla_s), jnp.stack(hgrn_p), jnp.stack(hgrn_s))
```
